```python
import math
import jax, jax.numpy as jnp
from jax import lax
import numpy as np

D_MODEL = 1024
BATCH = 16
SEQ = 2048
DEPTH = 2
DEC_BATCH = 32
DEC_SEQ = 1
PAST_LEN = 16384
PAGE_SIZE = 128

N_HEADS = 16
HEAD_DIM = 64
WINDOWS = (128, 512, 2048)
DILATIONS = (1, 4, 16)
N_DGROUPS = 3
BAND = 128
N_BUCKETS = 32
MAX_DISTANCE = 2048
POOL_WINDOWS = (2, 4, 8, 16)
POOL_CH = D_MODEL // 4
POOL_STATE = max(POOL_WINDOWS) - 1
N_EGROUPS = 4
EXPERTS_PER_GROUP = 4
N_EXPERTS = N_EGROUPS * EXPERTS_PER_GROUP
TOP_K = 2
D_EXPERT = D_MODEL // 2
MOE_BLOCK = 1024
N_ATT = (DEPTH + 1) // 2
N_POOL = DEPTH // 2
RMS_EPS = 1e-6
NEG = -1e30

kernel_name = 'hybrid_dilated_pool_hmoe_step'


def rms_norm(x, g):
    xf = x.astype(jnp.float32)
    y = xf * lax.rsqrt(jnp.mean(xf * xf, axis=-1, keepdims=True) + RMS_EPS) * g.astype(jnp.float32)
    return y.astype(x.dtype)


def t5_bucket(dist):
    max_exact = N_BUCKETS // 2
    df = jnp.maximum(dist, max_exact).astype(jnp.float32)
    large = max_exact + (jnp.log(df / max_exact) / math.log(MAX_DISTANCE / max_exact)
                         * (N_BUCKETS - max_exact)).astype(jnp.int32)
    large = jnp.minimum(large, N_BUCKETS - 1)
    return jnp.where(dist < max_exact, dist, large)


def project_qkv(u, w_qkv, q_gain, k_gain):
    b, t, _ = u.shape
    z = jnp.einsum('btd,df->btf', u, w_qkv).reshape(b, t, N_DGROUPS, 3, N_HEADS, HEAD_DIM)
    q = rms_norm(z[:, :, :, 0], q_gain[:, None, :])
    k = rms_norm(z[:, :, :, 1], k_gain[:, None, :])
    return q, k, z[:, :, :, 2]


def softmax_parts(s, valid):
    s = jnp.where(valid, s, NEG)
    m = jnp.max(s, axis=-1, keepdims=True)
    p = jnp.exp(s - m)
    den = jnp.sum(p, axis=-1)
    return p, den, m[..., 0] + jnp.log(den)


def dilated_band(q, k, v, window, dilation, bias_tab):
    b, s_len, h, e = q.shape
    n_step = window // dilation
    n_sub = -(-s_len // dilation)
    nb = -(-n_sub // BAND)
    lp = nb * BAND
    pad = lp * dilation - s_len

    def to_sub(t):
        t = jnp.pad(t, ((0, 0), (0, pad), (0, 0), (0, 0)))
        return t.reshape(b, lp, dilation, h, e).transpose(0, 2, 1, 3, 4).reshape(b, dilation, nb, BAND, h, e)

    def with_prev(t):
        prev = jnp.pad(t[:, :, :-1], ((0, 0), (0, 0), (1, 0), (0, 0), (0, 0), (0, 0)))
        return jnp.concatenate([prev, t], axis=3)

    def from_sub(t):
        t = t.reshape((b, dilation, lp) + t.shape[4:])
        t = jnp.swapaxes(t, 1, 2)
        return t.reshape((b, lp * dilation) + t.shape[3:])[:, :s_len]

    qs = to_sub(q)
    ks = with_prev(to_sub(k))
    vs = with_prev(to_sub(v))
    qi = jnp.arange(BAND)[:, None]
    kj = jnp.arange(2 * BAND)[None, :]
    step = BAND + qi - kj
    blk = jnp.arange(nb)[:, None, None]
    valid = (step >= 0) & (step <= n_step) & ((blk > 0) | (kj >= BAND))
    bias = bias_tab.astype(jnp.float32)[t5_bucket(jnp.maximum(step, 0) * dilation)]
    sc = jnp.einsum('brnqhe,brnkhe->brnhqk', qs, ks, preferred_element_type=jnp.float32) * (HEAD_DIM ** -0.5)
    sc = sc + jnp.transpose(bias, (2, 0, 1))
    p, den, lse = softmax_parts(sc, valid[:, None])
    o = jnp.einsum('brnhqk,brnkhe->brnqhe', p, vs.astype(jnp.float32)) / jnp.moveaxis(den, 3, 4)[..., None]
    return from_sub(o), from_sub(jnp.moveaxis(lse, 3, 4))


def dilated_gather(q, k_ext, v_ext, buf_len, window, dilation, bias_tab):
    t_new = q.shape[1]
    n_step = window // dilation
    steps = jnp.arange(n_step + 1)
    idx = buf_len + jnp.arange(t_new)[:, None] - steps[None, :] * dilation
    valid = idx >= 0
    idx = jnp.maximum(idx, 0)
    kg = k_ext[:, idx]
    vg = v_ext[:, idx]
    bias = bias_tab.astype(jnp.float32)[t5_bucket(steps * dilation)].T
    sc = jnp.einsum('bthe,btmhe->bhtm', q, kg, preferred_element_type=jnp.float32) * (HEAD_DIM ** -0.5)
    sc = sc + bias[:, None, :]
    p, den, lse = softmax_parts(sc, valid)
    o = jnp.einsum('bhtm,btmhe->bthe', p, vg.astype(jnp.float32)) / jnp.swapaxes(den, 1, 2)[..., None]
    return o, jnp.swapaxes(lse, 1, 2)


def merge_groups(outs, lses):
    w = jax.nn.softmax(jnp.stack(lses), axis=0)
    return jnp.sum(w[..., None] * jnp.stack(outs), axis=0)


def attn_prompt(u, w_qkv, q_gain, k_gain, w_o, rel_bias):
    b, t, _ = u.shape
    q, k, v = project_qkv(u, w_qkv, q_gain, k_gain)
    outs, lses, new_kv = [], [], []
    for g, (win, dil) in enumerate(zip(WINDOWS, DILATIONS)):
        tab = rel_bias[:, g * N_HEADS:(g + 1) * N_HEADS]
        o, l = dilated_band(q[:, :, g], k[:, :, g], v[:, :, g], win, dil, tab)
        outs.append(o)
        lses.append(l)
        keep = min(win, t)
        new_kv.append(jnp.stack([k[:, t - keep:, g], v[:, t - keep:, g]], axis=2))
    o = merge_groups(outs, lses).reshape(b, t, N_HEADS * HEAD_DIM).astype(u.dtype)
    return jnp.einsum('btf,fd->btd', o, w_o), new_kv


def attn_sample(u, bufs, w_qkv, q_gain, k_gain, w_o, rel_bias):
    b, t, _ = u.shape
    q, k, v = project_qkv(u, w_qkv, q_gain, k_gain)
    outs, lses, new_kv = [], [], []
    for g, (win, dil) in enumerate(zip(WINDOWS, DILATIONS)):
        buf = bufs[g]
        buf_len = buf.shape[1]
        k_ext = jnp.concatenate([buf[:, :, 0], k[:, :, g].astype(buf.dtype)], axis=1)
        v_ext = jnp.concatenate([buf[:, :, 1], v[:, :, g].astype(buf.dtype)], axis=1)
        tab = rel_bias[:, g * N_HEADS:(g + 1) * N_HEADS]
        o, l = dilated_gather(q[:, :, g], k_ext, v_ext, buf_len, win, dil, tab)
        outs.append(o)
        lses.append(l)
        new_kv.append(jnp.stack([k_ext[:, -buf_len:], v_ext[:, -buf_len:]], axis=2))
    o = merge_groups(outs, lses).reshape(b, t, N_HEADS * HEAD_DIM).astype(u.dtype)
    return jnp.einsum('btf,fd->btd', o, w_o), new_kv


def pool_mix(u_ext, n_out, w, bias, scale):
    bsz, length, _ = u_ext.shape
    uf = u_ext.astype(jnp.float32)
    cs = jnp.concatenate([jnp.zeros_like(uf[:, :1]), jnp.cumsum(uf, axis=1)], axis=1)
    e = jnp.arange(length - n_out, length)
    parts = []
    for gi, win in enumerate(POOL_WINDOWS):
        sl = slice(gi * POOL_CH, (gi + 1) * POOL_CH)
        cg = cs[:, :, sl]
        total = cg[:, e + 1] - cg[:, jnp.maximum(e + 1 - win, 0)]
        cnt = jnp.minimum(win, e + 1).astype(jnp.float32)
        parts.append(total / cnt[None, :, None] - uf[:, length - n_out:, sl])
    z = jnp.stack(parts, axis=2)
    y = jnp.einsum('bngc,gcf->bngf', z, w.astype(jnp.float32)).reshape(bsz, n_out, D_MODEL)
    y = (y + bias.astype(jnp.float32)) * scale.astype(jnp.float32)
    return y.astype(u_ext.dtype)


def route(u, wgr, bgr, wer, ber):
    uf = u.astype(jnp.float32)
    pg = jax.nn.softmax(uf @ wgr.astype(jnp.float32) + bgr.astype(jnp.float32), axis=-1)
    gi = jnp.argmax(pg, axis=-1)
    gw = jnp.max(pg, axis=-1)
    le = (uf @ wer.astype(jnp.float32) + ber.astype(jnp.float32)).reshape(-1, N_EGROUPS, EXPERTS_PER_GROUP)
    le_sel = jnp.take_along_axis(le, gi[:, None, None], axis=1)[:, 0]
    tv, ti = lax.top_k(le_sel, TOP_K)
    pw = jax.nn.softmax(tv, axis=-1) * gw[:, None]
    eid = gi[:, None] * EXPERTS_PER_GROUP + ti
    return jnp.sum(jax.nn.one_hot(eid, N_EXPERTS, dtype=jnp.float32) * pw[..., None], axis=1)


def moe_ffn(h, g_norm, wgr, bgr, wer, ber, wg, wu, wd):
    b, t, d = h.shape
    n = b * t
    u = rms_norm(h, g_norm).reshape(n, d)
    gates = route(u, wgr, bgr, wer, ber).astype(u.dtype)

    def experts(args):
        ub, gb = args
        a = jnp.einsum('td,edf->tef', ub, wg)
        c = jnp.einsum('td,edf->tef', ub, wu)
        hh = jax.nn.silu(a) * c * gb[:, :, None]
        return jnp.einsum('tef,efd->td', hh, wd)

    if n > MOE_BLOCK and n % MOE_BLOCK == 0:
        out = lax.map(experts, (u.reshape(-1, MOE_BLOCK, d), gates.reshape(-1, MOE_BLOCK, N_EXPERTS))).reshape(n, d)
    else:
        out = experts((u, gates))
    return h + out.reshape(b, t, d).astype(h.dtype)


def setup_inputs(seed: int = 0) -> dict:
    key = jax.random.key(seed)
    ks = jax.random.split(key, 24)

    def nrm(k, shape, s=1.0):
        return s * jax.random.normal(k, shape, jnp.float32)

    def gain(k, shape):
        return 1.0 + 0.02 * jax.random.normal(k, shape, jnp.float32)

    qkv_cols = N_DGROUPS * 3 * N_HEADS * HEAD_DIM
    hd = N_HEADS * HEAD_DIM
    return {
        'x_prompt': nrm(ks[0], (BATCH, SEQ, D_MODEL)),
        'x_sample': nrm(ks[1], (DEC_BATCH, DEC_SEQ, D_MODEL)),
        'cache_kv_w128': nrm(ks[2], (N_ATT, DEC_BATCH, min(WINDOWS[0], PAST_LEN), 2, N_HEADS, HEAD_DIM)),
        'cache_kv_w512': nrm(ks[3], (N_ATT, DEC_BATCH, min(WINDOWS[1], PAST_LEN), 2, N_HEADS, HEAD_DIM)),
        'cache_kv_w2048': nrm(ks[4], (N_ATT, DEC_BATCH, min(WINDOWS[2], PAST_LEN), 2, N_HEADS, HEAD_DIM)),
        'state_pool': nrm(ks[5], (N_POOL, DEC_BATCH, POOL_STATE, D_MODEL)),
        'w_qkv': nrm(ks[6], (N_ATT, D_MODEL, qkv_cols), D_MODEL ** -0.5),
        'q_norm': gain(ks[7], (N_ATT, N_DGROUPS, HEAD_DIM)),
        'k_norm': gain(ks[8], (N_ATT, N_DGROUPS, HEAD_DIM)),
        'w_o': nrm(ks[9], (N_ATT, hd, D_MODEL), hd ** -0.5),
        'rel_bias': nrm(ks[10], (N_BUCKETS, N_DGROUPS * N_HEADS), 0.1),
        'attn_norm': gain(ks[11], (N_ATT, D_MODEL)),
        'pool_norm': gain(ks[12], (N_POOL, D_MODEL)),
        'pool_w': nrm(ks[13], (N_POOL, len(POOL_WINDOWS), POOL_CH, POOL_CH), POOL_CH ** -0.5),
        'pool_b': nrm(ks[14], (N_POOL, D_MODEL), 0.01),
        'pool_scale': gain(ks[15], (N_POOL, D_MODEL)),
        'ffn_norm': gain(ks[16], (DEPTH, D_MODEL)),
        'router_group_w': nrm(ks[17], (DEPTH, D_MODEL, N_EGROUPS), D_MODEL ** -0.5),
        'router_group_b': nrm(ks[18], (DEPTH, N_EGROUPS), 0.01),
        'router_expert_w': nrm(ks[19], (DEPTH, D_MODEL, N_EXPERTS), D_MODEL ** -0.5),
        'router_expert_b': nrm(ks[20], (DEPTH, N_EXPERTS), 0.01),
        'w_gate': nrm(ks[21], (DEPTH, N_EXPERTS, D_MODEL, D_EXPERT), D_MODEL ** -0.5),
        'w_up': nrm(ks[22], (DEPTH, N_EXPERTS, D_MODEL, D_EXPERT), D_MODEL ** -0.5),
        'w_down': nrm(ks[23], (DEPTH, N_EXPERTS, D_EXPERT, D_MODEL), D_EXPERT ** -0.5),
    }


def reference(x_prompt, x_sample, cache_kv_w128, cache_kv_w512, cache_kv_w2048, state_pool,
              w_qkv, q_norm, k_norm, w_o, rel_bias, attn_norm, pool_norm, pool_w, pool_b,
              pool_scale, ffn_norm, router_group_w, router_group_b, router_expert_w,
              router_expert_b, w_gate, w_up, w_down):
    hp, hs = x_prompt, x_sample
    kv_p = [[], [], []]
    kv_s = [[], [], []]
    pool_p, pool_s = [], []
    for i in range(DEPTH):
        if i % 2 == 0:
            a = i // 2
            up = rms_norm(hp, attn_norm[a])
            us = rms_norm(hs, attn_norm[a])
            yp, newp = attn_prompt(up, w_qkv[a], q_norm[a], k_norm[a], w_o[a], rel_bias)
            ys, news = attn_sample(us, (cache_kv_w128[a], cache_kv_w512[a], cache_kv_w2048[a]),
                                   w_qkv[a], q_norm[a], k_norm[a], w_o[a], rel_bias)
            for g in range(N_DGROUPS):
                kv_p[g].append(newp[g])
                kv_s[g].append(news[g])
        else:
            p = i // 2
            up = rms_norm(hp, pool_norm[p])
            us = rms_norm(hs, pool_norm[p])
            yp = pool_mix(up, up.shape[1], pool_w[p], pool_b[p], pool_scale[p])
            pool_p.append(up[:, -POOL_STATE:])
            u_ext = jnp.concatenate([state_pool[p].astype(us.dtype), us], axis=1)
            ys = pool_mix(u_ext, us.shape[1], pool_w[p], pool_b[p], pool_scale[p])
            pool_s.append(u_ext[:, -POOL_STATE:])
        hp = hp + yp
        hs = hs + ys
        hp = moe_ffn(hp, ffn_norm[i], router_group_w[i], router_group_b[i], router_expert_w[i],
                     router_expert_b[i], w_gate[i], w_up[i], w_down[i])
        hs = moe_ffn(hs, ffn_norm[i], router_group_w[i], router_group_b[i], router_expert_w[i],
                     router_expert_b[i], w_gate[i], w_up[i], w_down[i])
    new_kv_w128_prompt = jnp.stack(kv_p[0])
    new_kv_w512_prompt = jnp.stack(kv_p[1])
    new_kv_w2048_prompt = jnp.stack(kv_p[2])
    new_pool_prompt = jnp.stack(pool_p)
    new_kv_w128_sample = jnp.stack(kv_s[0])
    new_kv_w512_sample = jnp.stack(kv_s[1])
    new_kv_w2048_sample = jnp.stack(kv_s[2])
    new_pool_sample = jnp.stack(pool_s)
    return (hp, hs, new_kv_w128_prompt, new_kv_w512_prompt, new_kv_w2048_prompt, new_pool_prompt,
            new_kv_w128_sample, new_kv_w512_sample, new_kv_w2048_sample, new_pool_sample)
```

```python
import functools
import math

import numpy as np
import jax
import jax.numpy as jnp
from jax import lax
from jax.experimental import pallas as pl
from jax.experimental.pallas import tpu as pltpu

f32 = jnp.float32
bf16 = jnp.bfloat16
u32 = jnp.uint32
i32 = jnp.int32

D_MODEL = 1024
N_HEADS = 16
HEAD_DIM = 64
HD = N_HEADS * HEAD_DIM
N_DGROUPS = 3
WINDOWS = (128, 512, 2048)
DILATIONS = (1, 4, 16)
BAND = 128
N_BUCKETS = 32
MAX_DISTANCE = 2048
POOL_WINDOWS = (2, 4, 8, 16)
POOL_CH = D_MODEL // 4
POOL_STATE = max(POOL_WINDOWS) - 1
N_EGROUPS = 4
EXPERTS_PER_GROUP = 4
N_EXPERTS = 16
D_EXPERT = D_MODEL // 2
RMS_EPS = 1e-6
NEG = -1e30

LANES = 128
SUBLANES = 8
HI_MASK = 0xFFFF0000
N_CLASSES = N_EGROUPS * 6
PLANES_PER_KIND = HD // (2 * LANES)
VMEM_LIMIT = 56 * 1024 * 1024

_PAIRS = [(a, b) for a in range(4) for b in range(a + 1, 4)]
_CLASS_E1 = np.array([g * 4 + _PAIRS[p][0] for g in range(4) for p in range(6)], np.int32)
_CLASS_E2 = np.array([g * 4 + _PAIRS[p][1] for g in range(4) for p in range(6)], np.int32)


def _cparams(sem=None, vmem=VMEM_LIMIT):
    kw = dict(vmem_limit_bytes=vmem)
    if sem is not None:
        kw["dimension_semantics"] = sem
    return pltpu.CompilerParams(**kw)


def _dot(a, b):
    return jnp.dot(a, b, preferred_element_type=f32)


def _rms(x, g):
    return x * lax.rsqrt(jnp.mean(x * x, axis=-1, keepdims=True) + RMS_EPS) * g


def _pack_words(lo, hi):
    lb = lax.bitcast_convert_type(lo.astype(bf16).astype(f32), u32) >> 16
    hb = lax.bitcast_convert_type(hi.astype(bf16).astype(f32), u32) & jnp.uint32(HI_MASK)
    return lb | hb


def _unpack_words(w):
    lo = lax.bitcast_convert_type(w << 16, f32)
    hi = lax.bitcast_convert_type(w & jnp.uint32(HI_MASK), f32)
    return lo, hi


def _head_norm(z, gain, s_ref, e2_ref):
    zz = z * z
    zz_hi = zz.astype(bf16)
    zz_lo = (zz - zz_hi.astype(f32)).astype(bf16)
    ss = _dot(zz_hi, s_ref[...]) + _dot(zz_lo, s_ref[...])
    r = lax.rsqrt(ss * (1.0 / HEAD_DIM) + RMS_EPS)
    r_hi = r.astype(bf16)
    r_lo = (r - r_hi.astype(f32)).astype(bf16)
    rb = _dot(jnp.concatenate([r_hi, r_lo], axis=1), e2_ref[...])
    return z * rb * gain


def _qkv_kernel(x_ref, gn_ref, w_ref, gq_ref, gk_ref, s_ref, e2_ref,
                planes_ref, kv2_ref, kv1_ref, kv0_ref, *, tm):
    j = pl.program_id(0)
    u = _rms(x_ref[...], gn_ref[...]).astype(bf16)
    z = _dot(u, w_ref[...])
    q = _head_norm(z[:, :HD], gq_ref[0], s_ref, e2_ref) * (HEAD_DIM ** -0.5)
    k = _head_norm(z[:, HD:2 * HD], gk_ref[0], s_ref, e2_ref)
    v = z[:, 2 * HD:]
    for kind, arr in enumerate((q, k, v)):
        for p in range(PLANES_PER_KIND):
            c0 = 2 * LANES * p
            planes_ref[kind * PLANES_PER_KIND + p] = _pack_words(
                arr[:, c0:c0 + LANES], arr[:, c0 + LANES:c0 + 2 * LANES])

    @pl.when(j == 2)
    def _():
        kv2_ref[:, :HD] = k
        kv2_ref[:, HD:] = v

    @pl.when(j == 1)
    def _():
        kv1_ref[:, :HD] = k
        kv1_ref[:, HD:] = v

    @pl.when(j == 0)
    def _():
        kv0_ref[:, :HD] = k[tm - BAND:, :]
        kv0_ref[:, HD:] = v[tm - BAND:, :]


def _qkv_prompt(x2, gn, w_bf, gq, gk, smat, e2mat, batch, seq, tm=256):
    n = x2.shape[0]
    nt = n // tm
    tps = seq // tm
    t512 = WINDOWS[1] // tm
    assert seq % tm == 0 and WINDOWS[1] % tm == 0 and tm >= BAND

    def kv1_map(j, i):
        b = i // tps
        r = i % tps
        inside = jnp.maximum(r - (tps - t512), 0)
        return (jnp.where(j == 1, b * t512 + inside, jnp.where(j < 1, 0, batch * t512 - 1)), 0)

    def kv0_map(j, i):
        return (jnp.where(j == 0, i // tps, batch - 1), 0)

    nplanes = 3 * PLANES_PER_KIND
    return pl.pallas_call(
        functools.partial(_qkv_kernel, tm=tm),
        grid=(N_DGROUPS, nt),
        in_specs=[
            pl.BlockSpec((tm, D_MODEL), lambda j, i: (i, 0)),
            pl.BlockSpec((1, D_MODEL), lambda j, i: (0, 0)),
            pl.BlockSpec((D_MODEL, 3 * HD), lambda j, i: (0, j)),
            pl.BlockSpec((1, 1, HD), lambda j, i: (j, 0, 0)),
            pl.BlockSpec((1, 1, HD), lambda j, i: (j, 0, 0)),
            pl.BlockSpec((HD, LANES), lambda j, i: (0, 0)),
            pl.BlockSpec((2 * LANES, HD), lambda j, i: (0, 0)),
        ],
        out_specs=[
            pl.BlockSpec((nplanes, tm, LANES), lambda j, i: (j, i, 0)),
            pl.BlockSpec((tm, 2 * HD), lambda j, i: (jnp.where(j == 2, i, 0), 0)),
            pl.BlockSpec((tm, 2 * HD), kv1_map),
            pl.BlockSpec((BAND, 2 * HD), kv0_map),
        ],
        out_shape=[
            jax.ShapeDtypeStruct((N_DGROUPS * nplanes, n, LANES), u32),
            jax.ShapeDtypeStruct((n, 2 * HD), f32),
            jax.ShapeDtypeStruct((batch * WINDOWS[1], 2 * HD), f32),
            jax.ShapeDtypeStruct((batch * WINDOWS[0], 2 * HD), f32),
        ],
        compiler_params=_cparams(("arbitrary", "arbitrary")),
        name="qkv_prompt",
    )(x2, gn, w_bf, gq, gk, smat, e2mat)


def _qkv_sample_kernel(x_ref, gn_ref, w_ref, gq_ref, gk_ref, s_ref, e2_ref, q_ref, k_ref, v_ref):
    u = _rms(x_ref[...], gn_ref[...]).astype(bf16)
    z = _dot(u, w_ref[...])
    q_ref[0] = _head_norm(z[:, :HD], gq_ref[0], s_ref, e2_ref) * (HEAD_DIM ** -0.5)
    k_ref[0] = _head_norm(z[:, HD:2 * HD], gk_ref[0], s_ref, e2_ref)
    v_ref[0] = z[:, 2 * HD:]


def _qkv_sample(xs, gn, w_bf, gq, gk, smat, e2mat):
    rows = xs.shape[0]
    out = jax.ShapeDtypeStruct((N_DGROUPS, rows, HD), f32)
    ospec = pl.BlockSpec((1, rows, HD), lambda j: (j, 0, 0))
    return pl.pallas_call(
        _qkv_sample_kernel,
        grid=(N_DGROUPS,),
        in_specs=[
            pl.BlockSpec((rows, D_MODEL), lambda j: (0, 0)),
            pl.BlockSpec((1, D_MODEL), lambda j: (0, 0)),
            pl.BlockSpec((D_MODEL, 3 * HD), lambda j: (0, j)),
            pl.BlockSpec((1, 1, HD), lambda j: (j, 0, 0)),
            pl.BlockSpec((1, 1, HD), lambda j: (j, 0, 0)),
            pl.BlockSpec((HD, LANES), lambda j: (0, 0)),
            pl.BlockSpec((2 * LANES, HD), lambda j: (0, 0)),
        ],
        out_specs=[ospec, ospec, ospec],
        out_shape=[out, out, out],
        compiler_params=_cparams(("arbitrary",)),
        name="qkv_sample",
    )(xs, gn, w_bf, gq, gk, smat, e2mat)


def _attn_heads(qw, kcw, kpw, vcw, vpw, bias_fn, prev):
    q2 = _unpack_words(qw)
    kc2 = _unpack_words(kcw)
    vc2 = _unpack_words(vcw)
    if kpw is not None:
        kp2 = _unpack_words(kpw)
        vp2 = _unpack_words(vpw)
    lane = lax.broadcasted_iota(i32, (BAND, LANES), 1)
    left = lane < HEAD_DIM
    outs, lses = [], []
    for pr in range(2):
        q = q2[pr].astype(bf16)
        if kpw is not None:
            kk = jnp.concatenate([kp2[pr], kc2[pr]], axis=0).astype(bf16)
            vv = jnp.concatenate([vp2[pr], vc2[pr]], axis=0).astype(bf16)
        else:
            kk = kc2[pr].astype(bf16)
            vv = vc2[pr].astype(bf16)
        halves = []
        for hf in range(2):
            hh = 2 * pr + hf
            qm = jnp.where(left if hf == 0 else jnp.logical_not(left), q, jnp.zeros_like(q))
            s = lax.dot_general(qm, kk, (((1,), (1,)), ((), ())), preferred_element_type=f32)
            s = s + bias_fn(hh)
            m = jnp.max(s, axis=1, keepdims=True)
            p = jnp.exp(s - m)
            l = jnp.sum(p, axis=1, keepdims=True)
            o = _dot(p.astype(bf16), vv) / l
            lse = m + jnp.log(l)
            if prev is not None:
                lp = prev[2][:, hh:hh + 1]
                mx = jnp.maximum(lp, lse)
                ea = jnp.exp(lp - mx)
                eb = jnp.exp(lse - mx)
                tot = ea + eb
                o = prev[pr] * (ea / tot) + o * (eb / tot)
                lse = mx + jnp.log(tot)
            halves.append(o)
            lses.append(lse)
        outs.append(jnp.where(left, halves[0], halves[1]))
    zero = jnp.zeros((BAND, LANES), f32)
    lse_tile = jnp.where(lane == 0, lses[0], jnp.where(lane == 1, lses[1],
                         jnp.where(lane == 2, lses[2], jnp.where(lane == 3, lses[3], zero))))
    return outs[0], outs[1], lse_tile


def _attn_kernel(q0, k0, v0, q1, k1, v1, q2, k2, v2, bias_ref, o_ref, acc_lo, acc_hi, lse_ref, *, seq):
    nblk = seq // BAND

    def body0(jb, c):
        st = pl.multiple_of(jb * BAND, BAND)
        sp = pl.multiple_of(jnp.maximum(jb - 1, 0) * BAND, BAND)
        first = jnp.where(jb == 0, 1, 0)
        o_lo, o_hi, lse = _attn_heads(
            q0[0, pl.ds(st, BAND), :], k0[0, pl.ds(st, BAND), :], k0[0, pl.ds(sp, BAND), :],
            v0[0, pl.ds(st, BAND), :], v0[0, pl.ds(sp, BAND), :],
            lambda hh: bias_ref[0, first, hh], None)
        acc_lo[pl.ds(st, BAND), :] = o_lo
        acc_hi[pl.ds(st, BAND), :] = o_hi
        lse_ref[pl.ds(st, BAND), :] = lse
        return c

    lax.fori_loop(0, nblk, body0, 0)

    d1 = DILATIONS[1]
    nb1 = seq // (d1 * BAND)

    def body1(idx, c):
        r = idx // nb1
        jb = idx % nb1
        st = r + jb * (d1 * BAND)
        sp = r + jnp.maximum(jb - 1, 0) * (d1 * BAND)
        first = jnp.where(jb == 0, 1, 0)
        cur = pl.ds(st, BAND, stride=d1)
        prv = pl.ds(sp, BAND, stride=d1)
        o_lo, o_hi, lse = _attn_heads(
            q1[0, cur, :], k1[0, cur, :], k1[0, prv, :], v1[0, cur, :], v1[0, prv, :],
            lambda hh: bias_ref[1, first, hh],
            (acc_lo[cur, :], acc_hi[cur, :], lse_ref[cur, :]))
        acc_lo[cur, :] = o_lo
        acc_hi[cur, :] = o_hi
        lse_ref[cur, :] = lse
        return c

    lax.fori_loop(0, d1 * nb1, body1, 0)

    d2 = DILATIONS[2]
    assert seq == d2 * BAND

    def body2(r, c):
        cur = pl.ds(r, BAND, stride=d2)
        o_lo, o_hi, lse = _attn_heads(
            q2[0, cur, :], k2[0, cur, :], None, v2[0, cur, :], None,
            lambda hh: bias_ref[2, 0, hh, :, pl.ds(BAND, BAND)],
            (acc_lo[cur, :], acc_hi[cur, :], lse_ref[cur, :]))
        acc_lo[cur, :] = o_lo
        acc_hi[cur, :] = o_hi
        lse_ref[cur, :] = lse
        return c

    lax.fori_loop(0, d2, body2, 0)

    o_ref[:, :LANES] = acc_lo[...].astype(bf16)
    o_ref[:, LANES:] = acc_hi[...].astype(bf16)


def _attn_prompt(planes, bias_tab, batch, seq):
    n = batch * seq
    ppk = PLANES_PER_KIND

    def spec(g, kind):
        return pl.BlockSpec((1, seq, LANES), lambda b, p: ((g * 3 + kind) * ppk + p, b, 0))

    in_specs = [spec(g, kind) for g in range(N_DGROUPS) for kind in range(3)]
    in_specs.append(pl.BlockSpec((N_DGROUPS, 2, 4, BAND, 2 * BAND), lambda b, p: (0, 0, p, 0, 0)))
    return pl.pallas_call(
        functools.partial(_attn_kernel, seq=seq),
        grid=(batch, ppk),
        in_specs=in_specs,
        out_specs=pl.BlockSpec((seq, 2 * LANES), lambda b, p: (b, p)),
        out_shape=jax.ShapeDtypeStruct((n, HD), bf16),
        scratch_shapes=[pltpu.VMEM((seq, LANES), f32), pltpu.VMEM((seq, LANES), f32),
                        pltpu.VMEM((seq, LANES), f32)],
        compiler_params=_cparams(("arbitrary", "arbitrary")),
        name="attn_prompt",
    )(*([planes] * 9), bias_tab)


def _attn_sample_kernel(q_ref, kn_ref, vn_ref, c0_ref, c1_ref, c2_ref, bs_ref, b0_ref, o_ref):
    caches = (c0_ref, c1_ref, c2_ref)
    rb = lambda a: a.astype(bf16).astype(f32)
    ms, ls, accs = [], [], []
    for g in range(N_DGROUPS):
        q = rb(q_ref[g, 0])
        kn = rb(kn_ref[g, 0])
        vn = rb(vn_ref[g, 0])
        kc = rb(caches[g][0, 0, :, 0, 0])
        vc = rb(caches[g][0, 0, :, 0, 1])
        s = jnp.sum(kc * q[None], axis=-1, keepdims=True) + bs_ref[g]
        s0 = jnp.sum(kn * q, axis=-1, keepdims=True) + b0_ref[g]
        m = jnp.maximum(jnp.max(s, axis=0), s0)
        p = jnp.exp(s - m[None])
        p0 = jnp.exp(s0 - m)
        ls.append(jnp.sum(p, axis=0) + p0)
        accs.append(jnp.sum(rb(p) * vc, axis=0) + rb(p0) * vn)
        ms.append(m)
    mm = jnp.maximum(jnp.maximum(ms[0], ms[1]), ms[2])
    num = jnp.zeros_like(accs[0])
    den = jnp.zeros_like(ls[0])
    for g in range(N_DGROUPS):
        w = jnp.exp(ms[g] - mm)
        num = num + w * accs[g]
        den = den + w * ls[g]
    o_ref[0] = num / den


def _attn_sample(q4, kn4, vn4, caches, bias_s, bias_s0, nb):
    qspec = pl.BlockSpec((N_DGROUPS, 1, N_HEADS, HEAD_DIM), lambda b: (0, b, 0, 0))
    cspecs = []
    cviews = []
    for g in range(N_DGROUPS):
        d = DILATIONS[g]
        length = caches[g].shape[1]
        assert length % (d * BAND) == 0 and length >= d * BAND
        cviews.append(caches[g].reshape(nb, length // d, d, 2, N_HEADS, HEAD_DIM)[None])
        last = length // (d * BAND) - 1
        cspecs.append(pl.BlockSpec((1, 1, BAND, 1, 2, N_HEADS, HEAD_DIM),
                                   lambda b, last=last: (0, b, last, 0, 0, 0, 0)))
    return pl.pallas_call(
        _attn_sample_kernel,
        grid=(nb,),
        in_specs=[qspec, qspec, qspec] + cspecs + [
            pl.BlockSpec((N_DGROUPS, BAND, N_HEADS, 1), lambda b: (0, 0, 0, 0)),
            pl.BlockSpec((N_DGROUPS, N_HEADS, 1), lambda b: (0, 0, 0)),
        ],
        out_specs=pl.BlockSpec((1, N_HEADS, HEAD_DIM), lambda b: (b, 0, 0)),
        out_shape=jax.ShapeDtypeStruct((nb, N_HEADS, HEAD_DIM), f32),
        compiler_params=_cparams(("arbitrary",)),
        name="attn_sample",
    )(q4, kn4, vn4, *cviews, bias_s, bias_s0)


def _route(u, wr_ref, br_ref):
    lg = _dot(u.astype(bf16), wr_ref[...]) + br_ref[...]
    rows = u.shape[0]
    lane = lax.broadcasted_iota(i32, (rows, LANES), 1)
    lanef = lane.astype(f32)
    big = jnp.float32(1e9)
    isg = lane < N_EGROUPS
    lgg = jnp.where(isg, lg, NEG)
    mg = jnp.max(lgg, axis=1, keepdims=True)
    eg = jnp.where(isg, jnp.exp(lgg - mg), 0.0)
    pg = eg / jnp.sum(eg, axis=1, keepdims=True)
    gw = jnp.max(pg, axis=1, keepdims=True)
    gi = jnp.min(jnp.where(isg & (pg == gw), lanef, big), axis=1, keepdims=True)
    grp_of_lane = ((lane - N_EGROUPS) >> 2).astype(f32)
    sel = (lane >= N_EGROUPS) & (lane < N_EGROUPS + N_EXPERTS) & (grp_of_lane == gi)
    les = jnp.where(sel, lg, NEG)
    v1 = jnp.max(les, axis=1, keepdims=True)
    i1 = jnp.min(jnp.where(sel & (les == v1), lanef, big), axis=1, keepdims=True)
    sel2 = sel & (lanef != i1)
    les2 = jnp.where(sel2, lg, NEG)
    v2 = jnp.max(les2, axis=1, keepdims=True)
    i2 = jnp.min(jnp.where(sel2 & (les2 == v2), lanef, big), axis=1, keepdims=True)
    ex = jnp.exp(v2 - v1)
    den = 1.0 + ex
    pw1 = (1.0 / den) * gw
    pw2 = (ex / den) * gw
    e1 = i1 - N_EGROUPS
    e2 = i2 - N_EGROUPS
    first_low = e1 < e2
    lo = jnp.minimum(e1, e2) - 4.0 * gi
    hi = jnp.maximum(e1, e2) - 4.0 * gi
    pair = jnp.where(lo == 0.0, hi - 1.0, jnp.where(lo == 1.0, hi + 1.0, 5.0))
    cls = gi * 6.0 + pair
    g_lo = jnp.where(first_low, pw1, pw2)
    g_hi = jnp.where(first_low, pw2, pw1)
    zero = jnp.zeros((rows, LANES), f32)
    return jnp.where(lane == 0, cls, jnp.where(lane == 1, g_lo, jnp.where(lane == 2, g_hi, zero)))


def _write_slab(slab_ref, u, meta):
    rows = u.shape[0]
    half = D_MODEL // 2
    for c in range(4):
        w = _pack_words(u[:, c * LANES:(c + 1) * LANES], u[:, half + c * LANES:half + (c + 1) * LANES])
        slab_ref[pl.ds(c, rows, stride=SUBLANES), :] = w
    slab_ref[pl.ds(4, rows, stride=SUBLANES), :] = lax.bitcast_convert_type(meta, u32)
    z = jnp.zeros((rows, LANES), u32)
    for c in range(5, SUBLANES):
        slab_ref[pl.ds(c, rows, stride=SUBLANES), :] = z


def _ffn_prep(h, gn_ref, wr_ref, br_ref, slab_ref, metat_ref):
    u = _rms(h, gn_ref[...])
    meta = _route(u, wr_ref, br_ref)
    _write_slab(slab_ref, u, meta)
    metat_ref[...] = meta.T[:SUBLANES, :]


def _from_slab_f32(ref, rows):
    return jnp.concatenate([ref[pl.ds(c, rows, stride=SUBLANES), :] for c in range(SUBLANES)], axis=1)


def _wo_kernel(o_ref, x_ref, wo_ref, gn_ref, wr_ref, br_ref, h_ref, slab_ref, metat_ref):
    h = x_ref[...] + _dot(o_ref[...], wo_ref[...])
    h_ref[...] = h
    _ffn_prep(h, gn_ref, wr_ref, br_ref, slab_ref, metat_ref)


def _router_specs():
    cmap = lambda i: (0, 0)
    return [pl.BlockSpec((1, D_MODEL), cmap),
            pl.BlockSpec((D_MODEL, LANES), cmap),
            pl.BlockSpec((1, LANES), cmap)]


def _prep_out(n, tm):
    specs = [pl.BlockSpec((tm, D_MODEL), lambda i: (i, 0)),
             pl.BlockSpec((tm * SUBLANES, LANES), lambda i: (i, 0)),
             pl.BlockSpec((SUBLANES, tm), lambda i: (0, i))]
    shapes = [jax.ShapeDtypeStruct((n, D_MODEL), f32),
              jax.ShapeDtypeStruct((n * SUBLANES, LANES), u32),
              jax.ShapeDtypeStruct((SUBLANES, n), f32)]
    return specs, shapes


def _wo_call(o_bf, x2, wo_bf, router, tm):
    n = x2.shape[0]
    ospecs, oshapes = _prep_out(n, tm)
    return pl.pallas_call(
        _wo_kernel,
        grid=(n // tm,),
        in_specs=[pl.BlockSpec((tm, HD), lambda i: (i, 0)),
                  pl.BlockSpec((tm, D_MODEL), lambda i: (i, 0)),
                  pl.BlockSpec((HD, D_MODEL), lambda i: (0, 0))] + _router_specs(),
        out_specs=ospecs,
        out_shape=oshapes,
        compiler_params=_cparams(("arbitrary",)),
        name="wo_residual",
    )(o_bf, x2, wo_bf, *router)


def _rank_kernel(metat_ref, rank_ref, cnt_ref, carry_ref, *, tr):
    @pl.when(pl.program_id(0) == 0)
    def _():
        carry_ref[...] = jnp.zeros_like(carry_ref)

    cls = metat_ref[0:1, :]
    sub = lax.broadcasted_iota(i32, (32, tr), 0).astype(f32)
    oh = jnp.where(sub == cls, 1.0, 0.0)
    rr = lax.broadcasted_iota(i32, (tr, tr), 0)
    cc = lax.broadcasted_iota(i32, (tr, tr), 1)
    upper = jnp.where(rr < cc, 1.0, 0.0).astype(bf16)
    pre = _dot(oh.astype(bf16), upper)
    carry = carry_ref[:, 0:1]
    rank_ref[...] = jnp.sum(oh * (pre + carry), axis=0, keepdims=True).astype(i32)
    newc = carry + jnp.sum(oh, axis=1, keepdims=True)
    carry_ref[...] = jnp.broadcast_to(newc, carry_ref.shape)
    cnt_ref[...] = carry_ref[...]


def _rank_call(metat, tr):
    n = metat.shape[1]
    return pl.pallas_call(
        functools.partial(_rank_kernel, tr=tr),
        grid=(n // tr,),
        in_specs=[pl.BlockSpec((SUBLANES, tr), lambda i: (0, i))],
        out_specs=[pl.BlockSpec((1, tr), lambda i: (0, i)),
                   pl.BlockSpec((32, LANES), lambda i: (0, 0))],
        out_shape=[jax.ShapeDtypeStruct((1, n), i32), jax.ShapeDtypeStruct((32, LANES), f32)],
        scratch_shapes=[pltpu.VMEM((32, LANES), f32)],
        compiler_params=_cparams(("arbitrary",)),
        name="class_rank",
    )(metat)


ROW_MOVE_CHUNK = 1024


def _row_move_kernel(pos_ref, src_ref, *rest, chunk, scatter):
    dst_ref, sem = rest[-2], rest[-1]
    base = pl.program_id(0) * chunk

    def copy(t):
        p = pos_ref[t]
        s_row = t if scatter else p
        d_row = p if scatter else t
        return pltpu.make_async_copy(
            src_ref.at[pl.ds(pl.multiple_of(s_row * SUBLANES, SUBLANES), SUBLANES)],
            dst_ref.at[pl.ds(pl.multiple_of(d_row * SUBLANES, SUBLANES), SUBLANES)], sem)

    def issue(k, c):
        copy(base + k).start()
        return c

    def drain(k, c):
        copy(base + k).wait()
        return c

    lax.fori_loop(0, chunk, issue, 0)
    lax.fori_loop(0, chunk, drain, 0)


def _scatter_rows(pos, slab, dst_init):
    n = pos.shape[0]
    chunk = min(ROW_MOVE_CHUNK, n)
    assert n % chunk == 0
    return pl.pallas_call(
        functools.partial(_row_move_kernel, chunk=chunk, scatter=True),
        grid_spec=pltpu.PrefetchScalarGridSpec(
            num_scalar_prefetch=1, grid=(n // chunk,),
            in_specs=[pl.BlockSpec(memory_space=pl.ANY), pl.BlockSpec(memory_space=pl.ANY)],
            out_specs=pl.BlockSpec(memory_space=pl.ANY),
            scratch_shapes=[pltpu.SemaphoreType.DMA(())]),
        out_shape=jax.ShapeDtypeStruct(dst_init.shape, dst_init.dtype),
        input_output_aliases={2: 0},
        compiler_params=_cparams(("arbitrary",)),
        name="scatter_rows",
    )(pos, slab, dst_init)


def _gather_rows(pos, src, n):
    chunk = min(ROW_MOVE_CHUNK, n)
    assert n % chunk == 0
    return pl.pallas_call(
        functools.partial(_row_move_kernel, chunk=chunk, scatter=False),
        grid_spec=pltpu.PrefetchScalarGridSpec(
            num_scalar_prefetch=1, grid=(n // chunk,),
            in_specs=[pl.BlockSpec(memory_space=pl.ANY)],
            out_specs=pl.BlockSpec(memory_space=pl.ANY),
            scratch_shapes=[pltpu.SemaphoreType.DMA(())]),
        out_shape=jax.ShapeDtypeStruct((n * SUBLANES, LANES), src.dtype),
        compiler_params=_cparams(("arbitrary",)),
        name="gather_rows",
    )(pos, src)


def _expert_kernel(e1_ref, e2_ref, valid_ref, xs_ref, wg1, wu1, wd1, wg2, wu2, wd2, y_ref, *, te):
    del e1_ref, e2_ref
    valid = valid_ref[pl.program_id(0)]

    @pl.when(valid == 0)
    def _():
        y_ref[...] = jnp.zeros_like(y_ref)

    @pl.when(valid > 0)
    def _():
        los, his = [], []
        for c in range(4):
            lo, hi = _unpack_words(xs_ref[pl.ds(c, te, stride=SUBLANES), :])
            los.append(lo)
            his.append(hi)
        x = jnp.concatenate(los + his, axis=1).astype(bf16)
        meta = lax.bitcast_convert_type(xs_ref[pl.ds(4, te, stride=SUBLANES), :], f32)
        y = None
        for wg, wu, wd, lane in ((wg1, wu1, wd1, 1), (wg2, wu2, wd2, 2)):
            a = _dot(x, wg[0])
            c = _dot(x, wu[0])
            hh = (a * (1.0 / (1.0 + jnp.exp(-a)))) * c * meta[:, lane:lane + 1]
            part = _dot(hh.astype(bf16), wd[0])
            y = part if y is None else y + part
        for c in range(SUBLANES):
            y_ref[pl.ds(c, te, stride=SUBLANES), :] = y[:, c * LANES:(c + 1) * LANES]


def _expert_call(te1, te2, tvalid, xs, wg, wu, wd, te):
    ntiles = te1.shape[0]

    def wspec(shape, which):
        if which == 1:
            return pl.BlockSpec(shape, lambda i, e1, e2, v: (e1[i], 0, 0))
        return pl.BlockSpec(shape, lambda i, e1, e2, v: (e2[i], 0, 0))

    gs = (1, D_MODEL, D_EXPERT)
    ds_ = (1, D_EXPERT, D_MODEL)
    return pl.pallas_call(
        functools.partial(_expert_kernel, te=te),
        grid_spec=pltpu.PrefetchScalarGridSpec(
            num_scalar_prefetch=3, grid=(ntiles,),
            in_specs=[pl.BlockSpec((te * SUBLANES, LANES), lambda i, e1, e2, v: (i, 0)),
                      wspec(gs, 1), wspec(gs, 1), wspec(ds_, 1),
                      wspec(gs, 2), wspec(gs, 2), wspec(ds_, 2)],
            out_specs=pl.BlockSpec((te * SUBLANES, LANES), lambda i, e1, e2, v: (i, 0))),
        out_shape=jax.ShapeDtypeStruct((ntiles * te * SUBLANES, LANES), f32),
        compiler_params=_cparams(("arbitrary",)),
        name="expert_pair_ffn",
    )(te1, te2, tvalid, xs, wg, wu, wd, wg, wu, wd)


def _moe(slab, metat, wg, wu, wd, te, tr):
    n = metat.shape[1]
    rank, cnt = _rank_call(metat, tr)
    counts = cnt[:N_CLASSES, 0].astype(i32)
    padded = ((counts + te - 1) // te) * te
    ends = jnp.cumsum(padded)
    off = ends - padded
    cls = metat[0].astype(i32)
    pos = off[cls] + rank[0]
    ntiles = -(-n // te) + N_CLASSES
    tstart = jnp.arange(ntiles, dtype=i32) * te
    tcls = jnp.minimum(jnp.searchsorted(ends, tstart, side="right"), N_CLASSES - 1).astype(i32)
    tvalid = (tstart < ends[-1]).astype(i32)
    te1 = jnp.asarray(_CLASS_E1)[tcls]
    te2 = jnp.asarray(_CLASS_E2)[tcls]
    xs = _scatter_rows(pos, slab, jnp.zeros((ntiles * te * SUBLANES, LANES), u32))
    ys = _expert_call(te1, te2, tvalid, xs, wg, wu, wd, te)
    return _gather_rows(pos, ys, n)


def _pool_mix_tail(hf, up, ext, tpos0, pw_ref, pb_ref, ps_ref):
    rows = up.shape[0]
    tpos = tpos0 + lax.broadcasted_iota(i32, (rows, 1), 0)
    parts = []
    for gi, win in enumerate(POOL_WINDOWS):
        sl = slice(gi * POOL_CH, (gi + 1) * POOL_CH)
        s = ext[:, sl]
        for kk in range(gi + 1):
            s = s + pltpu.roll(s, 1 << kk, 0)
        cnt = jnp.minimum(win, tpos + 1).astype(f32)
        z = s[16:, :] / cnt - up[:, sl]
        parts.append(_dot(z.astype(bf16), pw_ref[gi]))
    y = (jnp.concatenate(parts, axis=1) + pb_ref[...]) * ps_ref[...]
    return hf + y


def _pool_kernel(h_ref, y_ref, hh_ref, yh_ref, pn_ref, pw_ref, pb_ref, ps_ref,
                 gn_ref, wr_ref, br_ref, h2_ref, slab_ref, metat_ref, st_ref, *, tm, tps):
    i = pl.program_id(0)
    r = i % tps
    hf = h_ref[...] + _from_slab_f32(y_ref, tm)
    up = _rms(hf, pn_ref[...])
    hh = hh_ref[...] + _from_slab_f32(yh_ref, 16)
    uph = jnp.where(r > 0, _rms(hh, pn_ref[...]), 0.0)
    ext = jnp.concatenate([uph, up], axis=0)
    h2 = _pool_mix_tail(hf, up, ext, r * tm, pw_ref, pb_ref, ps_ref)
    h2_ref[...] = h2
    st_ref[...] = up[tm - 16:, :]
    _ffn_prep(h2, gn_ref, wr_ref, br_ref, slab_ref, metat_ref)


def _pool_call(h, ytt, pn, pw_bf, pb, psc, router, batch, seq, tm):
    n = h.shape[0]
    tps = seq // tm
    ospecs, oshapes = _prep_out(n, tm)
    ospecs.append(pl.BlockSpec((16, D_MODEL), lambda i: (i // tps, 0)))
    oshapes.append(jax.ShapeDtypeStruct((batch * 16, D_MODEL), f32))
    halo = lambda i: (jnp.maximum(i * (tm // 16) - 1, 0), 0)
    cm = lambda i: (0, 0)
    return pl.pallas_call(
        functools.partial(_pool_kernel, tm=tm, tps=tps),
        grid=(n // tm,),
        in_specs=[pl.BlockSpec((tm, D_MODEL), lambda i: (i, 0)),
                  pl.BlockSpec((tm * SUBLANES, LANES), lambda i: (i, 0)),
                  pl.BlockSpec((16, D_MODEL), halo),
                  pl.BlockSpec((16 * SUBLANES, LANES), halo),
                  pl.BlockSpec((1, D_MODEL), cm),
                  pl.BlockSpec((4, POOL_CH, POOL_CH), lambda i: (0, 0, 0)),
                  pl.BlockSpec((1, D_MODEL), cm),
                  pl.BlockSpec((1, D_MODEL), cm)] + _router_specs(),
        out_specs=ospecs,
        out_shape=oshapes,
        compiler_params=_cparams(("arbitrary",)),
        name="pool_mixer",
    )(h, ytt, h, ytt, pn, pw_bf, pb, psc, *router)


def _pool_sample_kernel(h_ref, y_ref, st_ref, pn_ref, pw_ref, pb_ref, ps_ref,
                        gn_ref, wr_ref, br_ref, h2_ref, slab_ref, metat_ref, nst_ref, *, nb):
    rows = h_ref.shape[0]
    hf = h_ref[...] + _from_slab_f32(y_ref, rows)
    up = _rms(hf, pn_ref[...])
    st = st_ref[...]
    ub = up[:nb, :]
    parts = []
    for gi, win in enumerate(POOL_WINDOWS):
        sl = slice(gi * POOL_CH, (gi + 1) * POOL_CH)
        tot = ub[:, sl] + jnp.sum(st[:, POOL_STATE - (win - 1):, sl], axis=1)
        z = tot / float(win) - ub[:, sl]
        parts.append(_dot(z.astype(bf16), pw_ref[gi]))
    y = (jnp.concatenate(parts, axis=1) + pb_ref[...]) * ps_ref[...]
    if rows > nb:
        y = jnp.concatenate([y, jnp.zeros((rows - nb, D_MODEL), f32)], axis=0)
    h2 = hf + y
    h2_ref[...] = h2
    nst_ref[:, :POOL_STATE - 1, :] = st[:, 1:, :]
    nst_ref[:, POOL_STATE - 1, :] = ub
    _ffn_prep(h2, gn_ref, wr_ref, br_ref, slab_ref, metat_ref)


def _pool_sample_call(h, ytt, state, pn, pw_bf, pb, psc, router, nb):
    rows = h.shape[0]
    ospecs, oshapes = _prep_out(rows, rows)
    ospecs.append(pl.BlockSpec((nb, POOL_STATE, D_MODEL), lambda i: (0, 0, 0)))
    oshapes.append(jax.ShapeDtypeStruct((nb, POOL_STATE, D_MODEL), f32))
    cm = lambda i: (0, 0)
    return pl.pallas_call(
        functools.partial(_pool_sample_kernel, nb=nb),
        grid=(1,),
        in_specs=[pl.BlockSpec((rows, D_MODEL), cm),
                  pl.BlockSpec((rows * SUBLANES, LANES), cm),
                  pl.BlockSpec((nb, POOL_STATE, D_MODEL), lambda i: (0, 0, 0)),
                  pl.BlockSpec((1, D_MODEL), cm),
                  pl.BlockSpec((4, POOL_CH, POOL_CH), lambda i: (0, 0, 0)),
                  pl.BlockSpec((1, D_MODEL), cm),
                  pl.BlockSpec((1, D_MODEL), cm)] + _router_specs(),
        out_specs=ospecs,
        out_shape=oshapes,
        compiler_params=_cparams(("arbitrary",)),
        name="pool_mixer_sample",
    )(h, ytt, state, pn, pw_bf, pb, psc, *router)


def _final_kernel(h_ref, y_ref, o_ref, *, tm):
    o_ref[...] = h_ref[...] + _from_slab_f32(y_ref, tm)


def _final_call(h, ytt, tm):
    n = h.shape[0]
    return pl.pallas_call(
        functools.partial(_final_kernel, tm=tm),
        grid=(n // tm,),
        in_specs=[pl.BlockSpec((tm, D_MODEL), lambda i: (i, 0)),
                  pl.BlockSpec((tm * SUBLANES, LANES), lambda i: (i, 0))],
        out_specs=pl.BlockSpec((tm, D_MODEL), lambda i: (i, 0)),
        out_shape=jax.ShapeDtypeStruct((n, D_MODEL), f32),
        compiler_params=_cparams(("arbitrary",)),
        name="final_residual",
    )(h, ytt)


def _t5_bucket(dist):
    max_exact = N_BUCKETS // 2
    df = jnp.maximum(dist, max_exact).astype(f32)
    large = max_exact + (jnp.log(df / max_exact) / math.log(MAX_DISTANCE / max_exact)
                         * (N_BUCKETS - max_exact)).astype(i32)
    large = jnp.minimum(large, N_BUCKETS - 1)
    return jnp.where(dist < max_exact, dist, large)


def _prompt_bias_tables(rel_bias):
    qi = jnp.arange(BAND)[:, None]
    kj = jnp.arange(2 * BAND)[None, :]
    step = BAND + qi - kj
    tabs = []
    for g, (win, dil) in enumerate(zip(WINDOWS, DILATIONS)):
        n_step = win // dil
        tab = rel_bias[:, g * N_HEADS:(g + 1) * N_HEADS].astype(f32)
        bias = jnp.transpose(tab[_t5_bucket(jnp.maximum(step, 0) * dil)], (2, 0, 1))
        band = (step >= 0) & (step <= n_step)
        reg = jnp.where(band[None], bias, NEG)
        first = jnp.where((band & (kj >= BAND))[None], bias, NEG)
        tabs.append(jnp.stack([reg, first]))
    return jnp.stack(tabs)


def _sample_bias_tables(rel_bias):
    bs, b0 = [], []
    for g, (win, dil) in enumerate(zip(WINDOWS, DILATIONS)):
        n_step = win // dil
        assert n_step == BAND
        tab = rel_bias[:, g * N_HEADS:(g + 1) * N_HEADS].astype(f32)
        steps = jnp.arange(n_step + 1)
        b = tab[_t5_bucket(steps * dil)]
        b0.append(b[0])
        bs.append(b[1:][::-1])
    return jnp.stack(bs)[..., None], jnp.stack(b0)[..., None]


def _router_weights(ffn_norm_i, wgr, bgr, wer, ber):
    w = jnp.zeros((D_MODEL, LANES), f32)
    w = w.at[:, :N_EGROUPS].set(wgr).at[:, N_EGROUPS:N_EGROUPS + N_EXPERTS].set(wer)
    b = jnp.zeros((1, LANES), f32)
    b = b.at[0, :N_EGROUPS].set(bgr).at[0, N_EGROUPS:N_EGROUPS + N_EXPERTS].set(ber)
    return ffn_norm_i.reshape(1, D_MODEL), w.astype(bf16), b


def _norm_mats():
    head_of_col = np.arange(HD) // HEAD_DIM
    s = (head_of_col[:, None] == np.arange(LANES)[None, :]).astype(np.float32)
    e = s.T
    return jnp.asarray(s, bf16), jnp.asarray(np.concatenate([e, e], axis=0), bf16)


def kernel(x_prompt, x_sample, cache_kv_w128, cache_kv_w512, cache_kv_w2048, state_pool, w_qkv, q_norm, k_norm, w_o, rel_bias, attn_norm, pool_norm, pool_w, pool_b, pool_scale, ffn_norm, router_group_w, router_group_b, router_expert_w, router_expert_b, w_gate, w_up, w_down):
    batch, seq, _ = x_prompt.shape
    nb = x_sample.shape[0]
    assert x_sample.shape[1] == 1
    n = batch * seq
    srows = LANES
    caches = (cache_kv_w128[0], cache_kv_w512[0], cache_kv_w2048[0])

    smat, e2mat = _norm_mats()
    wqkv_bf = w_qkv[0].astype(bf16)
    wo_bf = w_o[0].astype(bf16)
    gq = jnp.tile(q_norm[0], (1, N_HEADS)).reshape(N_DGROUPS, 1, HD)
    gk = jnp.tile(k_norm[0], (1, N_HEADS)).reshape(N_DGROUPS, 1, HD)
    an = attn_norm[0].reshape(1, D_MODEL)
    routers = [_router_weights(ffn_norm[i], router_group_w[i], router_group_b[i],
                               router_expert_w[i], router_expert_b[i]) for i in range(2)]
    wg_bf = [w_gate[i].astype(bf16) for i in range(2)]
    wu_bf = [w_up[i].astype(bf16) for i in range(2)]
    wd_bf = [w_down[i].astype(bf16) for i in range(2)]
    pw_bf = pool_w[0].astype(bf16)
    pn = pool_norm[0].reshape(1, D_MODEL)
    pb = pool_b[0].reshape(1, D_MODEL)
    psc = pool_scale[0].reshape(1, D_MODEL)

    x2 = x_prompt.reshape(n, D_MODEL)
    planes, kv2, kv1, kv0 = _qkv_prompt(x2, an, wqkv_bf, gq, gk, smat, e2mat, batch, seq)
    o_bf = _attn_prompt(planes, _prompt_bias_tables(rel_bias), batch, seq)
    h0, slab0, metat0 = _wo_call(o_bf, x2, wo_bf, routers[0], tm=512)
    ytt0 = _moe(slab0, metat0, wg_bf[0], wu_bf[0], wd_bf[0], te=256, tr=512)
    h1, slab1, metat1, pst = _pool_call(h0, ytt0, pn, pw_bf, pb, psc, routers[1], batch, seq, tm=512)
    ytt1 = _moe(slab1, metat1, wg_bf[1], wu_bf[1], wd_bf[1], te=256, tr=512)
    y_prompt = _final_call(h1, ytt1, tm=512).reshape(batch, seq, D_MODEL)

    kvshape = lambda keep: (1, batch, keep, 2, N_HEADS, HEAD_DIM)
    new_kv_p = (kv0.reshape(kvshape(WINDOWS[0])), kv1.reshape(kvshape(WINDOWS[1])),
                kv2.reshape(kvshape(seq)))
    new_pool_p = pst.reshape(batch, 16, D_MODEL)[:, 16 - POOL_STATE:][None]

    xs = jnp.zeros((srows, D_MODEL), f32).at[:nb].set(x_sample[:, 0, :])
    qs, ks, vs = _qkv_sample(xs, an, wqkv_bf, gq, gk, smat, e2mat)
    to4 = lambda a: a[:, :nb].reshape(N_DGROUPS, nb, N_HEADS, HEAD_DIM)
    ks4, vs4 = to4(ks), to4(vs)
    bias_s, bias_s0 = _sample_bias_tables(rel_bias)
    os_ = _attn_sample(to4(qs), ks4, vs4, caches, bias_s, bias_s0, nb)
    os_pad = jnp.zeros((srows, HD), bf16).at[:nb].set(os_.reshape(nb, HD).astype(bf16))
    hs0, sslab0, smetat0 = _wo_call(os_pad, xs, wo_bf, routers[0], tm=srows)
    ystt0 = _moe(sslab0, smetat0, wg_bf[0], wu_bf[0], wd_bf[0], te=64, tr=srows)
    hs1, sslab1, smetat1, new_pool_s = _pool_sample_call(
        hs0, ystt0, state_pool[0], pn, pw_bf, pb, psc, routers[1], nb)
    ystt1 = _moe(sslab1, smetat1, wg_bf[1], wu_bf[1], wd_bf[1], te=64, tr=srows)
    y_sample = _final_call(hs1, ystt1, tm=srows)[:nb].reshape(nb, 1, D_MODEL)

    new_kv_s = []
    for g in range(N_DGROUPS):
        row = jnp.stack([ks4[g], vs4[g]], axis=1)[:, None]
        new_kv_s.append(jnp.concatenate([caches[g][:, 1:], row], axis=1)[None])

    return (y_prompt, y_sample, new_kv_p[0], new_kv_p[1], new_kv_p[2], new_pool_p,
            new_kv_s[0], new_kv_s[1], new_kv_s[2], new_pool_s[None])
```

```python
import functools
import math

import numpy as np
import jax
import jax.numpy as jnp
from jax import lax
from jax.experimental import pallas as pl
from jax.experimental.pallas import tpu as pltpu

f32 = jnp.float32
bf16 = jnp.bfloat16
u32 = jnp.uint32
i32 = jnp.int32

D_MODEL = 1024
N_HEADS = 16
HEAD_DIM = 64
HD = N_HEADS * HEAD_DIM
N_DGROUPS = 3
WINDOWS = (128, 512, 2048)
DILATIONS = (1, 4, 16)
BAND = 128
N_BUCKETS = 32
MAX_DISTANCE = 2048
POOL_WINDOWS = (2, 4, 8, 16)
POOL_CH = D_MODEL // 4
POOL_STATE = max(POOL_WINDOWS) - 1
N_EGROUPS = 4
EXPERTS_PER_GROUP = 4
N_EXPERTS = 16
D_EXPERT = D_MODEL // 2
RMS_EPS = 1e-6
NEG = -1e30

LANES = 128
SUBLANES = 8
HI_MASK = 0xFFFF0000
N_CLASSES = N_EGROUPS * 6
PLANES_PER_KIND = HD // (2 * LANES)
VMEM_LIMIT = 56 * 1024 * 1024

_PAIRS = [(a, b) for a in range(4) for b in range(a + 1, 4)]
_CLASS_E1 = np.array([g * 4 + _PAIRS[p][0] for g in range(4) for p in range(6)], np.int32)
_CLASS_E2 = np.array([g * 4 + _PAIRS[p][1] for g in range(4) for p in range(6)], np.int32)


def _cparams(sem=None, vmem=VMEM_LIMIT):
    kw = dict(vmem_limit_bytes=vmem)
    if sem is not None:
        kw["dimension_semantics"] = sem
    return pltpu.CompilerParams(**kw)


def _dot(a, b):
    return jnp.dot(a, b, preferred_element_type=f32)


def _rms(x, g):
    return x * lax.rsqrt(jnp.mean(x * x, axis=-1, keepdims=True) + RMS_EPS) * g


def _pack_words(lo, hi):
    lb = lax.bitcast_convert_type(lo.astype(bf16).astype(f32), u32) >> 16
    hb = lax.bitcast_convert_type(hi.astype(bf16).astype(f32), u32) & jnp.uint32(HI_MASK)
    return lb | hb


def _unpack_words(w):
    lo = lax.bitcast_convert_type(w << 16, f32)
    hi = lax.bitcast_convert_type(w & jnp.uint32(HI_MASK), f32)
    return lo, hi


def _head_norm(z, gain, s_ref, e2_ref):
    zz = z * z
    zz_hi = zz.astype(bf16)
    zz_lo = (zz - zz_hi.astype(f32)).astype(bf16)
    ss = _dot(zz_hi, s_ref[...]) + _dot(zz_lo, s_ref[...])
    r = lax.rsqrt(ss * (1.0 / HEAD_DIM) + RMS_EPS)
    r_hi = r.astype(bf16)
    r_lo = (r - r_hi.astype(f32)).astype(bf16)
    rb = _dot(jnp.concatenate([r_hi, r_lo], axis=1), e2_ref[...])
    return z * rb * gain


def _qkv_kernel(x_ref, gn_ref, w_ref, gq_ref, gk_ref, s_ref, e2_ref,
                planes_ref, kv2_ref, kv1_ref, kv0_ref, *, tm):
    j = pl.program_id(0)
    u = _rms(x_ref[...], gn_ref[...]).astype(bf16)
    z = _dot(u, w_ref[...])
    q = _head_norm(z[:, :HD], gq_ref[0], s_ref, e2_ref) * (HEAD_DIM ** -0.5)
    k = _head_norm(z[:, HD:2 * HD], gk_ref[0], s_ref, e2_ref)
    v = z[:, 2 * HD:]
    for kind, arr in enumerate((q, k, v)):
        for p in range(PLANES_PER_KIND):
            c0 = 2 * LANES * p
            planes_ref[kind * PLANES_PER_KIND + p] = _pack_words(
                arr[:, c0:c0 + LANES], arr[:, c0 + LANES:c0 + 2 * LANES])

    @pl.when(j == 2)
    def _():
        kv2_ref[:, :HD] = k
        kv2_ref[:, HD:] = v

    @pl.when(j == 1)
    def _():
        kv1_ref[:, :HD] = k
        kv1_ref[:, HD:] = v

    @pl.when(j == 0)
    def _():
        kv0_ref[:, :HD] = k[tm - BAND:, :]
        kv0_ref[:, HD:] = v[tm - BAND:, :]


def _qkv_prompt(x2, gn, w_bf, gq, gk, smat, e2mat, batch, seq, tm=256):
    n = x2.shape[0]
    nt = n // tm
    tps = seq // tm
    t512 = WINDOWS[1] // tm
    assert seq % tm == 0 and WINDOWS[1] % tm == 0 and tm >= BAND

    def kv1_map(j, i):
        b = i // tps
        r = i % tps
        inside = jnp.maximum(r - (tps - t512), 0)
        return (jnp.where(j == 1, b * t512 + inside, jnp.where(j < 1, 0, batch * t512 - 1)), 0)

    def kv0_map(j, i):
        return (jnp.where(j == 0, i // tps, batch - 1), 0)

    nplanes = 3 * PLANES_PER_KIND
    return pl.pallas_call(
        functools.partial(_qkv_kernel, tm=tm),
        grid=(N_DGROUPS, nt),
        in_specs=[
            pl.BlockSpec((tm, D_MODEL), lambda j, i: (i, 0)),
            pl.BlockSpec((1, D_MODEL), lambda j, i: (0, 0)),
            pl.BlockSpec((D_MODEL, 3 * HD), lambda j, i: (0, j)),
            pl.BlockSpec((1, 1, HD), lambda j, i: (j, 0, 0)),
            pl.BlockSpec((1, 1, HD), lambda j, i: (j, 0, 0)),
            pl.BlockSpec((HD, LANES), lambda j, i: (0, 0)),
            pl.BlockSpec((2 * LANES, HD), lambda j, i: (0, 0)),
        ],
        out_specs=[
            pl.BlockSpec((nplanes, tm, LANES), lambda j, i: (j, i, 0)),
            pl.BlockSpec((tm, 2 * HD), lambda j, i: (jnp.where(j == 2, i, 0), 0)),
            pl.BlockSpec((tm, 2 * HD), kv1_map),
            pl.BlockSpec((BAND, 2 * HD), kv0_map),
        ],
        out_shape=[
            jax.ShapeDtypeStruct((N_DGROUPS * nplanes, n, LANES), u32),
            jax.ShapeDtypeStruct((n, 2 * HD), f32),
            jax.ShapeDtypeStruct((batch * WINDOWS[1], 2 * HD), f32),
            jax.ShapeDtypeStruct((batch * WINDOWS[0], 2 * HD), f32),
        ],
        compiler_params=_cparams(("arbitrary", "arbitrary")),
        name="qkv_prompt",
    )(x2, gn, w_bf, gq, gk, smat, e2mat)


def _qkv_sample_kernel(x_ref, gn_ref, w_ref, gq_ref, gk_ref, s_ref, e2_ref, q_ref, k_ref, v_ref):
    u = _rms(x_ref[...], gn_ref[...]).astype(bf16)
    z = _dot(u, w_ref[...])
    q_ref[0] = _head_norm(z[:, :HD], gq_ref[0], s_ref, e2_ref) * (HEAD_DIM ** -0.5)
    k_ref[0] = _head_norm(z[:, HD:2 * HD], gk_ref[0], s_ref, e2_ref)
    v_ref[0] = z[:, 2 * HD:]


def _qkv_sample(xs, gn, w_bf, gq, gk, smat, e2mat):
    rows = xs.shape[0]
    out = jax.ShapeDtypeStruct((N_DGROUPS, rows, HD), f32)
    ospec = pl.BlockSpec((1, rows, HD), lambda j: (j, 0, 0))
    return pl.pallas_call(
        _qkv_sample_kernel,
        grid=(N_DGROUPS,),
        in_specs=[
            pl.BlockSpec((rows, D_MODEL), lambda j: (0, 0)),
            pl.BlockSpec((1, D_MODEL), lambda j: (0, 0)),
            pl.BlockSpec((D_MODEL, 3 * HD), lambda j: (0, j)),
            pl.BlockSpec((1, 1, HD), lambda j: (j, 0, 0)),
            pl.BlockSpec((1, 1, HD), lambda j: (j, 0, 0)),
            pl.BlockSpec((HD, LANES), lambda j: (0, 0)),
            pl.BlockSpec((2 * LANES, HD), lambda j: (0, 0)),
        ],
        out_specs=[ospec, ospec, ospec],
        out_shape=[out, out, out],
        compiler_params=_cparams(("arbitrary",)),
        name="qkv_sample",
    )(xs, gn, w_bf, gq, gk, smat, e2mat)


def _attn_heads(qw, kcw, kpw, vcw, vpw, bias_fn, prev):
    q2 = _unpack_words(qw)
    kc2 = _unpack_words(kcw)
    vc2 = _unpack_words(vcw)
    if kpw is not None:
        kp2 = _unpack_words(kpw)
        vp2 = _unpack_words(vpw)
    lane = lax.broadcasted_iota(i32, (BAND, LANES), 1)
    left = lane < HEAD_DIM
    outs, lses = [], []
    for pr in range(2):
        q = q2[pr].astype(bf16)
        if kpw is not None:
            kk = jnp.concatenate([kp2[pr], kc2[pr]], axis=0).astype(bf16)
            vv = jnp.concatenate([vp2[pr], vc2[pr]], axis=0).astype(bf16)
        else:
            kk = kc2[pr].astype(bf16)
            vv = vc2[pr].astype(bf16)
        halves = []
        for hf in range(2):
            hh = 2 * pr + hf
            qm = jnp.where(left if hf == 0 else jnp.logical_not(left), q, jnp.zeros_like(q))
            s = lax.dot_general(qm, kk, (((1,), (1,)), ((), ())), preferred_element_type=f32)
            s = s + bias_fn(hh)
            m = jnp.max(s, axis=1, keepdims=True)
            p = jnp.exp(s - m)
            l = jnp.sum(p, axis=1, keepdims=True)
            o = _dot(p.astype(bf16), vv) / l
            lse = m + jnp.log(l)
            if prev is not None:
                lp = prev[2][:, hh:hh + 1]
                mx = jnp.maximum(lp, lse)
                ea = jnp.exp(lp - mx)
                eb = jnp.exp(lse - mx)
                tot = ea + eb
                o = prev[pr] * (ea / tot) + o * (eb / tot)
                lse = mx + jnp.log(tot)
            halves.append(o)
            lses.append(lse)
        outs.append(jnp.where(left, halves[0], halves[1]))
    zero = jnp.zeros((BAND, LANES), f32)
    lse_tile = jnp.where(lane == 0, lses[0], jnp.where(lane == 1, lses[1],
                         jnp.where(lane == 2, lses[2], jnp.where(lane == 3, lses[3], zero))))
    return outs[0], outs[1], lse_tile


def _attn_kernel(q0, k0, v0, q1, k1, v1, q2, k2, v2, bias_ref, o_ref, acc_lo, acc_hi, lse_ref, *, seq):
    d2 = DILATIONS[2]
    assert seq == d2 * BAND

    def body2(r, c):
        cur = pl.ds(r, BAND, stride=d2)
        o_lo, o_hi, lse = _attn_heads(
            q2[0, cur, :], k2[0, cur, :], None, v2[0, cur, :], None,
            lambda hh: bias_ref[2, 0, hh, :, pl.ds(BAND, BAND)], None)
        acc_lo[cur, :] = o_lo
        acc_hi[cur, :] = o_hi
        lse_ref[cur, :] = lse
        return c

    lax.fori_loop(0, d2, body2, 0)

    d1 = DILATIONS[1]
    nb1 = seq // (d1 * BAND)

    def body1(idx, c):
        r = idx // nb1
        jb = idx % nb1
        st = r + jb * (d1 * BAND)
        sp = r + jnp.maximum(jb - 1, 0) * (d1 * BAND)
        first = jnp.where(jb == 0, 1, 0)
        cur = pl.ds(st, BAND, stride=d1)
        prv = pl.ds(sp, BAND, stride=d1)
        o_lo, o_hi, lse = _attn_heads(
            q1[0, cur, :], k1[0, cur, :], k1[0, prv, :], v1[0, cur, :], v1[0, prv, :],
            lambda hh: bias_ref[1, first, hh],
            (acc_lo[cur, :], acc_hi[cur, :], lse_ref[cur, :]))
        acc_lo[cur, :] = o_lo
        acc_hi[cur, :] = o_hi
        lse_ref[cur, :] = lse
        return c

    lax.fori_loop(0, d1 * nb1, body1, 0)

    def body0(jb, c):
        st = pl.multiple_of(jb * BAND, BAND)
        sp = pl.multiple_of(jnp.maximum(jb - 1, 0) * BAND, BAND)
        first = jnp.where(jb == 0, 1, 0)
        cur = pl.ds(st, BAND)
        o_lo, o_hi, _ = _attn_heads(
            q0[0, cur, :], k0[0, cur, :], k0[0, pl.ds(sp, BAND), :],
            v0[0, cur, :], v0[0, pl.ds(sp, BAND), :],
            lambda hh: bias_ref[0, first, hh],
            (acc_lo[cur, :], acc_hi[cur, :], lse_ref[cur, :]))
        o_ref[cur, :LANES] = o_lo.astype(bf16)
        o_ref[cur, LANES:] = o_hi.astype(bf16)
        return c

    lax.fori_loop(0, seq // BAND, body0, 0)


def _attn_prompt(planes, bias_tab, batch, seq):
    n = batch * seq
    ppk = PLANES_PER_KIND

    def spec(g, kind):
        return pl.BlockSpec((1, seq, LANES), lambda b, p: ((g * 3 + kind) * ppk + p, b, 0))

    in_specs = [spec(g, kind) for g in range(N_DGROUPS) for kind in range(3)]
    in_specs.append(pl.BlockSpec((N_DGROUPS, 2, 4, BAND, 2 * BAND), lambda b, p: (0, 0, p, 0, 0)))
    return pl.pallas_call(
        functools.partial(_attn_kernel, seq=seq),
        grid=(batch, ppk),
        in_specs=in_specs,
        out_specs=pl.BlockSpec((seq, 2 * LANES), lambda b, p: (b, p)),
        out_shape=jax.ShapeDtypeStruct((n, HD), bf16),
        scratch_shapes=[pltpu.VMEM((seq, LANES), f32), pltpu.VMEM((seq, LANES), f32),
                        pltpu.VMEM((seq, LANES), f32)],
        compiler_params=_cparams(("arbitrary", "arbitrary")),
        name="attn_prompt",
    )(*([planes] * 9), bias_tab)


def _attn_sample_kernel(q_ref, kn_ref, vn_ref, c0_ref, c1_ref, c2_ref, bs_ref, b0_ref, o_ref):
    caches = (c0_ref, c1_ref, c2_ref)
    rb = lambda a: a.astype(bf16).astype(f32)
    ms, ls, accs = [], [], []
    for g in range(N_DGROUPS):
        q = rb(q_ref[g, 0])
        kn = rb(kn_ref[g, 0])
        vn = rb(vn_ref[g, 0])
        kc = rb(caches[g][0, 0, :, 0, 0])
        vc = rb(caches[g][0, 0, :, 0, 1])
        s = jnp.sum(kc * q[None], axis=-1, keepdims=True) + bs_ref[g]
        s0 = jnp.sum(kn * q, axis=-1, keepdims=True) + b0_ref[g]
        m = jnp.maximum(jnp.max(s, axis=0), s0)
        p = jnp.exp(s - m[None])
        p0 = jnp.exp(s0 - m)
        ls.append(jnp.sum(p, axis=0) + p0)
        accs.append(jnp.sum(rb(p) * vc, axis=0) + rb(p0) * vn)
        ms.append(m)
    mm = jnp.maximum(jnp.maximum(ms[0], ms[1]), ms[2])
    num = jnp.zeros_like(accs[0])
    den = jnp.zeros_like(ls[0])
    for g in range(N_DGROUPS):
        w = jnp.exp(ms[g] - mm)
        num = num + w * accs[g]
        den = den + w * ls[g]
    o_ref[0] = num / den


def _attn_sample(q4, kn4, vn4, caches, bias_s, bias_s0, nb):
    qspec = pl.BlockSpec((N_DGROUPS, 1, N_HEADS, HEAD_DIM), lambda b: (0, b, 0, 0))
    cspecs = []
    cviews = []
    for g in range(N_DGROUPS):
        d = DILATIONS[g]
        length = caches[g].shape[1]
        assert length % (d * BAND) == 0 and length >= d * BAND
        cviews.append(caches[g].reshape(nb, length // d, d, 2, N_HEADS, HEAD_DIM)[None])
        last = length // (d * BAND) - 1
        cspecs.append(pl.BlockSpec((1, 1, BAND, 1, 2, N_HEADS, HEAD_DIM),
                                   lambda b, last=last: (0, b, last, 0, 0, 0, 0)))
    return pl.pallas_call(
        _attn_sample_kernel,
        grid=(nb,),
        in_specs=[qspec, qspec, qspec] + cspecs + [
            pl.BlockSpec((N_DGROUPS, BAND, N_HEADS, 1), lambda b: (0, 0, 0, 0)),
            pl.BlockSpec((N_DGROUPS, N_HEADS, 1), lambda b: (0, 0, 0)),
        ],
        out_specs=pl.BlockSpec((1, N_HEADS, HEAD_DIM), lambda b: (b, 0, 0)),
        out_shape=jax.ShapeDtypeStruct((nb, N_HEADS, HEAD_DIM), f32),
        compiler_params=_cparams(("arbitrary",)),
        name="attn_sample",
    )(q4, kn4, vn4, *cviews, bias_s, bias_s0)


def _shift_kernel(c0, c1, c2, r0, r1, r2, o0, o1, o2, sem, *, nb):
    copies = []
    for c, r, o in ((c0, r0, o0), (c1, r1, o1), (c2, r2, o2)):
        length = c.shape[2]
        for b in range(nb):
            copies.append(pltpu.make_async_copy(
                c.at[0, b, pl.ds(1, length - 1)], o.at[0, b, pl.ds(0, length - 1)], sem))
            copies.append(pltpu.make_async_copy(r.at[b], o.at[0, b, pl.ds(length - 1, 1)], sem))
    for cp in copies:
        cp.start()
    for cp in copies:
        cp.wait()


def _shift_call(caches6, rows, nb):
    anyspec = pl.BlockSpec(memory_space=pl.ANY)
    return pl.pallas_call(
        functools.partial(_shift_kernel, nb=nb),
        in_specs=[anyspec] * 6,
        out_specs=[anyspec] * 3,
        out_shape=[jax.ShapeDtypeStruct(c.shape, c.dtype) for c in caches6],
        scratch_shapes=[pltpu.SemaphoreType.DMA(())],
        name="kv_window_shift",
    )(*caches6, *rows)


def _route(u, wr_ref, br_ref):
    lg = _dot(u.astype(bf16), wr_ref[...]) + br_ref[...]
    rows = u.shape[0]
    lane = lax.broadcasted_iota(i32, (rows, LANES), 1)
    lanef = lane.astype(f32)
    big = jnp.float32(1e9)
    isg = lane < N_EGROUPS
    lgg = jnp.where(isg, lg, NEG)
    mg = jnp.max(lgg, axis=1, keepdims=True)
    eg = jnp.where(isg, jnp.exp(lgg - mg), 0.0)
    pg = eg / jnp.sum(eg, axis=1, keepdims=True)
    gw = jnp.max(pg, axis=1, keepdims=True)
    gi = jnp.min(jnp.where(isg & (pg == gw), lanef, big), axis=1, keepdims=True)
    grp_of_lane = ((lane - N_EGROUPS) >> 2).astype(f32)
    sel = (lane >= N_EGROUPS) & (lane < N_EGROUPS + N_EXPERTS) & (grp_of_lane == gi)
    les = jnp.where(sel, lg, NEG)
    v1 = jnp.max(les, axis=1, keepdims=True)
    i1 = jnp.min(jnp.where(sel & (les == v1), lanef, big), axis=1, keepdims=True)
    sel2 = sel & (lanef != i1)
    les2 = jnp.where(sel2, lg, NEG)
    v2 = jnp.max(les2, axis=1, keepdims=True)
    i2 = jnp.min(jnp.where(sel2 & (les2 == v2), lanef, big), axis=1, keepdims=True)
    ex = jnp.exp(v2 - v1)
    den = 1.0 + ex
    pw1 = (1.0 / den) * gw
    pw2 = (ex / den) * gw
    e1 = i1 - N_EGROUPS
    e2 = i2 - N_EGROUPS
    first_low = e1 < e2
    lo = jnp.minimum(e1, e2) - 4.0 * gi
    hi = jnp.maximum(e1, e2) - 4.0 * gi
    pair = jnp.where(lo == 0.0, hi - 1.0, jnp.where(lo == 1.0, hi + 1.0, 5.0))
    cls = gi * 6.0 + pair
    g_lo = jnp.where(first_low, pw1, pw2)
    g_hi = jnp.where(first_low, pw2, pw1)
    zero = jnp.zeros((rows, LANES), f32)
    return jnp.where(lane == 0, cls, jnp.where(lane == 1, g_lo, jnp.where(lane == 2, g_hi, zero)))


def _write_slab(slab_ref, u, meta):
    rows = u.shape[0]
    half = D_MODEL // 2
    for c in range(4):
        w = _pack_words(u[:, c * LANES:(c + 1) * LANES], u[:, half + c * LANES:half + (c + 1) * LANES])
        slab_ref[pl.ds(c, rows, stride=SUBLANES), :] = w
    slab_ref[pl.ds(4, rows, stride=SUBLANES), :] = lax.bitcast_convert_type(meta, u32)
    z = jnp.zeros((rows, LANES), u32)
    for c in range(5, SUBLANES):
        slab_ref[pl.ds(c, rows, stride=SUBLANES), :] = z


def _ffn_prep(h, gn_ref, wr_ref, br_ref, slab_ref, metat_ref):
    u = _rms(h, gn_ref[...])
    meta = _route(u, wr_ref, br_ref)
    _write_slab(slab_ref, u, meta)
    metat_ref[...] = meta.T[:SUBLANES, :]


def _from_slab_f32(ref, rows):
    return jnp.concatenate([ref[pl.ds(c, rows, stride=SUBLANES), :] for c in range(SUBLANES)], axis=1)


def _wo_kernel(o_ref, x_ref, wo_ref, gn_ref, wr_ref, br_ref, h_ref, slab_ref, metat_ref):
    h = x_ref[...] + _dot(o_ref[...], wo_ref[...])
    h_ref[...] = h
    _ffn_prep(h, gn_ref, wr_ref, br_ref, slab_ref, metat_ref)


def _router_specs():
    cmap = lambda i: (0, 0)
    return [pl.BlockSpec((1, D_MODEL), cmap),
            pl.BlockSpec((D_MODEL, LANES), cmap),
            pl.BlockSpec((1, LANES), cmap)]


def _prep_out(n, tm):
    specs = [pl.BlockSpec((tm, D_MODEL), lambda i: (i, 0)),
             pl.BlockSpec((tm * SUBLANES, LANES), lambda i: (i, 0)),
             pl.BlockSpec((SUBLANES, tm), lambda i: (0, i))]
    shapes = [jax.ShapeDtypeStruct((n, D_MODEL), f32),
              jax.ShapeDtypeStruct((n * SUBLANES, LANES), u32),
              jax.ShapeDtypeStruct((SUBLANES, n), f32)]
    return specs, shapes


def _wo_call(o_bf, x2, wo_bf, router, tm):
    n = x2.shape[0]
    ospecs, oshapes = _prep_out(n, tm)
    return pl.pallas_call(
        _wo_kernel,
        grid=(n // tm,),
        in_specs=[pl.BlockSpec((tm, HD), lambda i: (i, 0)),
                  pl.BlockSpec((tm, D_MODEL), lambda i: (i, 0)),
                  pl.BlockSpec((HD, D_MODEL), lambda i: (0, 0))] + _router_specs(),
        out_specs=ospecs,
        out_shape=oshapes,
        compiler_params=_cparams(("arbitrary",)),
        name="wo_residual",
    )(o_bf, x2, wo_bf, *router)


def _rank_kernel(metat_ref, rank_ref, cnt_ref, carry_ref, *, tr):
    @pl.when(pl.program_id(0) == 0)
    def _():
        carry_ref[...] = jnp.zeros_like(carry_ref)

    cls = metat_ref[0:1, :]
    sub = lax.broadcasted_iota(i32, (32, tr), 0).astype(f32)
    oh = jnp.where(sub == cls, 1.0, 0.0)
    rr = lax.broadcasted_iota(i32, (tr, tr), 0)
    cc = lax.broadcasted_iota(i32, (tr, tr), 1)
    upper = jnp.where(rr < cc, 1.0, 0.0).astype(bf16)
    pre = _dot(oh.astype(bf16), upper)
    carry = carry_ref[:, 0:1]
    rank_ref[...] = jnp.sum(oh * (pre + carry), axis=0, keepdims=True).astype(i32)
    newc = carry + jnp.sum(oh, axis=1, keepdims=True)
    carry_ref[...] = jnp.broadcast_to(newc, carry_ref.shape)
    cnt_ref[...] = carry_ref[...]


def _rank_call(metat, tr):
    n = metat.shape[1]
    return pl.pallas_call(
        functools.partial(_rank_kernel, tr=tr),
        grid=(n // tr,),
        in_specs=[pl.BlockSpec((SUBLANES, tr), lambda i: (0, i))],
        out_specs=[pl.BlockSpec((1, tr), lambda i: (0, i)),
                   pl.BlockSpec((32, LANES), lambda i: (0, 0))],
        out_shape=[jax.ShapeDtypeStruct((1, n), i32), jax.ShapeDtypeStruct((32, LANES), f32)],
        scratch_shapes=[pltpu.VMEM((32, LANES), f32)],
        compiler_params=_cparams(("arbitrary",)),
        name="class_rank",
    )(metat)


ROWS_PER_GATHER_STEP = 256


def _gather_kernel(idx_ref, src_ref, o_ref, sem, *, rows):
    base = pl.program_id(0) * rows

    def copy(t, k):
        return pltpu.make_async_copy(
            src_ref.at[pl.ds(pl.multiple_of(t * SUBLANES, SUBLANES), SUBLANES)],
            o_ref.at[pl.ds(pl.multiple_of(k * SUBLANES, SUBLANES), SUBLANES)], sem)

    def issue(k, c):
        t = idx_ref[base + k]

        @pl.when(t >= 0)
        def _():
            copy(t, k).start()

        @pl.when(t < 0)
        def _():
            o_ref[pl.ds(pl.multiple_of(k * SUBLANES, SUBLANES), SUBLANES), :] = jnp.zeros(
                (SUBLANES, LANES), o_ref.dtype)
        return c

    def drain(k, c):
        t = idx_ref[base + k]

        @pl.when(t >= 0)
        def _():
            copy(t, k).wait()
        return c

    lax.fori_loop(0, rows, issue, 0)
    lax.fori_loop(0, rows, drain, 0)


def _gather_slabs(idx, src):
    n_out = idx.shape[0]
    rows = math.gcd(ROWS_PER_GATHER_STEP, n_out)
    assert rows % SUBLANES == 0
    return pl.pallas_call(
        functools.partial(_gather_kernel, rows=rows),
        grid_spec=pltpu.PrefetchScalarGridSpec(
            num_scalar_prefetch=1, grid=(n_out // rows,),
            in_specs=[pl.BlockSpec(memory_space=pl.ANY)],
            out_specs=pl.BlockSpec((rows * SUBLANES, LANES), lambda i, idx: (i, 0)),
            scratch_shapes=[pltpu.SemaphoreType.DMA(())]),
        out_shape=jax.ShapeDtypeStruct((n_out * SUBLANES, LANES), src.dtype),
        compiler_params=_cparams(("arbitrary",)),
        name="gather_slabs",
    )(idx, src)


def _expert_kernel(e1_ref, e2_ref, valid_ref, xs_ref, wg1, wu1, wd1, wg2, wu2, wd2, y_ref, *, te):
    del e1_ref, e2_ref
    valid = valid_ref[pl.program_id(0)]

    @pl.when(valid == 0)
    def _():
        y_ref[...] = jnp.zeros_like(y_ref)

    @pl.when(valid > 0)
    def _():
        los, his = [], []
        for c in range(4):
            lo, hi = _unpack_words(xs_ref[pl.ds(c, te, stride=SUBLANES), :])
            los.append(lo)
            his.append(hi)
        x = jnp.concatenate(los + his, axis=1).astype(bf16)
        meta = lax.bitcast_convert_type(xs_ref[pl.ds(4, te, stride=SUBLANES), :], f32)
        y = None
        for wg, wu, wd, lane in ((wg1, wu1, wd1, 1), (wg2, wu2, wd2, 2)):
            a = _dot(x, wg[0])
            c = _dot(x, wu[0])
            hh = (a * (1.0 / (1.0 + jnp.exp(-a)))) * c * meta[:, lane:lane + 1]
            part = _dot(hh.astype(bf16), wd[0])
            y = part if y is None else y + part
        for c in range(SUBLANES):
            y_ref[pl.ds(c, te, stride=SUBLANES), :] = y[:, c * LANES:(c + 1) * LANES]


def _expert_call(te1, te2, tvalid, xs, wg, wu, wd, te):
    ntiles = te1.shape[0]

    def wspec(shape, which):
        if which == 1:
            return pl.BlockSpec(shape, lambda i, e1, e2, v: (e1[i], 0, 0))
        return pl.BlockSpec(shape, lambda i, e1, e2, v: (e2[i], 0, 0))

    gs = (1, D_MODEL, D_EXPERT)
    ds_ = (1, D_EXPERT, D_MODEL)
    return pl.pallas_call(
        functools.partial(_expert_kernel, te=te),
        grid_spec=pltpu.PrefetchScalarGridSpec(
            num_scalar_prefetch=3, grid=(ntiles,),
            in_specs=[pl.BlockSpec((te * SUBLANES, LANES), lambda i, e1, e2, v: (i, 0)),
                      wspec(gs, 1), wspec(gs, 1), wspec(ds_, 1),
                      wspec(gs, 2), wspec(gs, 2), wspec(ds_, 2)],
            out_specs=pl.BlockSpec((te * SUBLANES, LANES), lambda i, e1, e2, v: (i, 0))),
        out_shape=jax.ShapeDtypeStruct((ntiles * te * SUBLANES, LANES), f32),
        compiler_params=_cparams(("arbitrary",)),
        name="expert_pair_ffn",
    )(te1, te2, tvalid, xs, wg, wu, wd, wg, wu, wd)


def _moe(slab, metat, wg, wu, wd, te, tr):
    n = metat.shape[1]
    rank, cnt = _rank_call(metat, tr)
    counts = cnt[:N_CLASSES, 0].astype(i32)
    padded = ((counts + te - 1) // te) * te
    ends = jnp.cumsum(padded)
    off = ends - padded
    cls = metat[0].astype(i32)
    class_ids = jnp.arange(N_CLASSES, dtype=i32)
    pos = rank[0] + jnp.sum(jnp.where(cls[:, None] == class_ids[None, :], off[None, :], 0), axis=1)
    ntiles = -(-n // te) + N_CLASSES
    tstart = jnp.arange(ntiles, dtype=i32) * te
    tcls = jnp.minimum(jnp.sum((tstart[:, None] >= ends[None, :]).astype(i32), axis=1), N_CLASSES - 1)
    tvalid = (tstart < ends[-1]).astype(i32)
    te1 = jnp.asarray(_CLASS_E1)[tcls]
    te2 = jnp.asarray(_CLASS_E2)[tcls]
    src = jnp.full((ntiles * te,), -1, i32).at[pos].set(jnp.arange(n, dtype=i32), unique_indices=True)
    xs = _gather_slabs(src, slab)
    ys = _expert_call(te1, te2, tvalid, xs, wg, wu, wd, te)
    return _gather_slabs(pos, ys)


def _pool_mix_tail(hf, up, ext, tpos0, pw_ref, pb_ref, ps_ref):
    rows = up.shape[0]
    tpos = tpos0 + lax.broadcasted_iota(i32, (rows, 1), 0)
    parts = []
    for gi, win in enumerate(POOL_WINDOWS):
        sl = slice(gi * POOL_CH, (gi + 1) * POOL_CH)
        s = ext[:, sl]
        for kk in range(gi + 1):
            s = s + pltpu.roll(s, 1 << kk, 0)
        cnt = jnp.minimum(win, tpos + 1).astype(f32)
        z = s[16:, :] / cnt - up[:, sl]
        parts.append(_dot(z.astype(bf16), pw_ref[gi]))
    y = (jnp.concatenate(parts, axis=1) + pb_ref[...]) * ps_ref[...]
    return hf + y


def _pool_kernel(h_ref, y_ref, hh_ref, yh_ref, pn_ref, pw_ref, pb_ref, ps_ref,
                 gn_ref, wr_ref, br_ref, h2_ref, slab_ref, metat_ref, st_ref, *, tm, tps):
    i = pl.program_id(0)
    r = i % tps
    hf = h_ref[...] + _from_slab_f32(y_ref, tm)
    up = _rms(hf, pn_ref[...])
    hh = hh_ref[...] + _from_slab_f32(yh_ref, 16)
    uph = jnp.where(r > 0, _rms(hh, pn_ref[...]), 0.0)
    ext = jnp.concatenate([uph, up], axis=0)
    h2 = _pool_mix_tail(hf, up, ext, r * tm, pw_ref, pb_ref, ps_ref)
    h2_ref[...] = h2
    st_ref[...] = up[tm - 16:, :]
    _ffn_prep(h2, gn_ref, wr_ref, br_ref, slab_ref, metat_ref)


def _pool_call(h, ytt, pn, pw_bf, pb, psc, router, batch, seq, tm):
    n = h.shape[0]
    tps = seq // tm
    ospecs, oshapes = _prep_out(n, tm)
    ospecs.append(pl.BlockSpec((16, D_MODEL), lambda i: (i // tps, 0)))
    oshapes.append(jax.ShapeDtypeStruct((batch * 16, D_MODEL), f32))
    halo = lambda i: (jnp.maximum(i * (tm // 16) - 1, 0), 0)
    cm = lambda i: (0, 0)
    return pl.pallas_call(
        functools.partial(_pool_kernel, tm=tm, tps=tps),
        grid=(n // tm,),
        in_specs=[pl.BlockSpec((tm, D_MODEL), lambda i: (i, 0)),
                  pl.BlockSpec((tm * SUBLANES, LANES), lambda i: (i, 0)),
                  pl.BlockSpec((16, D_MODEL), halo),
                  pl.BlockSpec((16 * SUBLANES, LANES), halo),
                  pl.BlockSpec((1, D_MODEL), cm),
                  pl.BlockSpec((4, POOL_CH, POOL_CH), lambda i: (0, 0, 0)),
                  pl.BlockSpec((1, D_MODEL), cm),
                  pl.BlockSpec((1, D_MODEL), cm)] + _router_specs(),
        out_specs=ospecs,
        out_shape=oshapes,
        compiler_params=_cparams(("arbitrary",)),
        name="pool_mixer",
    )(h, ytt, h, ytt, pn, pw_bf, pb, psc, *router)


def _pool_sample_kernel(h_ref, y_ref, st_ref, pn_ref, pw_ref, pb_ref, ps_ref,
                        gn_ref, wr_ref, br_ref, h2_ref, slab_ref, metat_ref, nst_ref, *, nb):
    rows = h_ref.shape[0]
    hf = h_ref[...] + _from_slab_f32(y_ref, rows)
    up = _rms(hf, pn_ref[...])
    st = st_ref[...]
    ub = up[:nb, :]
    parts = []
    for gi, win in enumerate(POOL_WINDOWS):
        sl = slice(gi * POOL_CH, (gi + 1) * POOL_CH)
        tot = ub[:, sl] + jnp.sum(st[:, POOL_STATE - (win - 1):, sl], axis=1)
        z = tot / float(win) - ub[:, sl]
        parts.append(_dot(z.astype(bf16), pw_ref[gi]))
    y = (jnp.concatenate(parts, axis=1) + pb_ref[...]) * ps_ref[...]
    if rows > nb:
        y = jnp.concatenate([y, jnp.zeros((rows - nb, D_MODEL), f32)], axis=0)
    h2 = hf + y
    h2_ref[...] = h2
    nst_ref[:, :POOL_STATE - 1, :] = st[:, 1:, :]
    nst_ref[:, POOL_STATE - 1, :] = ub
    _ffn_prep(h2, gn_ref, wr_ref, br_ref, slab_ref, metat_ref)


def _pool_sample_call(h, ytt, state, pn, pw_bf, pb, psc, router, nb):
    rows = h.shape[0]
    ospecs, oshapes = _prep_out(rows, rows)
    ospecs.append(pl.BlockSpec((nb, POOL_STATE, D_MODEL), lambda i: (0, 0, 0)))
    oshapes.append(jax.ShapeDtypeStruct((nb, POOL_STATE, D_MODEL), f32))
    cm = lambda i: (0, 0)
    return pl.pallas_call(
        functools.partial(_pool_sample_kernel, nb=nb),
        grid=(1,),
        in_specs=[pl.BlockSpec((rows, D_MODEL), cm),
                  pl.BlockSpec((rows * SUBLANES, LANES), cm),
                  pl.BlockSpec((nb, POOL_STATE, D_MODEL), lambda i: (0, 0, 0)),
                  pl.BlockSpec((1, D_MODEL), cm),
                  pl.BlockSpec((4, POOL_CH, POOL_CH), lambda i: (0, 0, 0)),
                  pl.BlockSpec((1, D_MODEL), cm),
                  pl.BlockSpec((1, D_MODEL), cm)] + _router_specs(),
        out_specs=ospecs,
        out_shape=oshapes,
        compiler_params=_cparams(("arbitrary",)),
        name="pool_mixer_sample",
    )(h, ytt, state, pn, pw_bf, pb, psc, *router)


def _final_kernel(h_ref, y_ref, o_ref, *, tm):
    o_ref[...] = h_ref[...] + _from_slab_f32(y_ref, tm)


def _final_call(h, ytt, tm):
    n = h.shape[0]
    return pl.pallas_call(
        functools.partial(_final_kernel, tm=tm),
        grid=(n // tm,),
        in_specs=[pl.BlockSpec((tm, D_MODEL), lambda i: (i, 0)),
                  pl.BlockSpec((tm * SUBLANES, LANES), lambda i: (i, 0))],
        out_specs=pl.BlockSpec((tm, D_MODEL), lambda i: (i, 0)),
        out_shape=jax.ShapeDtypeStruct((n, D_MODEL), f32),
        compiler_params=_cparams(("arbitrary",)),
        name="final_residual",
    )(h, ytt)


def _t5_bucket(dist):
    max_exact = N_BUCKETS // 2
    df = jnp.maximum(dist, max_exact).astype(f32)
    large = max_exact + (jnp.log(df / max_exact) / math.log(MAX_DISTANCE / max_exact)
                         * (N_BUCKETS - max_exact)).astype(i32)
    large = jnp.minimum(large, N_BUCKETS - 1)
    return jnp.where(dist < max_exact, dist, large)


def _prompt_bias_tables(rel_bias):
    qi = jnp.arange(BAND)[:, None]
    kj = jnp.arange(2 * BAND)[None, :]
    step = BAND + qi - kj
    tabs = []
    for g, (win, dil) in enumerate(zip(WINDOWS, DILATIONS)):
        n_step = win // dil
        tab = rel_bias[:, g * N_HEADS:(g + 1) * N_HEADS].astype(f32)
        bias = jnp.transpose(tab[_t5_bucket(jnp.maximum(step, 0) * dil)], (2, 0, 1))
        band = (step >= 0) & (step <= n_step)
        reg = jnp.where(band[None], bias, NEG)
        first = jnp.where((band & (kj >= BAND))[None], bias, NEG)
        tabs.append(jnp.stack([reg, first]))
    return jnp.stack(tabs)


def _sample_bias_tables(rel_bias):
    bs, b0 = [], []
    for g, (win, dil) in enumerate(zip(WINDOWS, DILATIONS)):
        n_step = win // dil
        assert n_step == BAND
        tab = rel_bias[:, g * N_HEADS:(g + 1) * N_HEADS].astype(f32)
        steps = jnp.arange(n_step + 1)
        b = tab[_t5_bucket(steps * dil)]
        b0.append(b[0])
        bs.append(b[1:][::-1])
    return jnp.stack(bs)[..., None], jnp.stack(b0)[..., None]


def _router_weights(ffn_norm_i, wgr, bgr, wer, ber):
    w = jnp.zeros((D_MODEL, LANES), f32)
    w = w.at[:, :N_EGROUPS].set(wgr).at[:, N_EGROUPS:N_EGROUPS + N_EXPERTS].set(wer)
    b = jnp.zeros((1, LANES), f32)
    b = b.at[0, :N_EGROUPS].set(bgr).at[0, N_EGROUPS:N_EGROUPS + N_EXPERTS].set(ber)
    return ffn_norm_i.reshape(1, D_MODEL), w.astype(bf16), b


def _norm_mats():
    head_of_col = np.arange(HD) // HEAD_DIM
    s = (head_of_col[:, None] == np.arange(LANES)[None, :]).astype(np.float32)
    e = s.T
    return jnp.asarray(s, bf16), jnp.asarray(np.concatenate([e, e], axis=0), bf16)


def kernel(x_prompt, x_sample, cache_kv_w128, cache_kv_w512, cache_kv_w2048, state_pool, w_qkv, q_norm, k_norm, w_o, rel_bias, attn_norm, pool_norm, pool_w, pool_b, pool_scale, ffn_norm, router_group_w, router_group_b, router_expert_w, router_expert_b, w_gate, w_up, w_down):
    batch, seq, _ = x_prompt.shape
    nb = x_sample.shape[0]
    assert x_sample.shape[1] == 1
    n = batch * seq
    srows = LANES
    caches = (cache_kv_w128[0], cache_kv_w512[0], cache_kv_w2048[0])

    smat, e2mat = _norm_mats()
    wqkv_bf = w_qkv[0].astype(bf16)
    wo_bf = w_o[0].astype(bf16)
    gq = jnp.tile(q_norm[0], (1, N_HEADS)).reshape(N_DGROUPS, 1, HD)
    gk = jnp.tile(k_norm[0], (1, N_HEADS)).reshape(N_DGROUPS, 1, HD)
    an = attn_norm[0].reshape(1, D_MODEL)
    routers = [_router_weights(ffn_norm[i], router_group_w[i], router_group_b[i],
                               router_expert_w[i], router_expert_b[i]) for i in range(2)]
    wg_bf = [w_gate[i].astype(bf16) for i in range(2)]
    wu_bf = [w_up[i].astype(bf16) for i in range(2)]
    wd_bf = [w_down[i].astype(bf16) for i in range(2)]
    pw_bf = pool_w[0].astype(bf16)
    pn = pool_norm[0].reshape(1, D_MODEL)
    pb = pool_b[0].reshape(1, D_MODEL)
    psc = pool_scale[0].reshape(1, D_MODEL)

    x2 = x_prompt.reshape(n, D_MODEL)
    planes, kv2, kv1, kv0 = _qkv_prompt(x2, an, wqkv_bf, gq, gk, smat, e2mat, batch, seq)
    o_bf = _attn_prompt(planes, _prompt_bias_tables(rel_bias), batch, seq)
    h0, slab0, metat0 = _wo_call(o_bf, x2, wo_bf, routers[0], tm=512)
    ytt0 = _moe(slab0, metat0, wg_bf[0], wu_bf[0], wd_bf[0], te=256, tr=512)
    h1, slab1, metat1, pst = _pool_call(h0, ytt0, pn, pw_bf, pb, psc, routers[1], batch, seq, tm=512)
    ytt1 = _moe(slab1, metat1, wg_bf[1], wu_bf[1], wd_bf[1], te=256, tr=512)
    y_prompt = _final_call(h1, ytt1, tm=512).reshape(batch, seq, D_MODEL)

    kvshape = lambda keep: (1, batch, keep, 2, N_HEADS, HEAD_DIM)
    new_kv_p = (kv0.reshape(kvshape(WINDOWS[0])), kv1.reshape(kvshape(WINDOWS[1])),
                kv2.reshape(kvshape(seq)))
    new_pool_p = pst.reshape(batch, 16, D_MODEL)[:, 16 - POOL_STATE:][None]

    xs = jnp.zeros((srows, D_MODEL), f32).at[:nb].set(x_sample[:, 0, :])
    qs, ks, vs = _qkv_sample(xs, an, wqkv_bf, gq, gk, smat, e2mat)
    to4 = lambda a: a[:, :nb].reshape(N_DGROUPS, nb, N_HEADS, HEAD_DIM)
    ks4, vs4 = to4(ks), to4(vs)
    bias_s, bias_s0 = _sample_bias_tables(rel_bias)
    os_ = _attn_sample(to4(qs), ks4, vs4, caches, bias_s, bias_s0, nb)
    os_pad = jnp.zeros((srows, HD), bf16).at[:nb].set(os_.reshape(nb, HD).astype(bf16))
    hs0, sslab0, smetat0 = _wo_call(os_pad, xs, wo_bf, routers[0], tm=srows)
    ystt0 = _moe(sslab0, smetat0, wg_bf[0], wu_bf[0], wd_bf[0], te=64, tr=srows)
    hs1, sslab1, smetat1, new_pool_s = _pool_sample_call(
        hs0, ystt0, state_pool[0], pn, pw_bf, pb, psc, routers[1], nb)
    ystt1 = _moe(sslab1, smetat1, wg_bf[1], wu_bf[1], wd_bf[1], te=64, tr=srows)
    y_sample = _final_call(hs1, ystt1, tm=srows)[:nb].reshape(nb, 1, D_MODEL)

    rows = [jnp.stack([ks4[g], vs4[g]], axis=1)[:, None] for g in range(N_DGROUPS)]
    new_kv_s = _shift_call((cache_kv_w128, cache_kv_w512, cache_kv_w2048), rows, nb)

    return (y_prompt, y_sample, new_kv_p[0], new_kv_p[1], new_kv_p[2], new_pool_p,
            new_kv_s[0], new_kv_s[1], new_kv_s[2], new_pool_s[None])
```

```python
import functools
import math

import numpy as np
import jax
import jax.numpy as jnp
from jax import lax
from jax.experimental import pallas as pl
from jax.experimental.pallas import tpu as pltpu

f32 = jnp.float32
bf16 = jnp.bfloat16
u32 = jnp.uint32
i32 = jnp.int32

D_MODEL = 1024
N_HEADS = 16
HEAD_DIM = 64
HD = N_HEADS * HEAD_DIM
N_DGROUPS = 3
WINDOWS = (128, 512, 2048)
DILATIONS = (1, 4, 16)
BAND = 128
N_BUCKETS = 32
MAX_DISTANCE = 2048
POOL_WINDOWS = (2, 4, 8, 16)
POOL_CH = D_MODEL // 4
POOL_STATE = max(POOL_WINDOWS) - 1
N_EGROUPS = 4
EXPERTS_PER_GROUP = 4
N_EXPERTS = 16
D_EXPERT = D_MODEL // 2
RMS_EPS = 1e-6
NEG = -1e30

LANES = 128
SUBLANES = 8
HI_MASK = 0xFFFF0000
N_CLASSES = N_EGROUPS * 6
PLANES_PER_KIND = HD // (2 * LANES)
VMEM_LIMIT = 56 * 1024 * 1024

_PAIRS = [(a, b) for a in range(4) for b in range(a + 1, 4)]
_CLASS_E1 = np.array([g * 4 + _PAIRS[p][0] for g in range(4) for p in range(6)], np.int32)
_CLASS_E2 = np.array([g * 4 + _PAIRS[p][1] for g in range(4) for p in range(6)], np.int32)


def _cparams(sem=None, vmem=VMEM_LIMIT):
    kw = dict(vmem_limit_bytes=vmem)
    if sem is not None:
        kw["dimension_semantics"] = sem
    return pltpu.CompilerParams(**kw)


def _dot(a, b):
    return jnp.dot(a, b, preferred_element_type=f32)


def _rms(x, g):
    return x * lax.rsqrt(jnp.mean(x * x, axis=-1, keepdims=True) + RMS_EPS) * g


def _pack_words(lo, hi):
    lb = lax.bitcast_convert_type(lo.astype(bf16).astype(f32), u32) >> 16
    hb = lax.bitcast_convert_type(hi.astype(bf16).astype(f32), u32) & jnp.uint32(HI_MASK)
    return lb | hb


def _unpack_words(w):
    lo = lax.bitcast_convert_type(w << 16, f32)
    hi = lax.bitcast_convert_type(w & jnp.uint32(HI_MASK), f32)
    return lo, hi


def _head_norm(z, gain, s_ref, e2_ref):
    zz = z * z
    zz_hi = zz.astype(bf16)
    zz_lo = (zz - zz_hi.astype(f32)).astype(bf16)
    ss = _dot(zz_hi, s_ref[...]) + _dot(zz_lo, s_ref[...])
    r = lax.rsqrt(ss * (1.0 / HEAD_DIM) + RMS_EPS)
    r_hi = r.astype(bf16)
    r_lo = (r - r_hi.astype(f32)).astype(bf16)
    rb = _dot(jnp.concatenate([r_hi, r_lo], axis=1), e2_ref[...])
    return z * rb * gain


def _qkv_kernel(x_ref, gn_ref, w_ref, gq_ref, gk_ref, s_ref, e2_ref,
                planes_ref, kv2_ref, kv1_ref, kv0_ref, *, tm):
    j = pl.program_id(0)
    u = _rms(x_ref[...], gn_ref[...]).astype(bf16)
    z = _dot(u, w_ref[...])
    q = _head_norm(z[:, :HD], gq_ref[0], s_ref, e2_ref) * (HEAD_DIM ** -0.5)
    k = _head_norm(z[:, HD:2 * HD], gk_ref[0], s_ref, e2_ref)
    v = z[:, 2 * HD:]
    for kind, arr in enumerate((q, k, v)):
        for p in range(PLANES_PER_KIND):
            c0 = 2 * LANES * p
            planes_ref[kind * PLANES_PER_KIND + p] = _pack_words(
                arr[:, c0:c0 + LANES], arr[:, c0 + LANES:c0 + 2 * LANES])

    @pl.when(j == 2)
    def _():
        kv2_ref[:, :HD] = k
        kv2_ref[:, HD:] = v

    @pl.when(j == 1)
    def _():
        kv1_ref[:, :HD] = k
        kv1_ref[:, HD:] = v

    @pl.when(j == 0)
    def _():
        kv0_ref[:, :HD] = k[tm - BAND:, :]
        kv0_ref[:, HD:] = v[tm - BAND:, :]


def _qkv_prompt(x2, gn, w_bf, gq, gk, smat, e2mat, batch, seq, tm=256):
    n = x2.shape[0]
    nt = n // tm
    tps = seq // tm
    t512 = WINDOWS[1] // tm
    assert seq % tm == 0 and WINDOWS[1] % tm == 0 and tm >= BAND

    def kv1_map(j, i):
        b = i // tps
        r = i % tps
        inside = jnp.maximum(r - (tps - t512), 0)
        return (jnp.where(j == 1, b * t512 + inside, jnp.where(j < 1, 0, batch * t512 - 1)), 0)

    def kv0_map(j, i):
        return (jnp.where(j == 0, i // tps, batch - 1), 0)

    nplanes = 3 * PLANES_PER_KIND
    return pl.pallas_call(
        functools.partial(_qkv_kernel, tm=tm),
        grid=(N_DGROUPS, nt),
        in_specs=[
            pl.BlockSpec((tm, D_MODEL), lambda j, i: (i, 0)),
            pl.BlockSpec((1, D_MODEL), lambda j, i: (0, 0)),
            pl.BlockSpec((D_MODEL, 3 * HD), lambda j, i: (0, j)),
            pl.BlockSpec((1, 1, HD), lambda j, i: (j, 0, 0)),
            pl.BlockSpec((1, 1, HD), lambda j, i: (j, 0, 0)),
            pl.BlockSpec((HD, LANES), lambda j, i: (0, 0)),
            pl.BlockSpec((2 * LANES, HD), lambda j, i: (0, 0)),
        ],
        out_specs=[
            pl.BlockSpec((nplanes, tm, LANES), lambda j, i: (j, i, 0)),
            pl.BlockSpec((tm, 2 * HD), lambda j, i: (jnp.where(j == 2, i, 0), 0)),
            pl.BlockSpec((tm, 2 * HD), kv1_map),
            pl.BlockSpec((BAND, 2 * HD), kv0_map),
        ],
        out_shape=[
            jax.ShapeDtypeStruct((N_DGROUPS * nplanes, n, LANES), u32),
            jax.ShapeDtypeStruct((n, 2 * HD), f32),
            jax.ShapeDtypeStruct((batch * WINDOWS[1], 2 * HD), f32),
            jax.ShapeDtypeStruct((batch * WINDOWS[0], 2 * HD), f32),
        ],
        compiler_params=_cparams(("arbitrary", "arbitrary")),
        name="qkv_prompt",
    )(x2, gn, w_bf, gq, gk, smat, e2mat)


def _qkv_sample_kernel(x_ref, gn_ref, w_ref, gq_ref, gk_ref, s_ref, e2_ref, q_ref, k_ref, v_ref):
    u = _rms(x_ref[...], gn_ref[...]).astype(bf16)
    z = _dot(u, w_ref[...])
    q_ref[0] = _head_norm(z[:, :HD], gq_ref[0], s_ref, e2_ref) * (HEAD_DIM ** -0.5)
    k_ref[0] = _head_norm(z[:, HD:2 * HD], gk_ref[0], s_ref, e2_ref)
    v_ref[0] = z[:, 2 * HD:]


def _qkv_sample(xs, gn, w_bf, gq, gk, smat, e2mat):
    rows = xs.shape[0]
    out = jax.ShapeDtypeStruct((N_DGROUPS, rows, HD), f32)
    ospec = pl.BlockSpec((1, rows, HD), lambda j: (j, 0, 0))
    return pl.pallas_call(
        _qkv_sample_kernel,
        grid=(N_DGROUPS,),
        in_specs=[
            pl.BlockSpec((rows, D_MODEL), lambda j: (0, 0)),
            pl.BlockSpec((1, D_MODEL), lambda j: (0, 0)),
            pl.BlockSpec((D_MODEL, 3 * HD), lambda j: (0, j)),
            pl.BlockSpec((1, 1, HD), lambda j: (j, 0, 0)),
            pl.BlockSpec((1, 1, HD), lambda j: (j, 0, 0)),
            pl.BlockSpec((HD, LANES), lambda j: (0, 0)),
            pl.BlockSpec((2 * LANES, HD), lambda j: (0, 0)),
        ],
        out_specs=[ospec, ospec, ospec],
        out_shape=[out, out, out],
        compiler_params=_cparams(("arbitrary",)),
        name="qkv_sample",
    )(xs, gn, w_bf, gq, gk, smat, e2mat)


ATTN_BLOCKS_PER_STEP = 8


def _attn_plane(qw, kcw, kpw, vcw, vpw, bias_fn, prev):
    q2 = _unpack_words(qw)
    kc2 = _unpack_words(kcw)
    vc2 = _unpack_words(vcw)
    if kpw is not None:
        kp2 = _unpack_words(kpw)
        vp2 = _unpack_words(vpw)
    lane = lax.broadcasted_iota(i32, (BAND, LANES), 1)
    left = lane < HEAD_DIM
    res = []
    for pr in range(2):
        q = q2[pr].astype(bf16)
        if kpw is not None:
            kk = jnp.concatenate([kp2[pr], kc2[pr]], axis=0).astype(bf16)
            vv = jnp.concatenate([vp2[pr], vc2[pr]], axis=0).astype(bf16)
        else:
            kk = kc2[pr].astype(bf16)
            vv = vc2[pr].astype(bf16)
        zq = jnp.zeros_like(q)
        qs = jnp.concatenate([jnp.where(left, q, zq), jnp.where(left, zq, q)], axis=0)
        s = lax.dot_general(qs, kk, (((1,), (1,)), ((), ())), preferred_element_type=f32)
        s = s + bias_fn(pr)
        m = jnp.max(s, axis=1, keepdims=True)
        p = jnp.exp(s - m)
        l = jnp.sum(p, axis=1, keepdims=True)
        o2 = _dot(p.astype(bf16), vv)
        l_pair = jnp.where(left, l[:BAND], l[BAND:])
        m_pair = jnp.where(left, m[:BAND], m[BAND:])
        o = jnp.where(left, o2[:BAND], o2[BAND:]) / l_pair
        lse = m_pair + jnp.log(l_pair)
        if prev is not None:
            po, pl_ = prev[pr]
            mx = jnp.maximum(pl_, lse)
            ea = jnp.exp(pl_ - mx)
            eb = jnp.exp(lse - mx)
            tot = ea + eb
            o = po * (ea / tot) + o * (eb / tot)
            lse = mx + jnp.log(tot)
        res.append((o, lse))
    return res


def _attn_kernel(q0, k0, v0, q1, k1, v1, q2, k2, v2, bias_ref, o_ref,
                 acc_lo, acc_hi, lse_lo, lse_hi, *, seq):
    state = ((acc_lo, lse_lo), (acc_hi, lse_hi))

    def load_state(rows):
        return tuple((a[rows, :], l[rows, :]) for a, l in state)

    def store_state(rows, res):
        for (a, l), (o, lse) in zip(state, res):
            a[rows, :] = o
            l[rows, :] = lse

    def bias2(g, first, pr, cols=None):
        if cols is None:
            b = bias_ref[g, first, pl.ds(2 * pr, 2)]
        else:
            b = bias_ref[g, first, pl.ds(2 * pr, 2), :, cols]
        return b.reshape(2 * BAND, b.shape[-1])

    d2 = DILATIONS[2]
    assert seq == d2 * BAND

    def body2(i, c):
        work = []
        for u in range(ATTN_BLOCKS_PER_STEP):
            cur = pl.ds(i * ATTN_BLOCKS_PER_STEP + u, BAND, stride=d2)
            work.append((cur, (q2[0, cur, :], k2[0, cur, :], None, v2[0, cur, :], None)))
        outs = [_attn_plane(*args, lambda pr: bias2(2, 0, pr, pl.ds(BAND, BAND)), None)
                for _, args in work]
        for (cur, _), res in zip(work, outs):
            store_state(cur, res)
        return c

    lax.fori_loop(0, d2 // ATTN_BLOCKS_PER_STEP, body2, 0)

    d1 = DILATIONS[1]
    nb1 = seq // (d1 * BAND)
    assert (d1 * nb1) % ATTN_BLOCKS_PER_STEP == 0 and d2 % ATTN_BLOCKS_PER_STEP == 0

    def body1(i, c):
        work = []
        for u in range(ATTN_BLOCKS_PER_STEP):
            idx = i * ATTN_BLOCKS_PER_STEP + u
            r = idx // nb1
            jb = idx % nb1
            st = r + jb * (d1 * BAND)
            sp = r + jnp.maximum(jb - 1, 0) * (d1 * BAND)
            first = jnp.where(jb == 0, 1, 0)
            cur = pl.ds(st, BAND, stride=d1)
            prv = pl.ds(sp, BAND, stride=d1)
            work.append((cur, first, (q1[0, cur, :], k1[0, cur, :], k1[0, prv, :],
                                      v1[0, cur, :], v1[0, prv, :]), load_state(cur)))
        outs = [_attn_plane(*args, lambda pr, first=first: bias2(1, first, pr), prev)
                for _, first, args, prev in work]
        for (cur, _, _, _), res in zip(work, outs):
            store_state(cur, res)
        return c

    lax.fori_loop(0, d1 * nb1 // ATTN_BLOCKS_PER_STEP, body1, 0)

    def body0(i, c):
        work = []
        for u in range(ATTN_BLOCKS_PER_STEP):
            jb = i * ATTN_BLOCKS_PER_STEP + u
            st = pl.multiple_of(jb * BAND, BAND)
            sp = pl.multiple_of(jnp.maximum(jb - 1, 0) * BAND, BAND)
            first = jnp.where(jb == 0, 1, 0)
            cur = pl.ds(st, BAND)
            work.append((cur, first, (q0[0, cur, :], k0[0, cur, :], k0[0, pl.ds(sp, BAND), :],
                                      v0[0, cur, :], v0[0, pl.ds(sp, BAND), :]), load_state(cur)))
        outs = [_attn_plane(*args, lambda pr, first=first: bias2(0, first, pr), prev)
                for _, first, args, prev in work]
        for (cur, _, _, _), res in zip(work, outs):
            o_ref[cur, :LANES] = res[0][0].astype(bf16)
            o_ref[cur, LANES:] = res[1][0].astype(bf16)
        return c

    lax.fori_loop(0, seq // BAND // ATTN_BLOCKS_PER_STEP, body0, 0)


def _attn_prompt(planes, bias_tab, batch, seq):
    n = batch * seq
    ppk = PLANES_PER_KIND

    def spec(g, kind):
        return pl.BlockSpec((1, seq, LANES), lambda b, p: ((g * 3 + kind) * ppk + p, b, 0))

    in_specs = [spec(g, kind) for g in range(N_DGROUPS) for kind in range(3)]
    in_specs.append(pl.BlockSpec((N_DGROUPS, 2, 4, BAND, 2 * BAND), lambda b, p: (0, 0, p, 0, 0)))
    return pl.pallas_call(
        functools.partial(_attn_kernel, seq=seq),
        grid=(batch, ppk),
        in_specs=in_specs,
        out_specs=pl.BlockSpec((seq, 2 * LANES), lambda b, p: (b, p)),
        out_shape=jax.ShapeDtypeStruct((n, HD), bf16),
        scratch_shapes=[pltpu.VMEM((seq, LANES), f32)] * 4,
        compiler_params=_cparams(("arbitrary", "arbitrary")),
        name="attn_prompt",
    )(*([planes] * 9), bias_tab)


def _attn_sample_kernel(q_ref, kn_ref, vn_ref, c0_ref, c1_ref, c2_ref, bs_ref, b0_ref, o_ref,
                        buf0, buf1, buf2, sems):
    b = pl.program_id(0)
    nb = pl.num_programs(0)
    caches = (c0_ref, c1_ref, c2_ref)
    bufs = (buf0, buf1, buf2)
    slot = b % 2

    def whole_window(g, bb, sl):
        first = caches[g].shape[2] - BAND * DILATIONS[g]
        return pltpu.make_async_copy(caches[g].at[0, bb, pl.ds(first, BAND)], bufs[g].at[sl],
                                     sems.at[g, sl])

    def start_window(g, bb, sl):
        d = DILATIONS[g]
        first = caches[g].shape[2] - BAND * d
        if d == 1:
            whole_window(g, bb, sl).start()
            return

        def one(j, c):
            pltpu.make_async_copy(caches[g].at[0, bb, pl.ds(first + j * d, 1)],
                                  bufs[g].at[sl, pl.ds(j, 1)], sems.at[g, sl]).start()
            return c
        lax.fori_loop(0, BAND, one, 0)

    @pl.when(b == 0)
    def _():
        for g in range(N_DGROUPS):
            start_window(g, b, slot)

    @pl.when(b + 1 < nb)
    def _():
        for g in range(N_DGROUPS):
            start_window(g, b + 1, 1 - slot)

    rb = lambda a: a.astype(bf16).astype(f32)
    ms, ls, accs = [], [], []
    for g in range(N_DGROUPS):
        q = rb(q_ref[g, 0])
        kn = rb(kn_ref[g, 0])
        vn = rb(vn_ref[g, 0])
        whole_window(g, b, slot).wait()
        kc = rb(bufs[g][slot, :, 0])
        vc = rb(bufs[g][slot, :, 1])
        s = jnp.sum(kc * q[None], axis=-1, keepdims=True) + bs_ref[g]
        s0 = jnp.sum(kn * q, axis=-1, keepdims=True) + b0_ref[g]
        m = jnp.maximum(jnp.max(s, axis=0), s0)
        p = jnp.exp(s - m[None])
        p0 = jnp.exp(s0 - m)
        ls.append(jnp.sum(p, axis=0) + p0)
        accs.append(jnp.sum(rb(p) * vc, axis=0) + rb(p0) * vn)
        ms.append(m)
    mm = jnp.maximum(jnp.maximum(ms[0], ms[1]), ms[2])
    num = jnp.zeros_like(accs[0])
    den = jnp.zeros_like(ls[0])
    for g in range(N_DGROUPS):
        w = jnp.exp(ms[g] - mm)
        num = num + w * accs[g]
        den = den + w * ls[g]
    o_ref[0] = num / den


def _attn_sample(q4, kn4, vn4, caches6, bias_s, bias_s0, nb):
    qspec = pl.BlockSpec((N_DGROUPS, 1, N_HEADS, HEAD_DIM), lambda b: (0, b, 0, 0))
    for g in range(N_DGROUPS):
        assert caches6[g].shape[2] >= DILATIONS[g] * BAND
    anyspec = pl.BlockSpec(memory_space=pl.ANY)
    buf = pltpu.VMEM((2, BAND, 2, N_HEADS, HEAD_DIM), f32)
    return pl.pallas_call(
        _attn_sample_kernel,
        grid=(nb,),
        in_specs=[qspec, qspec, qspec, anyspec, anyspec, anyspec,
                  pl.BlockSpec((N_DGROUPS, BAND, N_HEADS, 1), lambda b: (0, 0, 0, 0)),
                  pl.BlockSpec((N_DGROUPS, N_HEADS, 1), lambda b: (0, 0, 0))],
        out_specs=pl.BlockSpec((1, N_HEADS, HEAD_DIM), lambda b: (b, 0, 0)),
        out_shape=jax.ShapeDtypeStruct((nb, N_HEADS, HEAD_DIM), f32),
        scratch_shapes=[buf, buf, buf, pltpu.SemaphoreType.DMA((N_DGROUPS, 2))],
        compiler_params=_cparams(("arbitrary",)),
        name="attn_sample",
    )(q4, kn4, vn4, *caches6, bias_s, bias_s0)


SHIFT_ROWS = 256


def _shift_kernel(a_ref, nxt_ref, new_ref, o_ref, *, rows):
    i = pl.program_id(1)
    o_ref[0, 0, 0:rows - 1] = a_ref[0, 0, 1:rows]
    last = jnp.where(i == pl.num_programs(1) - 1, new_ref[0], nxt_ref[0, 0])
    o_ref[0, 0, rows - 1:rows] = last


def _shift_call(cache6, new_rows, nb):
    length = cache6.shape[2]
    rows = min(SHIFT_ROWS, length)
    assert length % rows == 0
    tail = (2, N_HEADS, HEAD_DIM)
    return pl.pallas_call(
        functools.partial(_shift_kernel, rows=rows),
        grid=(nb, length // rows),
        in_specs=[pl.BlockSpec((1, 1, rows) + tail, lambda b, i: (0, b, i, 0, 0, 0)),
                  pl.BlockSpec((1, 1, 1) + tail,
                               lambda b, i: (0, b, jnp.minimum((i + 1) * rows, length - 1), 0, 0, 0)),
                  pl.BlockSpec((1, 1) + tail, lambda b, i: (b, 0, 0, 0, 0))],
        out_specs=pl.BlockSpec((1, 1, rows) + tail, lambda b, i: (0, b, i, 0, 0, 0)),
        out_shape=jax.ShapeDtypeStruct(cache6.shape, cache6.dtype),
        compiler_params=_cparams(("arbitrary", "arbitrary")),
        name="kv_window_shift",
    )(cache6, cache6, new_rows)


def _route(u, wr_ref, br_ref):
    lg = _dot(u.astype(bf16), wr_ref[...]) + br_ref[...]
    rows = u.shape[0]
    lane = lax.broadcasted_iota(i32, (rows, LANES), 1)
    lanef = lane.astype(f32)
    big = jnp.float32(1e9)
    isg = lane < N_EGROUPS
    lgg = jnp.where(isg, lg, NEG)
    mg = jnp.max(lgg, axis=1, keepdims=True)
    eg = jnp.where(isg, jnp.exp(lgg - mg), 0.0)
    pg = eg / jnp.sum(eg, axis=1, keepdims=True)
    gw = jnp.max(pg, axis=1, keepdims=True)
    gi = jnp.min(jnp.where(isg & (pg == gw), lanef, big), axis=1, keepdims=True)
    grp_of_lane = ((lane - N_EGROUPS) >> 2).astype(f32)
    sel = (lane >= N_EGROUPS) & (lane < N_EGROUPS + N_EXPERTS) & (grp_of_lane == gi)
    les = jnp.where(sel, lg, NEG)
    v1 = jnp.max(les, axis=1, keepdims=True)
    i1 = jnp.min(jnp.where(sel & (les == v1), lanef, big), axis=1, keepdims=True)
    sel2 = sel & (lanef != i1)
    les2 = jnp.where(sel2, lg, NEG)
    v2 = jnp.max(les2, axis=1, keepdims=True)
    i2 = jnp.min(jnp.where(sel2 & (les2 == v2), lanef, big), axis=1, keepdims=True)
    ex = jnp.exp(v2 - v1)
    den = 1.0 + ex
    pw1 = (1.0 / den) * gw
    pw2 = (ex / den) * gw
    e1 = i1 - N_EGROUPS
    e2 = i2 - N_EGROUPS
    first_low = e1 < e2
    lo = jnp.minimum(e1, e2) - 4.0 * gi
    hi = jnp.maximum(e1, e2) - 4.0 * gi
    pair = jnp.where(lo == 0.0, hi - 1.0, jnp.where(lo == 1.0, hi + 1.0, 5.0))
    cls = gi * 6.0 + pair
    g_lo = jnp.where(first_low, pw1, pw2)
    g_hi = jnp.where(first_low, pw2, pw1)
    zero = jnp.zeros((rows, LANES), f32)
    return jnp.where(lane == 0, cls, jnp.where(lane == 1, g_lo, jnp.where(lane == 2, g_hi, zero)))


def _write_slab(slab_ref, u, meta):
    rows = u.shape[0]
    half = D_MODEL // 2
    for c in range(4):
        w = _pack_words(u[:, c * LANES:(c + 1) * LANES], u[:, half + c * LANES:half + (c + 1) * LANES])
        slab_ref[pl.ds(c, rows, stride=SUBLANES), :] = w
    slab_ref[pl.ds(4, rows, stride=SUBLANES), :] = lax.bitcast_convert_type(meta, u32)
    z = jnp.zeros((rows, LANES), u32)
    for c in range(5, SUBLANES):
        slab_ref[pl.ds(c, rows, stride=SUBLANES), :] = z


def _ffn_prep(h, gn_ref, wr_ref, br_ref, slab_ref, metat_ref):
    u = _rms(h, gn_ref[...])
    meta = _route(u, wr_ref, br_ref)
    _write_slab(slab_ref, u, meta)
    metat_ref[...] = meta.T[:SUBLANES, :]


def _from_slab_f32(ref, rows):
    return jnp.concatenate([ref[pl.ds(c, rows, stride=SUBLANES), :] for c in range(SUBLANES)], axis=1)


def _wo_kernel(o_ref, x_ref, wo_ref, gn_ref, wr_ref, br_ref, h_ref, slab_ref, metat_ref):
    h = x_ref[...] + _dot(o_ref[...], wo_ref[...])
    h_ref[...] = h
    _ffn_prep(h, gn_ref, wr_ref, br_ref, slab_ref, metat_ref)


def _router_specs():
    cmap = lambda i: (0, 0)
    return [pl.BlockSpec((1, D_MODEL), cmap),
            pl.BlockSpec((D_MODEL, LANES), cmap),
            pl.BlockSpec((1, LANES), cmap)]


def _prep_out(n, tm):
    specs = [pl.BlockSpec((tm, D_MODEL), lambda i: (i, 0)),
             pl.BlockSpec((tm * SUBLANES, LANES), lambda i: (i, 0)),
             pl.BlockSpec((SUBLANES, tm), lambda i: (0, i))]
    shapes = [jax.ShapeDtypeStruct((n, D_MODEL), f32),
              jax.ShapeDtypeStruct((n * SUBLANES, LANES), u32),
              jax.ShapeDtypeStruct((SUBLANES, n), f32)]
    return specs, shapes


def _wo_call(o_bf, x2, wo_bf, router, tm):
    n = x2.shape[0]
    ospecs, oshapes = _prep_out(n, tm)
    return pl.pallas_call(
        _wo_kernel,
        grid=(n // tm,),
        in_specs=[pl.BlockSpec((tm, HD), lambda i: (i, 0)),
                  pl.BlockSpec((tm, D_MODEL), lambda i: (i, 0)),
                  pl.BlockSpec((HD, D_MODEL), lambda i: (0, 0))] + _router_specs(),
        out_specs=ospecs,
        out_shape=oshapes,
        compiler_params=_cparams(("arbitrary",)),
        name="wo_residual",
    )(o_bf, x2, wo_bf, *router)


def _rank_kernel(metat_ref, rank_ref, cnt_ref, carry_ref, *, tr):
    @pl.when(pl.program_id(0) == 0)
    def _():
        carry_ref[...] = jnp.zeros_like(carry_ref)

    cls = metat_ref[0:1, :]
    sub = lax.broadcasted_iota(i32, (32, tr), 0).astype(f32)
    oh = jnp.where(sub == cls, 1.0, 0.0)
    rr = lax.broadcasted_iota(i32, (tr, tr), 0)
    cc = lax.broadcasted_iota(i32, (tr, tr), 1)
    upper = jnp.where(rr < cc, 1.0, 0.0).astype(bf16)
    pre = _dot(oh.astype(bf16), upper)
    carry = carry_ref[:, 0:1]
    rank_ref[...] = jnp.sum(oh * (pre + carry), axis=0, keepdims=True).astype(i32)
    newc = carry + jnp.sum(oh, axis=1, keepdims=True)
    carry_ref[...] = jnp.broadcast_to(newc, carry_ref.shape)
    cnt_ref[...] = carry_ref[...]


def _rank_call(metat, tr):
    n = metat.shape[1]
    return pl.pallas_call(
        functools.partial(_rank_kernel, tr=tr),
        grid=(n // tr,),
        in_specs=[pl.BlockSpec((SUBLANES, tr), lambda i: (0, i))],
        out_specs=[pl.BlockSpec((1, tr), lambda i: (0, i)),
                   pl.BlockSpec((32, LANES), lambda i: (0, 0))],
        out_shape=[jax.ShapeDtypeStruct((1, n), i32), jax.ShapeDtypeStruct((32, LANES), f32)],
        scratch_shapes=[pltpu.VMEM((32, LANES), f32)],
        compiler_params=_cparams(("arbitrary",)),
        name="class_rank",
    )(metat)


ROWS_PER_GATHER_STEP = 256


GATHER_UNROLL = 8


def _gather_kernel(idx_ref, src_ref, o_ref, sem, *, rows):
    base = pl.program_id(0) * rows

    def issue(kk, c):
        for u in range(GATHER_UNROLL):
            k = kk * GATHER_UNROLL + u
            t = idx_ref[base + k]
            pltpu.make_async_copy(
                src_ref.at[pl.ds(pl.multiple_of(t * SUBLANES, SUBLANES), SUBLANES)],
                o_ref.at[pl.ds(pl.multiple_of(k * SUBLANES, SUBLANES), SUBLANES)], sem).start()
        return c

    lax.fori_loop(0, rows // GATHER_UNROLL, issue, 0)
    pltpu.make_async_copy(src_ref.at[pl.ds(0, rows * SUBLANES)], o_ref, sem).wait()


def _gather_slabs(idx, src):
    n_out = idx.shape[0]
    rows = math.gcd(ROWS_PER_GATHER_STEP, n_out)
    assert rows % GATHER_UNROLL == 0 and src.shape[0] >= rows * SUBLANES
    return pl.pallas_call(
        functools.partial(_gather_kernel, rows=rows),
        grid_spec=pltpu.PrefetchScalarGridSpec(
            num_scalar_prefetch=1, grid=(n_out // rows,),
            in_specs=[pl.BlockSpec(memory_space=pl.ANY)],
            out_specs=pl.BlockSpec((rows * SUBLANES, LANES), lambda i, idx: (i, 0)),
            scratch_shapes=[pltpu.SemaphoreType.DMA(())]),
        out_shape=jax.ShapeDtypeStruct((n_out * SUBLANES, LANES), src.dtype),
        compiler_params=_cparams(("arbitrary",)),
        name="gather_slabs",
    )(idx, src)


def _expert_kernel(e1_ref, e2_ref, valid_ref, xs_ref, wg1, wu1, wd1, wg2, wu2, wd2, y_ref, *, te):
    del e1_ref, e2_ref
    valid = valid_ref[pl.program_id(0)]

    @pl.when(valid == 0)
    def _():
        y_ref[...] = jnp.zeros_like(y_ref)

    @pl.when(valid > 0)
    def _():
        los, his = [], []
        for c in range(4):
            lo, hi = _unpack_words(xs_ref[pl.ds(c, te, stride=SUBLANES), :])
            los.append(lo)
            his.append(hi)
        x = jnp.concatenate(los + his, axis=1).astype(bf16)
        meta = lax.bitcast_convert_type(xs_ref[pl.ds(4, te, stride=SUBLANES), :], f32)
        y = None
        for wg, wu, wd, lane in ((wg1, wu1, wd1, 1), (wg2, wu2, wd2, 2)):
            a = _dot(x, wg[0])
            c = _dot(x, wu[0])
            hh = (a * (1.0 / (1.0 + jnp.exp(-a)))) * c * meta[:, lane:lane + 1]
            part = _dot(hh.astype(bf16), wd[0])
            y = part if y is None else y + part
        for c in range(SUBLANES):
            y_ref[pl.ds(c, te, stride=SUBLANES), :] = y[:, c * LANES:(c + 1) * LANES]


def _expert_call(te1, te2, tvalid, xs, wg, wu, wd, te):
    ntiles = te1.shape[0]

    def wspec(shape, which):
        if which == 1:
            return pl.BlockSpec(shape, lambda i, e1, e2, v: (e1[i], 0, 0))
        return pl.BlockSpec(shape, lambda i, e1, e2, v: (e2[i], 0, 0))

    gs = (1, D_MODEL, D_EXPERT)
    ds_ = (1, D_EXPERT, D_MODEL)
    return pl.pallas_call(
        functools.partial(_expert_kernel, te=te),
        grid_spec=pltpu.PrefetchScalarGridSpec(
            num_scalar_prefetch=3, grid=(ntiles,),
            in_specs=[pl.BlockSpec((te * SUBLANES, LANES), lambda i, e1, e2, v: (i, 0)),
                      wspec(gs, 1), wspec(gs, 1), wspec(ds_, 1),
                      wspec(gs, 2), wspec(gs, 2), wspec(ds_, 2)],
            out_specs=pl.BlockSpec((te * SUBLANES, LANES), lambda i, e1, e2, v: (i, 0))),
        out_shape=jax.ShapeDtypeStruct((ntiles * te * SUBLANES, LANES), f32),
        compiler_params=_cparams(("arbitrary",)),
        name="expert_pair_ffn",
    )(te1, te2, tvalid, xs, wg, wu, wd, wg, wu, wd)


def _moe(slab, metat, wg, wu, wd, te, tr):
    n = metat.shape[1]
    rank, cnt = _rank_call(metat, tr)
    counts = cnt[:N_CLASSES, 0].astype(i32)
    padded = ((counts + te - 1) // te) * te
    ends = jnp.cumsum(padded)
    off = ends - padded
    cls = metat[0].astype(i32)
    class_ids = jnp.arange(N_CLASSES, dtype=i32)
    pos = rank[0] + jnp.sum(jnp.where(cls[:, None] == class_ids[None, :], off[None, :], 0), axis=1)
    ntiles = -(-n // te) + N_CLASSES
    tstart = jnp.arange(ntiles, dtype=i32) * te
    tcls = jnp.minimum(jnp.sum((tstart[:, None] >= ends[None, :]).astype(i32), axis=1), N_CLASSES - 1)
    tvalid = (tstart < ends[-1]).astype(i32)
    te1 = jnp.asarray(_CLASS_E1)[tcls]
    te2 = jnp.asarray(_CLASS_E2)[tcls]
    src = jnp.zeros((ntiles * te,), i32).at[pos].set(jnp.arange(n, dtype=i32), unique_indices=True)
    xs = _gather_slabs(src, slab)
    ys = _expert_call(te1, te2, tvalid, xs, wg, wu, wd, te)
    return _gather_slabs(pos, ys)


def _pool_mix_tail(hf, up, ext, tpos0, pw_ref, pb_ref, ps_ref):
    rows = up.shape[0]
    tpos = tpos0 + lax.broadcasted_iota(i32, (rows, 1), 0)
    parts = []
    for gi, win in enumerate(POOL_WINDOWS):
        sl = slice(gi * POOL_CH, (gi + 1) * POOL_CH)
        s = ext[:, sl]
        for kk in range(gi + 1):
            s = s + pltpu.roll(s, 1 << kk, 0)
        cnt = jnp.minimum(win, tpos + 1).astype(f32)
        z = s[16:, :] / cnt - up[:, sl]
        parts.append(_dot(z.astype(bf16), pw_ref[gi]))
    y = (jnp.concatenate(parts, axis=1) + pb_ref[...]) * ps_ref[...]
    return hf + y


def _pool_kernel(h_ref, y_ref, hh_ref, yh_ref, pn_ref, pw_ref, pb_ref, ps_ref,
                 gn_ref, wr_ref, br_ref, h2_ref, slab_ref, metat_ref, st_ref, *, tm, tps):
    i = pl.program_id(0)
    r = i % tps
    hf = h_ref[...] + _from_slab_f32(y_ref, tm)
    up = _rms(hf, pn_ref[...])
    hh = hh_ref[...] + _from_slab_f32(yh_ref, 16)
    uph = jnp.where(r > 0, _rms(hh, pn_ref[...]), 0.0)
    ext = jnp.concatenate([uph, up], axis=0)
    h2 = _pool_mix_tail(hf, up, ext, r * tm, pw_ref, pb_ref, ps_ref)
    h2_ref[...] = h2
    st_ref[...] = up[tm - 16:, :]
    _ffn_prep(h2, gn_ref, wr_ref, br_ref, slab_ref, metat_ref)


def _pool_call(h, ytt, pn, pw_bf, pb, psc, router, batch, seq, tm):
    n = h.shape[0]
    tps = seq // tm
    ospecs, oshapes = _prep_out(n, tm)
    ospecs.append(pl.BlockSpec((16, D_MODEL), lambda i: (i // tps, 0)))
    oshapes.append(jax.ShapeDtypeStruct((batch * 16, D_MODEL), f32))
    halo = lambda i: (jnp.maximum(i * (tm // 16) - 1, 0), 0)
    cm = lambda i: (0, 0)
    return pl.pallas_call(
        functools.partial(_pool_kernel, tm=tm, tps=tps),
        grid=(n // tm,),
        in_specs=[pl.BlockSpec((tm, D_MODEL), lambda i: (i, 0)),
                  pl.BlockSpec((tm * SUBLANES, LANES), lambda i: (i, 0)),
                  pl.BlockSpec((16, D_MODEL), halo),
                  pl.BlockSpec((16 * SUBLANES, LANES), halo),
                  pl.BlockSpec((1, D_MODEL), cm),
                  pl.BlockSpec((4, POOL_CH, POOL_CH), lambda i: (0, 0, 0)),
                  pl.BlockSpec((1, D_MODEL), cm),
                  pl.BlockSpec((1, D_MODEL), cm)] + _router_specs(),
        out_specs=ospecs,
        out_shape=oshapes,
        compiler_params=_cparams(("arbitrary",)),
        name="pool_mixer",
    )(h, ytt, h, ytt, pn, pw_bf, pb, psc, *router)


def _pool_sample_kernel(h_ref, y_ref, st_ref, pn_ref, pw_ref, pb_ref, ps_ref,
                        gn_ref, wr_ref, br_ref, h2_ref, slab_ref, metat_ref, nst_ref, *, nb):
    rows = h_ref.shape[0]
    hf = h_ref[...] + _from_slab_f32(y_ref, rows)
    up = _rms(hf, pn_ref[...])
    st = st_ref[...]
    ub = up[:nb, :]
    parts = []
    for gi, win in enumerate(POOL_WINDOWS):
        sl = slice(gi * POOL_CH, (gi + 1) * POOL_CH)
        tot = ub[:, sl] + jnp.sum(st[:, POOL_STATE - (win - 1):, sl], axis=1)
        z = tot / float(win) - ub[:, sl]
        parts.append(_dot(z.astype(bf16), pw_ref[gi]))
    y = (jnp.concatenate(parts, axis=1) + pb_ref[...]) * ps_ref[...]
    if rows > nb:
        y = jnp.concatenate([y, jnp.zeros((rows - nb, D_MODEL), f32)], axis=0)
    h2 = hf + y
    h2_ref[...] = h2
    nst_ref[:, :POOL_STATE - 1, :] = st[:, 1:, :]
    nst_ref[:, POOL_STATE - 1, :] = ub
    _ffn_prep(h2, gn_ref, wr_ref, br_ref, slab_ref, metat_ref)


def _pool_sample_call(h, ytt, state, pn, pw_bf, pb, psc, router, nb):
    rows = h.shape[0]
    ospecs, oshapes = _prep_out(rows, rows)
    ospecs.append(pl.BlockSpec((nb, POOL_STATE, D_MODEL), lambda i: (0, 0, 0)))
    oshapes.append(jax.ShapeDtypeStruct((nb, POOL_STATE, D_MODEL), f32))
    cm = lambda i: (0, 0)
    return pl.pallas_call(
        functools.partial(_pool_sample_kernel, nb=nb),
        grid=(1,),
        in_specs=[pl.BlockSpec((rows, D_MODEL), cm),
                  pl.BlockSpec((rows * SUBLANES, LANES), cm),
                  pl.BlockSpec((nb, POOL_STATE, D_MODEL), lambda i: (0, 0, 0)),
                  pl.BlockSpec((1, D_MODEL), cm),
                  pl.BlockSpec((4, POOL_CH, POOL_CH), lambda i: (0, 0, 0)),
                  pl.BlockSpec((1, D_MODEL), cm),
                  pl.BlockSpec((1, D_MODEL), cm)] + _router_specs(),
        out_specs=ospecs,
        out_shape=oshapes,
        compiler_params=_cparams(("arbitrary",)),
        name="pool_mixer_sample",
    )(h, ytt, state, pn, pw_bf, pb, psc, *router)


def _final_kernel(h_ref, y_ref, o_ref, *, tm):
    o_ref[...] = h_ref[...] + _from_slab_f32(y_ref, tm)


def _final_call(h, ytt, tm):
    n = h.shape[0]
    return pl.pallas_call(
        functools.partial(_final_kernel, tm=tm),
        grid=(n // tm,),
        in_specs=[pl.BlockSpec((tm, D_MODEL), lambda i: (i, 0)),
                  pl.BlockSpec((tm * SUBLANES, LANES), lambda i: (i, 0))],
        out_specs=pl.BlockSpec((tm, D_MODEL), lambda i: (i, 0)),
        out_shape=jax.ShapeDtypeStruct((n, D_MODEL), f32),
        compiler_params=_cparams(("arbitrary",)),
        name="final_residual",
    )(h, ytt)


def _t5_bucket(dist):
    max_exact = N_BUCKETS // 2
    df = jnp.maximum(dist, max_exact).astype(f32)
    large = max_exact + (jnp.log(df / max_exact) / math.log(MAX_DISTANCE / max_exact)
                         * (N_BUCKETS - max_exact)).astype(i32)
    large = jnp.minimum(large, N_BUCKETS - 1)
    return jnp.where(dist < max_exact, dist, large)


def _prompt_bias_tables(rel_bias):
    qi = jnp.arange(BAND)[:, None]
    kj = jnp.arange(2 * BAND)[None, :]
    step = BAND + qi - kj
    tabs = []
    for g, (win, dil) in enumerate(zip(WINDOWS, DILATIONS)):
        n_step = win // dil
        tab = rel_bias[:, g * N_HEADS:(g + 1) * N_HEADS].astype(f32)
        svals = jnp.maximum(2 * BAND - 1 - jnp.arange(3 * BAND), 0)
        bvec = tab[_t5_bucket(svals * dil)].T
        bias = jnp.stack([bvec[:, BAND - 1 - q:3 * BAND - 1 - q] for q in range(BAND)], axis=1)
        band = (step >= 0) & (step <= n_step)
        reg = jnp.where(band[None], bias, NEG)
        first = jnp.where((band & (kj >= BAND))[None], bias, NEG)
        tabs.append(jnp.stack([reg, first]))
    return jnp.stack(tabs)


def _sample_bias_tables(rel_bias):
    bs, b0 = [], []
    for g, (win, dil) in enumerate(zip(WINDOWS, DILATIONS)):
        n_step = win // dil
        assert n_step == BAND
        tab = rel_bias[:, g * N_HEADS:(g + 1) * N_HEADS].astype(f32)
        steps = jnp.arange(n_step + 1)
        b = tab[_t5_bucket(steps * dil)]
        b0.append(b[0])
        bs.append(b[1:][::-1])
    return jnp.stack(bs)[..., None], jnp.stack(b0)[..., None]


def _router_weights(ffn_norm_i, wgr, bgr, wer, ber):
    w = jnp.zeros((D_MODEL, LANES), f32)
    w = w.at[:, :N_EGROUPS].set(wgr).at[:, N_EGROUPS:N_EGROUPS + N_EXPERTS].set(wer)
    b = jnp.zeros((1, LANES), f32)
    b = b.at[0, :N_EGROUPS].set(bgr).at[0, N_EGROUPS:N_EGROUPS + N_EXPERTS].set(ber)
    return ffn_norm_i.reshape(1, D_MODEL), w.astype(bf16), b


def _norm_mats():
    head_of_col = np.arange(HD) // HEAD_DIM
    s = (head_of_col[:, None] == np.arange(LANES)[None, :]).astype(np.float32)
    e = s.T
    return jnp.asarray(s, bf16), jnp.asarray(np.concatenate([e, e], axis=0), bf16)


def kernel(x_prompt, x_sample, cache_kv_w128, cache_kv_w512, cache_kv_w2048, state_pool, w_qkv, q_norm, k_norm, w_o, rel_bias, attn_norm, pool_norm, pool_w, pool_b, pool_scale, ffn_norm, router_group_w, router_group_b, router_expert_w, router_expert_b, w_gate, w_up, w_down):
    batch, seq, _ = x_prompt.shape
    nb = x_sample.shape[0]
    assert x_sample.shape[1] == 1
    n = batch * seq
    srows = LANES
    caches6 = (cache_kv_w128, cache_kv_w512, cache_kv_w2048)

    smat, e2mat = _norm_mats()
    wqkv_bf = w_qkv[0].astype(bf16)
    wo_bf = w_o[0].astype(bf16)
    gq = jnp.tile(q_norm[0], (1, N_HEADS)).reshape(N_DGROUPS, 1, HD)
    gk = jnp.tile(k_norm[0], (1, N_HEADS)).reshape(N_DGROUPS, 1, HD)
    an = attn_norm[0].reshape(1, D_MODEL)
    routers = [_router_weights(ffn_norm[i], router_group_w[i], router_group_b[i],
                               router_expert_w[i], router_expert_b[i]) for i in range(2)]
    wg_bf = [w_gate[i].astype(bf16) for i in range(2)]
    wu_bf = [w_up[i].astype(bf16) for i in range(2)]
    wd_bf = [w_down[i].astype(bf16) for i in range(2)]
    pw_bf = pool_w[0].astype(bf16)
    pn = pool_norm[0].reshape(1, D_MODEL)
    pb = pool_b[0].reshape(1, D_MODEL)
    psc = pool_scale[0].reshape(1, D_MODEL)

    x2 = x_prompt.reshape(n, D_MODEL)
    planes, kv2, kv1, kv0 = _qkv_prompt(x2, an, wqkv_bf, gq, gk, smat, e2mat, batch, seq)
    o_bf = _attn_prompt(planes, _prompt_bias_tables(rel_bias), batch, seq)
    h0, slab0, metat0 = _wo_call(o_bf, x2, wo_bf, routers[0], tm=512)
    ytt0 = _moe(slab0, metat0, wg_bf[0], wu_bf[0], wd_bf[0], te=256, tr=512)
    h1, slab1, metat1, pst = _pool_call(h0, ytt0, pn, pw_bf, pb, psc, routers[1], batch, seq, tm=512)
    ytt1 = _moe(slab1, metat1, wg_bf[1], wu_bf[1], wd_bf[1], te=256, tr=512)
    y_prompt = _final_call(h1, ytt1, tm=512).reshape(batch, seq, D_MODEL)

    kvshape = lambda keep: (1, batch, keep, 2, N_HEADS, HEAD_DIM)
    new_kv_p = (kv0.reshape(kvshape(WINDOWS[0])), kv1.reshape(kvshape(WINDOWS[1])),
                kv2.reshape(kvshape(seq)))
    new_pool_p = pst.reshape(batch, 16, D_MODEL)[:, 16 - POOL_STATE:][None]

    xs = jnp.zeros((srows, D_MODEL), f32).at[:nb].set(x_sample[:, 0, :])
    qs, ks, vs = _qkv_sample(xs, an, wqkv_bf, gq, gk, smat, e2mat)
    to4 = lambda a: a[:, :nb].reshape(N_DGROUPS, nb, N_HEADS, HEAD_DIM)
    ks4, vs4 = to4(ks), to4(vs)
    bias_s, bias_s0 = _sample_bias_tables(rel_bias)
    os_ = _attn_sample(to4(qs), ks4, vs4, caches6, bias_s, bias_s0, nb)
    os_pad = jnp.zeros((srows, HD), bf16).at[:nb].set(os_.reshape(nb, HD).astype(bf16))
    hs0, sslab0, smetat0 = _wo_call(os_pad, xs, wo_bf, routers[0], tm=srows)
    ystt0 = _moe(sslab0, smetat0, wg_bf[0], wu_bf[0], wd_bf[0], te=64, tr=srows)
    hs1, sslab1, smetat1, new_pool_s = _pool_sample_call(
        hs0, ystt0, state_pool[0], pn, pw_bf, pb, psc, routers[1], nb)
    ystt1 = _moe(sslab1, smetat1, wg_bf[1], wu_bf[1], wd_bf[1], te=64, tr=srows)
    y_sample = _final_call(hs1, ystt1, tm=srows)[:nb].reshape(nb, 1, D_MODEL)

    rows = [jnp.stack([ks4[g], vs4[g]], axis=1)[:, None] for g in range(N_DGROUPS)]
    new_kv_s = [_shift_call(caches6[g], rows[g], nb) for g in range(N_DGROUPS)]

    return (y_prompt, y_sample, new_kv_p[0], new_kv_p[1], new_kv_p[2], new_pool_p,
            new_kv_s[0], new_kv_s[1], new_kv_s[2], new_pool_s[None])
```

```python
import functools
import math

import numpy as np
import jax
import jax.numpy as jnp
from jax import lax
from jax.experimental import pallas as pl
from jax.experimental.pallas import tpu as pltpu

f32 = jnp.float32
bf16 = jnp.bfloat16
u32 = jnp.uint32
i32 = jnp.int32

D_MODEL = 1024
N_HEADS = 16
HEAD_DIM = 64
HD = N_HEADS * HEAD_DIM
N_DGROUPS = 3
WINDOWS = (128, 512, 2048)
DILATIONS = (1, 4, 16)
BAND = 128
N_BUCKETS = 32
MAX_DISTANCE = 2048
POOL_WINDOWS = (2, 4, 8, 16)
POOL_CH = D_MODEL // 4
POOL_STATE = max(POOL_WINDOWS) - 1
N_EGROUPS = 4
EXPERTS_PER_GROUP = 4
N_EXPERTS = 16
D_EXPERT = D_MODEL // 2
RMS_EPS = 1e-6
NEG = -1e30

LANES = 128
SUBLANES = 8
HI_MASK = 0xFFFF0000
N_CLASSES = N_EGROUPS * 6
PLANES_PER_KIND = HD // (2 * LANES)
VMEM_LIMIT = 56 * 1024 * 1024

_PAIRS = [(a, b) for a in range(4) for b in range(a + 1, 4)]
_CLASS_E1 = np.array([g * 4 + _PAIRS[p][0] for g in range(4) for p in range(6)], np.int32)
_CLASS_E2 = np.array([g * 4 + _PAIRS[p][1] for g in range(4) for p in range(6)], np.int32)


def _cparams(sem=None, vmem=VMEM_LIMIT):
    kw = dict(vmem_limit_bytes=vmem)
    if sem is not None:
        kw["dimension_semantics"] = sem
    return pltpu.CompilerParams(**kw)


def _dot(a, b):
    return jnp.dot(a, b, preferred_element_type=f32)


def _rms(x, g):
    return x * lax.rsqrt(jnp.mean(x * x, axis=-1, keepdims=True) + RMS_EPS) * g


def _pack_words(lo, hi):
    lb = lax.bitcast_convert_type(lo.astype(bf16).astype(f32), u32) >> 16
    hb = lax.bitcast_convert_type(hi.astype(bf16).astype(f32), u32) & jnp.uint32(HI_MASK)
    return lb | hb


def _unpack_words(w):
    lo = lax.bitcast_convert_type(w << 16, f32)
    hi = lax.bitcast_convert_type(w & jnp.uint32(HI_MASK), f32)
    return lo, hi


def _head_norm(z, gain, s_ref, e2_ref):
    zz = z * z
    zz_hi = zz.astype(bf16)
    zz_lo = (zz - zz_hi.astype(f32)).astype(bf16)
    ss = _dot(zz_hi, s_ref[...]) + _dot(zz_lo, s_ref[...])
    r = lax.rsqrt(ss * (1.0 / HEAD_DIM) + RMS_EPS)
    r_hi = r.astype(bf16)
    r_lo = (r - r_hi.astype(f32)).astype(bf16)
    rb = _dot(jnp.concatenate([r_hi, r_lo], axis=1), e2_ref[...])
    return z * rb * gain


def _qkv_kernel(x_ref, gn_ref, w_ref, gq_ref, gk_ref, s_ref, e2_ref,
                planes_ref, kv2_ref, kv1_ref, kv0_ref, *, tm):
    j = pl.program_id(0)
    u = _rms(x_ref[...], gn_ref[...]).astype(bf16)
    z = _dot(u, w_ref[...])
    q = _head_norm(z[:, :HD], gq_ref[0], s_ref, e2_ref) * (HEAD_DIM ** -0.5)
    k = _head_norm(z[:, HD:2 * HD], gk_ref[0], s_ref, e2_ref)
    v = z[:, 2 * HD:]
    for kind, arr in enumerate((q, k, v)):
        for p in range(PLANES_PER_KIND):
            c0 = 2 * LANES * p
            planes_ref[kind * PLANES_PER_KIND + p] = _pack_words(
                arr[:, c0:c0 + LANES], arr[:, c0 + LANES:c0 + 2 * LANES])

    def store_t(ref, kk, vv):
        ref[0, 0, 0] = kk.T.reshape(N_HEADS, HEAD_DIM, kk.shape[0])
        ref[0, 0, 1] = vv.T.reshape(N_HEADS, HEAD_DIM, vv.shape[0])

    @pl.when(j == 2)
    def _():
        store_t(kv2_ref, k, v)

    @pl.when(j == 1)
    def _():
        store_t(kv1_ref, k, v)

    @pl.when(j == 0)
    def _():
        store_t(kv0_ref, k[tm - BAND:, :], v[tm - BAND:, :])


def _qkv_prompt(x2, gn, w_bf, gq, gk, smat, e2mat, batch, seq, tm=256):
    n = x2.shape[0]
    nt = n // tm
    tps = seq // tm
    t512 = WINDOWS[1] // tm
    assert seq % tm == 0 and WINDOWS[1] % tm == 0 and tm >= BAND and seq == WINDOWS[2]

    def kv2_map(j, i):
        on = j == 2
        return (0, jnp.where(on, i // tps, 0), 0, 0, 0, jnp.where(on, i % tps, 0))

    def kv1_map(j, i):
        inside = jnp.maximum(i % tps - (tps - t512), 0)
        b = jnp.where(j == 1, i // tps, jnp.where(j < 1, 0, batch - 1))
        t = jnp.where(j == 1, inside, jnp.where(j < 1, 0, t512 - 1))
        return (0, b, 0, 0, 0, t)

    def kv0_map(j, i):
        return (0, jnp.where(j == 0, i // tps, batch - 1), 0, 0, 0, 0)

    nplanes = 3 * PLANES_PER_KIND
    kvt = lambda keep: (1, batch, 2, N_HEADS, HEAD_DIM, keep)
    kvblock = lambda cols: (1, 1, 2, N_HEADS, HEAD_DIM, cols)
    return pl.pallas_call(
        functools.partial(_qkv_kernel, tm=tm),
        grid=(N_DGROUPS, nt),
        in_specs=[
            pl.BlockSpec((tm, D_MODEL), lambda j, i: (i, 0)),
            pl.BlockSpec((1, D_MODEL), lambda j, i: (0, 0)),
            pl.BlockSpec((D_MODEL, 3 * HD), lambda j, i: (0, j)),
            pl.BlockSpec((1, 1, HD), lambda j, i: (j, 0, 0)),
            pl.BlockSpec((1, 1, HD), lambda j, i: (j, 0, 0)),
            pl.BlockSpec((HD, LANES), lambda j, i: (0, 0)),
            pl.BlockSpec((2 * LANES, HD), lambda j, i: (0, 0)),
        ],
        out_specs=[
            pl.BlockSpec((nplanes, tm, LANES), lambda j, i: (j, i, 0)),
            pl.BlockSpec(kvblock(tm), kv2_map),
            pl.BlockSpec(kvblock(tm), kv1_map),
            pl.BlockSpec(kvblock(BAND), kv0_map),
        ],
        out_shape=[
            jax.ShapeDtypeStruct((N_DGROUPS * nplanes, n, LANES), u32),
            jax.ShapeDtypeStruct(kvt(WINDOWS[2]), f32),
            jax.ShapeDtypeStruct(kvt(WINDOWS[1]), f32),
            jax.ShapeDtypeStruct(kvt(WINDOWS[0]), f32),
        ],
        compiler_params=_cparams(("arbitrary", "arbitrary")),
        name="qkv_prompt",
    )(x2, gn, w_bf, gq, gk, smat, e2mat)


def _qkv_sample_kernel(x_ref, gn_ref, w_ref, gq_ref, gk_ref, s_ref, e2_ref, q_ref, k_ref, v_ref):
    u = _rms(x_ref[...], gn_ref[...]).astype(bf16)
    z = _dot(u, w_ref[...])
    q_ref[0] = _head_norm(z[:, :HD], gq_ref[0], s_ref, e2_ref) * (HEAD_DIM ** -0.5)
    k_ref[0] = _head_norm(z[:, HD:2 * HD], gk_ref[0], s_ref, e2_ref)
    v_ref[0] = z[:, 2 * HD:]


def _qkv_sample(xs, gn, w_bf, gq, gk, smat, e2mat):
    rows = xs.shape[0]
    out = jax.ShapeDtypeStruct((N_DGROUPS, rows, HD), f32)
    ospec = pl.BlockSpec((1, rows, HD), lambda j: (j, 0, 0))
    return pl.pallas_call(
        _qkv_sample_kernel,
        grid=(N_DGROUPS,),
        in_specs=[
            pl.BlockSpec((rows, D_MODEL), lambda j: (0, 0)),
            pl.BlockSpec((1, D_MODEL), lambda j: (0, 0)),
            pl.BlockSpec((D_MODEL, 3 * HD), lambda j: (0, j)),
            pl.BlockSpec((1, 1, HD), lambda j: (j, 0, 0)),
            pl.BlockSpec((1, 1, HD), lambda j: (j, 0, 0)),
            pl.BlockSpec((HD, LANES), lambda j: (0, 0)),
            pl.BlockSpec((2 * LANES, HD), lambda j: (0, 0)),
        ],
        out_specs=[ospec, ospec, ospec],
        out_shape=[out, out, out],
        compiler_params=_cparams(("arbitrary",)),
        name="qkv_sample",
    )(xs, gn, w_bf, gq, gk, smat, e2mat)


ATTN_BLOCKS_PER_STEP = 8


def _attn_plane(qw, kcw, kpw, vcw, vpw, bias_fn, prev):
    q2 = _unpack_words(qw)
    kc2 = _unpack_words(kcw)
    vc2 = _unpack_words(vcw)
    if kpw is not None:
        kp2 = _unpack_words(kpw)
        vp2 = _unpack_words(vpw)
    lane = lax.broadcasted_iota(i32, (BAND, LANES), 1)
    left = lane < HEAD_DIM
    res = []
    for pr in range(2):
        q = q2[pr].astype(bf16)
        if kpw is not None:
            kk = jnp.concatenate([kp2[pr], kc2[pr]], axis=0).astype(bf16)
            vv = jnp.concatenate([vp2[pr], vc2[pr]], axis=0).astype(bf16)
        else:
            kk = kc2[pr].astype(bf16)
            vv = vc2[pr].astype(bf16)
        zq = jnp.zeros_like(q)
        qs = jnp.concatenate([jnp.where(left, q, zq), jnp.where(left, zq, q)], axis=0)
        s = lax.dot_general(qs, kk, (((1,), (1,)), ((), ())), preferred_element_type=f32)
        s = s + bias_fn(pr)
        m = jnp.max(s, axis=1, keepdims=True)
        p = jnp.exp(s - m)
        l = jnp.sum(p, axis=1, keepdims=True)
        o2 = _dot(p.astype(bf16), vv)
        l_pair = jnp.where(left, l[:BAND], l[BAND:])
        m_pair = jnp.where(left, m[:BAND], m[BAND:])
        o = jnp.where(left, o2[:BAND], o2[BAND:]) / l_pair
        lse = m_pair + jnp.log(l_pair)
        if prev is not None:
            po, pl_ = prev[pr]
            mx = jnp.maximum(pl_, lse)
            ea = jnp.exp(pl_ - mx)
            eb = jnp.exp(lse - mx)
            tot = ea + eb
            o = po * (ea / tot) + o * (eb / tot)
            lse = mx + jnp.log(tot)
        res.append((o, lse))
    return res


def _attn_kernel(q0, k0, v0, q1, k1, v1, q2, k2, v2, bias_ref, o_ref,
                 acc_lo, acc_hi, lse_lo, lse_hi, *, seq):
    state = ((acc_lo, lse_lo), (acc_hi, lse_hi))

    def load_state(rows):
        return tuple((a[rows, :], l[rows, :]) for a, l in state)

    def store_state(rows, res):
        for (a, l), (o, lse) in zip(state, res):
            a[rows, :] = o
            l[rows, :] = lse

    def bias2(g, first, pr, cols=None):
        if cols is None:
            b = bias_ref[g, first, pl.ds(2 * pr, 2)]
        else:
            b = bias_ref[g, first, pl.ds(2 * pr, 2), :, cols]
        return b.reshape(2 * BAND, b.shape[-1])

    d2 = DILATIONS[2]
    assert seq == d2 * BAND

    def body2(i, c):
        work = []
        for u in range(ATTN_BLOCKS_PER_STEP):
            cur = pl.ds(i * ATTN_BLOCKS_PER_STEP + u, BAND, stride=d2)
            work.append((cur, (q2[0, cur, :], k2[0, cur, :], None, v2[0, cur, :], None)))
        outs = [_attn_plane(*args, lambda pr: bias2(2, 0, pr, pl.ds(BAND, BAND)), None)
                for _, args in work]
        for (cur, _), res in zip(work, outs):
            store_state(cur, res)
        return c

    lax.fori_loop(0, d2 // ATTN_BLOCKS_PER_STEP, body2, 0)

    d1 = DILATIONS[1]
    nb1 = seq // (d1 * BAND)
    assert (d1 * nb1) % ATTN_BLOCKS_PER_STEP == 0 and d2 % ATTN_BLOCKS_PER_STEP == 0

    def body1(i, c):
        work = []
        for u in range(ATTN_BLOCKS_PER_STEP):
            idx = i * ATTN_BLOCKS_PER_STEP + u
            r = idx // nb1
            jb = idx % nb1
            st = r + jb * (d1 * BAND)
            sp = r + jnp.maximum(jb - 1, 0) * (d1 * BAND)
            first = jnp.where(jb == 0, 1, 0)
            cur = pl.ds(st, BAND, stride=d1)
            prv = pl.ds(sp, BAND, stride=d1)
            work.append((cur, first, (q1[0, cur, :], k1[0, cur, :], k1[0, prv, :],
                                      v1[0, cur, :], v1[0, prv, :]), load_state(cur)))
        outs = [_attn_plane(*args, lambda pr, first=first: bias2(1, first, pr), prev)
                for _, first, args, prev in work]
        for (cur, _, _, _), res in zip(work, outs):
            store_state(cur, res)
        return c

    lax.fori_loop(0, d1 * nb1 // ATTN_BLOCKS_PER_STEP, body1, 0)

    def body0(i, c):
        work = []
        for u in range(ATTN_BLOCKS_PER_STEP):
            jb = i * ATTN_BLOCKS_PER_STEP + u
            st = pl.multiple_of(jb * BAND, BAND)
            sp = pl.multiple_of(jnp.maximum(jb - 1, 0) * BAND, BAND)
            first = jnp.where(jb == 0, 1, 0)
            cur = pl.ds(st, BAND)
            work.append((cur, first, (q0[0, cur, :], k0[0, cur, :], k0[0, pl.ds(sp, BAND), :],
                                      v0[0, cur, :], v0[0, pl.ds(sp, BAND), :]), load_state(cur)))
        outs = [_attn_plane(*args, lambda pr, first=first: bias2(0, first, pr), prev)
                for _, first, args, prev in work]
        for (cur, _, _, _), res in zip(work, outs):
            o_ref[cur, :LANES] = res[0][0].astype(bf16)
            o_ref[cur, LANES:] = res[1][0].astype(bf16)
        return c

    lax.fori_loop(0, seq // BAND // ATTN_BLOCKS_PER_STEP, body0, 0)


def _attn_prompt(planes, bias_tab, batch, seq):
    n = batch * seq
    ppk = PLANES_PER_KIND

    def spec(g, kind):
        return pl.BlockSpec((1, seq, LANES), lambda b, p: ((g * 3 + kind) * ppk + p, b, 0))

    in_specs = [spec(g, kind) for g in range(N_DGROUPS) for kind in range(3)]
    in_specs.append(pl.BlockSpec((N_DGROUPS, 2, 4, BAND, 2 * BAND), lambda b, p: (0, 0, p, 0, 0)))
    return pl.pallas_call(
        functools.partial(_attn_kernel, seq=seq),
        grid=(batch, ppk),
        in_specs=in_specs,
        out_specs=pl.BlockSpec((seq, 2 * LANES), lambda b, p: (b, p)),
        out_shape=jax.ShapeDtypeStruct((n, HD), bf16),
        scratch_shapes=[pltpu.VMEM((seq, LANES), f32)] * 4,
        compiler_params=_cparams(("arbitrary", "arbitrary")),
        name="attn_prompt",
    )(*([planes] * 9), bias_tab)


SAMPLE_HEADS_PER_STEP = 4


def _sample_kv_kernel(c_ref, q_ref, kn_ref, vn_ref, knc_ref, vnc_ref, bias_ref, b0_ref,
                      o_ref, m_ref, l_ref, acc_ref, *, length):
    k = c_ref[0, 0, 0]
    v = c_ref[0, 0, 1]
    hb = k.shape[0]
    q = q_ref[0, 0]
    kn = kn_ref[0, 0]
    vn = vn_ref[0, 0]
    rb = lambda a: a.astype(bf16).astype(f32)
    q8 = jnp.broadcast_to(q[:, None, :], (hb, SUBLANES, HEAD_DIM)).astype(bf16)
    s = jnp.einsum("hqe,hel->hql", q8, k.astype(bf16), preferred_element_type=f32)[:, 0:1, :]
    s = s + bias_ref[...]
    s0 = jnp.sum(rb(kn) * rb(q), axis=-1, keepdims=True)[:, :, None] + b0_ref[...]
    m = jnp.maximum(jnp.max(s, axis=2, keepdims=True), s0)
    p = jnp.exp(s - m)
    p0 = jnp.exp(s0 - m)
    l = jnp.sum(p, axis=2, keepdims=True) + p0
    p8 = jnp.broadcast_to(p, (hb, SUBLANES, length)).astype(bf16)
    acc = jnp.einsum("hql,hel->hqe", p8, v.astype(bf16), preferred_element_type=f32)[:, 0, :]
    acc = acc + rb(p0[:, 0, :]) * rb(vn)
    m_ref[0, 0] = m[:, 0, :]
    l_ref[0, 0] = l[:, 0, :]
    acc_ref[0, 0] = acc
    last = lax.broadcasted_iota(i32, k.shape, 2) == length - 1
    o_ref[0, 0, 0] = jnp.where(last, knc_ref[0], pltpu.roll(k, length - 1, 2))
    o_ref[0, 0, 1] = jnp.where(last, vnc_ref[0], pltpu.roll(v, length - 1, 2))


def _sample_kv_call(cache_t, q, kn, vn, bias, b0, nb):
    length = cache_t.shape[-1]
    hb = SAMPLE_HEADS_PER_STEP
    nh = N_HEADS // hb
    row = pl.BlockSpec((1, 1, hb, HEAD_DIM), lambda b, h: (b, h, 0, 0))
    col = pl.BlockSpec((1, hb, HEAD_DIM, 1), lambda b, h: (b, h, 0, 0))
    stat = pl.BlockSpec((1, 1, hb, 1), lambda b, h: (b, h, 0, 0))
    cspec = pl.BlockSpec((1, 1, 2, hb, HEAD_DIM, length), lambda b, h: (0, b, 0, h, 0, 0))
    grp = lambda a: a.reshape(nb, nh, hb, HEAD_DIM)
    new_t, m, l, acc = pl.pallas_call(
        functools.partial(_sample_kv_kernel, length=length),
        grid=(nb, nh),
        in_specs=[cspec, row, row, row, col, col,
                  pl.BlockSpec((hb, 1, length), lambda b, h: (h, 0, 0)),
                  pl.BlockSpec((hb, 1, 1), lambda b, h: (h, 0, 0))],
        out_specs=[cspec, stat, stat, row],
        out_shape=[jax.ShapeDtypeStruct(cache_t.shape, f32),
                   jax.ShapeDtypeStruct((nb, nh, hb, 1), f32),
                   jax.ShapeDtypeStruct((nb, nh, hb, 1), f32),
                   jax.ShapeDtypeStruct((nb, nh, hb, HEAD_DIM), f32)],
        compiler_params=_cparams(("arbitrary", "arbitrary")),
        name="sample_kv_group",
    )(cache_t, grp(q), grp(kn), grp(vn), kn[..., None], vn[..., None], bias, b0)
    return new_t, m.reshape(nb, N_HEADS, 1), l.reshape(nb, N_HEADS, 1), acc.reshape(nb, N_HEADS, HEAD_DIM)


def _sample_merge_kernel(m_ref, l_ref, acc_ref, o_ref):
    m = m_ref[...]
    mm = jnp.max(m, axis=0)
    w = jnp.exp(m - mm[None])
    num = jnp.sum(w * acc_ref[...], axis=0)
    den = jnp.sum(w * l_ref[...], axis=0)
    o_ref[...] = num / den


def _sample_merge(ms, ls, accs):
    return pl.pallas_call(
        _sample_merge_kernel,
        out_shape=jax.ShapeDtypeStruct(accs.shape[1:], f32),
        name="sample_merge",
    )(ms, ls, accs)


def _route(u, wr_ref, br_ref):
    lg = _dot(u.astype(bf16), wr_ref[...]) + br_ref[...]
    rows = u.shape[0]
    lane = lax.broadcasted_iota(i32, (rows, LANES), 1)
    lanef = lane.astype(f32)
    big = jnp.float32(1e9)
    isg = lane < N_EGROUPS
    lgg = jnp.where(isg, lg, NEG)
    mg = jnp.max(lgg, axis=1, keepdims=True)
    eg = jnp.where(isg, jnp.exp(lgg - mg), 0.0)
    pg = eg / jnp.sum(eg, axis=1, keepdims=True)
    gw = jnp.max(pg, axis=1, keepdims=True)
    gi = jnp.min(jnp.where(isg & (pg == gw), lanef, big), axis=1, keepdims=True)
    grp_of_lane = ((lane - N_EGROUPS) >> 2).astype(f32)
    sel = (lane >= N_EGROUPS) & (lane < N_EGROUPS + N_EXPERTS) & (grp_of_lane == gi)
    les = jnp.where(sel, lg, NEG)
    v1 = jnp.max(les, axis=1, keepdims=True)
    i1 = jnp.min(jnp.where(sel & (les == v1), lanef, big), axis=1, keepdims=True)
    sel2 = sel & (lanef != i1)
    les2 = jnp.where(sel2, lg, NEG)
    v2 = jnp.max(les2, axis=1, keepdims=True)
    i2 = jnp.min(jnp.where(sel2 & (les2 == v2), lanef, big), axis=1, keepdims=True)
    ex = jnp.exp(v2 - v1)
    den = 1.0 + ex
    pw1 = (1.0 / den) * gw
    pw2 = (ex / den) * gw
    e1 = i1 - N_EGROUPS
    e2 = i2 - N_EGROUPS
    first_low = e1 < e2
    lo = jnp.minimum(e1, e2) - 4.0 * gi
    hi = jnp.maximum(e1, e2) - 4.0 * gi
    pair = jnp.where(lo == 0.0, hi - 1.0, jnp.where(lo == 1.0, hi + 1.0, 5.0))
    cls = gi * 6.0 + pair
    g_lo = jnp.where(first_low, pw1, pw2)
    g_hi = jnp.where(first_low, pw2, pw1)
    zero = jnp.zeros((rows, LANES), f32)
    return jnp.where(lane == 0, cls, jnp.where(lane == 1, g_lo, jnp.where(lane == 2, g_hi, zero)))


def _write_slab(slab_ref, u, meta):
    rows = u.shape[0]
    half = D_MODEL // 2
    for c in range(4):
        w = _pack_words(u[:, c * LANES:(c + 1) * LANES], u[:, half + c * LANES:half + (c + 1) * LANES])
        slab_ref[pl.ds(c, rows, stride=SUBLANES), :] = w
    slab_ref[pl.ds(4, rows, stride=SUBLANES), :] = lax.bitcast_convert_type(meta, u32)
    z = jnp.zeros((rows, LANES), u32)
    for c in range(5, SUBLANES):
        slab_ref[pl.ds(c, rows, stride=SUBLANES), :] = z


def _ffn_prep(h, gn_ref, wr_ref, br_ref, slab_ref, metat_ref):
    u = _rms(h, gn_ref[...])
    meta = _route(u, wr_ref, br_ref)
    _write_slab(slab_ref, u, meta)
    metat_ref[...] = meta.T[:SUBLANES, :]


def _from_slab_f32(ref, rows):
    return jnp.concatenate([ref[pl.ds(c, rows, stride=SUBLANES), :] for c in range(SUBLANES)], axis=1)


def _wo_kernel(o_ref, x_ref, wo_ref, gn_ref, wr_ref, br_ref, h_ref, slab_ref, metat_ref):
    h = x_ref[...] + _dot(o_ref[...], wo_ref[...])
    h_ref[...] = h
    _ffn_prep(h, gn_ref, wr_ref, br_ref, slab_ref, metat_ref)


def _router_specs():
    cmap = lambda i: (0, 0)
    return [pl.BlockSpec((1, D_MODEL), cmap),
            pl.BlockSpec((D_MODEL, LANES), cmap),
            pl.BlockSpec((1, LANES), cmap)]


def _prep_out(n, tm):
    specs = [pl.BlockSpec((tm, D_MODEL), lambda i: (i, 0)),
             pl.BlockSpec((tm * SUBLANES, LANES), lambda i: (i, 0)),
             pl.BlockSpec((SUBLANES, tm), lambda i: (0, i))]
    shapes = [jax.ShapeDtypeStruct((n, D_MODEL), f32),
              jax.ShapeDtypeStruct((n * SUBLANES, LANES), u32),
              jax.ShapeDtypeStruct((SUBLANES, n), f32)]
    return specs, shapes


def _wo_call(o_bf, x2, wo_bf, router, tm):
    n = x2.shape[0]
    ospecs, oshapes = _prep_out(n, tm)
    return pl.pallas_call(
        _wo_kernel,
        grid=(n // tm,),
        in_specs=[pl.BlockSpec((tm, HD), lambda i: (i, 0)),
                  pl.BlockSpec((tm, D_MODEL), lambda i: (i, 0)),
                  pl.BlockSpec((HD, D_MODEL), lambda i: (0, 0))] + _router_specs(),
        out_specs=ospecs,
        out_shape=oshapes,
        compiler_params=_cparams(("arbitrary",)),
        name="wo_residual",
    )(o_bf, x2, wo_bf, *router)


def _rank_kernel(metat_ref, rank_ref, cnt_ref, carry_ref, *, tr):
    @pl.when(pl.program_id(0) == 0)
    def _():
        carry_ref[...] = jnp.zeros_like(carry_ref)

    cls = metat_ref[0:1, :]
    sub = lax.broadcasted_iota(i32, (32, tr), 0).astype(f32)
    oh = jnp.where(sub == cls, 1.0, 0.0)
    rr = lax.broadcasted_iota(i32, (tr, tr), 0)
    cc = lax.broadcasted_iota(i32, (tr, tr), 1)
    upper = jnp.where(rr < cc, 1.0, 0.0).astype(bf16)
    pre = _dot(oh.astype(bf16), upper)
    carry = carry_ref[:, 0:1]
    rank_ref[...] = jnp.sum(oh * (pre + carry), axis=0, keepdims=True).astype(i32)
    newc = carry + jnp.sum(oh, axis=1, keepdims=True)
    carry_ref[...] = jnp.broadcast_to(newc, carry_ref.shape)
    cnt_ref[...] = carry_ref[...]


def _rank_call(metat, tr):
    n = metat.shape[1]
    return pl.pallas_call(
        functools.partial(_rank_kernel, tr=tr),
        grid=(n // tr,),
        in_specs=[pl.BlockSpec((SUBLANES, tr), lambda i: (0, i))],
        out_specs=[pl.BlockSpec((1, tr), lambda i: (0, i)),
                   pl.BlockSpec((32, LANES), lambda i: (0, 0))],
        out_shape=[jax.ShapeDtypeStruct((1, n), i32), jax.ShapeDtypeStruct((32, LANES), f32)],
        scratch_shapes=[pltpu.VMEM((32, LANES), f32)],
        compiler_params=_cparams(("arbitrary",)),
        name="class_rank",
    )(metat)


ROWS_PER_GATHER_STEP = 256


GATHER_UNROLL = 8


def _gather_kernel(idx_ref, src_ref, o_ref, sem, *, rows):
    base = pl.program_id(0) * rows

    def issue(kk, c):
        for u in range(GATHER_UNROLL):
            k = kk * GATHER_UNROLL + u
            t = idx_ref[base + k]
            pltpu.make_async_copy(
                src_ref.at[pl.ds(pl.multiple_of(t * SUBLANES, SUBLANES), SUBLANES)],
                o_ref.at[pl.ds(pl.multiple_of(k * SUBLANES, SUBLANES), SUBLANES)], sem).start()
        return c

    lax.fori_loop(0, rows // GATHER_UNROLL, issue, 0)
    pltpu.make_async_copy(src_ref.at[pl.ds(0, rows * SUBLANES)], o_ref, sem).wait()


def _gather_slabs(idx, src):
    n_out = idx.shape[0]
    rows = math.gcd(ROWS_PER_GATHER_STEP, n_out)
    assert rows % GATHER_UNROLL == 0 and src.shape[0] >= rows * SUBLANES
    return pl.pallas_call(
        functools.partial(_gather_kernel, rows=rows),
        grid_spec=pltpu.PrefetchScalarGridSpec(
            num_scalar_prefetch=1, grid=(n_out // rows,),
            in_specs=[pl.BlockSpec(memory_space=pl.ANY)],
            out_specs=pl.BlockSpec((rows * SUBLANES, LANES), lambda i, idx: (i, 0)),
            scratch_shapes=[pltpu.SemaphoreType.DMA(())]),
        out_shape=jax.ShapeDtypeStruct((n_out * SUBLANES, LANES), src.dtype),
        compiler_params=_cparams(("arbitrary",)),
        name="gather_slabs",
    )(idx, src)


def _expert_kernel(e1_ref, e2_ref, valid_ref, xs_ref, wg1, wu1, wd1, wg2, wu2, wd2, y_ref, *, te):
    del e1_ref, e2_ref
    valid = valid_ref[pl.program_id(0)]

    @pl.when(valid == 0)
    def _():
        y_ref[...] = jnp.zeros_like(y_ref)

    @pl.when(valid > 0)
    def _():
        los, his = [], []
        for c in range(4):
            lo, hi = _unpack_words(xs_ref[pl.ds(c, te, stride=SUBLANES), :])
            los.append(lo)
            his.append(hi)
        x = jnp.concatenate(los + his, axis=1).astype(bf16)
        meta = lax.bitcast_convert_type(xs_ref[pl.ds(4, te, stride=SUBLANES), :], f32)
        y = None
        for wg, wu, wd, lane in ((wg1, wu1, wd1, 1), (wg2, wu2, wd2, 2)):
            a = _dot(x, wg[0])
            c = _dot(x, wu[0])
            hh = (a * (1.0 / (1.0 + jnp.exp(-a)))) * c * meta[:, lane:lane + 1]
            part = _dot(hh.astype(bf16), wd[0])
            y = part if y is None else y + part
        for c in range(SUBLANES):
            y_ref[pl.ds(c, te, stride=SUBLANES), :] = y[:, c * LANES:(c + 1) * LANES]


def _expert_call(te1, te2, tvalid, xs, wg, wu, wd, te):
    ntiles = te1.shape[0]

    def wspec(shape, which):
        if which == 1:
            return pl.BlockSpec(shape, lambda i, e1, e2, v: (e1[i], 0, 0))
        return pl.BlockSpec(shape, lambda i, e1, e2, v: (e2[i], 0, 0))

    gs = (1, D_MODEL, D_EXPERT)
    ds_ = (1, D_EXPERT, D_MODEL)
    return pl.pallas_call(
        functools.partial(_expert_kernel, te=te),
        grid_spec=pltpu.PrefetchScalarGridSpec(
            num_scalar_prefetch=3, grid=(ntiles,),
            in_specs=[pl.BlockSpec((te * SUBLANES, LANES), lambda i, e1, e2, v: (i, 0)),
                      wspec(gs, 1), wspec(gs, 1), wspec(ds_, 1),
                      wspec(gs, 2), wspec(gs, 2), wspec(ds_, 2)],
            out_specs=pl.BlockSpec((te * SUBLANES, LANES), lambda i, e1, e2, v: (i, 0))),
        out_shape=jax.ShapeDtypeStruct((ntiles * te * SUBLANES, LANES), f32),
        compiler_params=_cparams(("arbitrary",)),
        name="expert_pair_ffn",
    )(te1, te2, tvalid, xs, wg, wu, wd, wg, wu, wd)


def _moe(slab, metat, wg, wu, wd, te, tr):
    n = metat.shape[1]
    rank, cnt = _rank_call(metat, tr)
    counts = cnt[:N_CLASSES, 0].astype(i32)
    padded = ((counts + te - 1) // te) * te
    ends = jnp.cumsum(padded)
    off = ends - padded
    cls = metat[0].astype(i32)
    class_ids = jnp.arange(N_CLASSES, dtype=i32)
    pos = rank[0] + jnp.sum(jnp.where(cls[:, None] == class_ids[None, :], off[None, :], 0), axis=1)
    ntiles = -(-n // te) + N_CLASSES
    tstart = jnp.arange(ntiles, dtype=i32) * te
    tcls = jnp.minimum(jnp.sum((tstart[:, None] >= ends[None, :]).astype(i32), axis=1), N_CLASSES - 1)
    tvalid = (tstart < ends[-1]).astype(i32)
    te1 = jnp.asarray(_CLASS_E1)[tcls]
    te2 = jnp.asarray(_CLASS_E2)[tcls]
    src = jnp.zeros((ntiles * te,), i32).at[pos].set(jnp.arange(n, dtype=i32), unique_indices=True)
    xs = _gather_slabs(src, slab)
    ys = _expert_call(te1, te2, tvalid, xs, wg, wu, wd, te)
    return _gather_slabs(pos, ys)


def _pool_mix_tail(hf, up, ext, tpos0, pw_ref, pb_ref, ps_ref):
    rows = up.shape[0]
    tpos = tpos0 + lax.broadcasted_iota(i32, (rows, 1), 0)
    parts = []
    for gi, win in enumerate(POOL_WINDOWS):
        sl = slice(gi * POOL_CH, (gi + 1) * POOL_CH)
        s = ext[:, sl]
        for kk in range(gi + 1):
            s = s + pltpu.roll(s, 1 << kk, 0)
        cnt = jnp.minimum(win, tpos + 1).astype(f32)
        z = s[16:, :] / cnt - up[:, sl]
        parts.append(_dot(z.astype(bf16), pw_ref[gi]))
    y = (jnp.concatenate(parts, axis=1) + pb_ref[...]) * ps_ref[...]
    return hf + y


def _pool_kernel(h_ref, y_ref, hh_ref, yh_ref, pn_ref, pw_ref, pb_ref, ps_ref,
                 gn_ref, wr_ref, br_ref, h2_ref, slab_ref, metat_ref, st_ref, *, tm, tps):
    i = pl.program_id(0)
    r = i % tps
    hf = h_ref[...] + _from_slab_f32(y_ref, tm)
    up = _rms(hf, pn_ref[...])
    hh = hh_ref[...] + _from_slab_f32(yh_ref, 16)
    uph = jnp.where(r > 0, _rms(hh, pn_ref[...]), 0.0)
    ext = jnp.concatenate([uph, up], axis=0)
    h2 = _pool_mix_tail(hf, up, ext, r * tm, pw_ref, pb_ref, ps_ref)
    h2_ref[...] = h2
    st_ref[...] = up[tm - 16:, :]
    _ffn_prep(h2, gn_ref, wr_ref, br_ref, slab_ref, metat_ref)


def _pool_call(h, ytt, pn, pw_bf, pb, psc, router, batch, seq, tm):
    n = h.shape[0]
    tps = seq // tm
    ospecs, oshapes = _prep_out(n, tm)
    ospecs.append(pl.BlockSpec((16, D_MODEL), lambda i: (i // tps, 0)))
    oshapes.append(jax.ShapeDtypeStruct((batch * 16, D_MODEL), f32))
    halo = lambda i: (jnp.maximum(i * (tm // 16) - 1, 0), 0)
    cm = lambda i: (0, 0)
    return pl.pallas_call(
        functools.partial(_pool_kernel, tm=tm, tps=tps),
        grid=(n // tm,),
        in_specs=[pl.BlockSpec((tm, D_MODEL), lambda i: (i, 0)),
                  pl.BlockSpec((tm * SUBLANES, LANES), lambda i: (i, 0)),
                  pl.BlockSpec((16, D_MODEL), halo),
                  pl.BlockSpec((16 * SUBLANES, LANES), halo),
                  pl.BlockSpec((1, D_MODEL), cm),
                  pl.BlockSpec((4, POOL_CH, POOL_CH), lambda i: (0, 0, 0)),
                  pl.BlockSpec((1, D_MODEL), cm),
                  pl.BlockSpec((1, D_MODEL), cm)] + _router_specs(),
        out_specs=ospecs,
        out_shape=oshapes,
        compiler_params=_cparams(("arbitrary",)),
        name="pool_mixer",
    )(h, ytt, h, ytt, pn, pw_bf, pb, psc, *router)


def _pool_sample_kernel(h_ref, y_ref, st_ref, pn_ref, pw_ref, pb_ref, ps_ref,
                        gn_ref, wr_ref, br_ref, h2_ref, slab_ref, metat_ref, nst_ref, *, nb):
    rows = h_ref.shape[0]
    hf = h_ref[...] + _from_slab_f32(y_ref, rows)
    up = _rms(hf, pn_ref[...])
    st = st_ref[...]
    ub = up[:nb, :]
    parts = []
    for gi, win in enumerate(POOL_WINDOWS):
        sl = slice(gi * POOL_CH, (gi + 1) * POOL_CH)
        tot = ub[:, sl] + jnp.sum(st[:, POOL_STATE - (win - 1):, sl], axis=1)
        z = tot / float(win) - ub[:, sl]
        parts.append(_dot(z.astype(bf16), pw_ref[gi]))
    y = (jnp.concatenate(parts, axis=1) + pb_ref[...]) * ps_ref[...]
    if rows > nb:
        y = jnp.concatenate([y, jnp.zeros((rows - nb, D_MODEL), f32)], axis=0)
    h2 = hf + y
    h2_ref[...] = h2
    nst_ref[:, :POOL_STATE - 1, :] = st[:, 1:, :]
    nst_ref[:, POOL_STATE - 1, :] = ub
    _ffn_prep(h2, gn_ref, wr_ref, br_ref, slab_ref, metat_ref)


def _pool_sample_call(h, ytt, state, pn, pw_bf, pb, psc, router, nb):
    rows = h.shape[0]
    ospecs, oshapes = _prep_out(rows, rows)
    ospecs.append(pl.BlockSpec((nb, POOL_STATE, D_MODEL), lambda i: (0, 0, 0)))
    oshapes.append(jax.ShapeDtypeStruct((nb, POOL_STATE, D_MODEL), f32))
    cm = lambda i: (0, 0)
    return pl.pallas_call(
        functools.partial(_pool_sample_kernel, nb=nb),
        grid=(1,),
        in_specs=[pl.BlockSpec((rows, D_MODEL), cm),
                  pl.BlockSpec((rows * SUBLANES, LANES), cm),
                  pl.BlockSpec((nb, POOL_STATE, D_MODEL), lambda i: (0, 0, 0)),
                  pl.BlockSpec((1, D_MODEL), cm),
                  pl.BlockSpec((4, POOL_CH, POOL_CH), lambda i: (0, 0, 0)),
                  pl.BlockSpec((1, D_MODEL), cm),
                  pl.BlockSpec((1, D_MODEL), cm)] + _router_specs(),
        out_specs=ospecs,
        out_shape=oshapes,
        compiler_params=_cparams(("arbitrary",)),
        name="pool_mixer_sample",
    )(h, ytt, state, pn, pw_bf, pb, psc, *router)


def _final_kernel(h_ref, y_ref, o_ref, *, tm):
    o_ref[...] = h_ref[...] + _from_slab_f32(y_ref, tm)


def _final_call(h, ytt, tm):
    n = h.shape[0]
    return pl.pallas_call(
        functools.partial(_final_kernel, tm=tm),
        grid=(n // tm,),
        in_specs=[pl.BlockSpec((tm, D_MODEL), lambda i: (i, 0)),
                  pl.BlockSpec((tm * SUBLANES, LANES), lambda i: (i, 0))],
        out_specs=pl.BlockSpec((tm, D_MODEL), lambda i: (i, 0)),
        out_shape=jax.ShapeDtypeStruct((n, D_MODEL), f32),
        compiler_params=_cparams(("arbitrary",)),
        name="final_residual",
    )(h, ytt)


def _t5_bucket(dist):
    max_exact = N_BUCKETS // 2
    df = jnp.maximum(dist, max_exact).astype(f32)
    large = max_exact + (jnp.log(df / max_exact) / math.log(MAX_DISTANCE / max_exact)
                         * (N_BUCKETS - max_exact)).astype(i32)
    large = jnp.minimum(large, N_BUCKETS - 1)
    return jnp.where(dist < max_exact, dist, large)


def _prompt_bias_tables(rel_bias):
    qi = jnp.arange(BAND)[:, None]
    kj = jnp.arange(2 * BAND)[None, :]
    step = BAND + qi - kj
    tabs = []
    for g, (win, dil) in enumerate(zip(WINDOWS, DILATIONS)):
        n_step = win // dil
        tab = rel_bias[:, g * N_HEADS:(g + 1) * N_HEADS].astype(f32)
        svals = jnp.maximum(2 * BAND - 1 - jnp.arange(3 * BAND), 0)
        bvec = tab[_t5_bucket(svals * dil)].T
        bias = jnp.stack([bvec[:, BAND - 1 - q:3 * BAND - 1 - q] for q in range(BAND)], axis=1)
        band = (step >= 0) & (step <= n_step)
        reg = jnp.where(band[None], bias, NEG)
        first = jnp.where((band & (kj >= BAND))[None], bias, NEG)
        tabs.append(jnp.stack([reg, first]))
    return jnp.stack(tabs)


def _sample_bias_tables(rel_bias, lengths):
    bs, b0 = [], []
    for g, (win, dil) in enumerate(zip(WINDOWS, DILATIONS)):
        length = lengths[g]
        n_step = win // dil
        tab = rel_bias[:, g * N_HEADS:(g + 1) * N_HEADS].astype(f32)
        dist = length - jnp.arange(length)
        hit = (dist % dil == 0) & (dist <= n_step * dil)
        b = jnp.where(hit[:, None], tab[_t5_bucket(dist)], NEG)
        bs.append(b.T[:, None, :])
        b0.append(tab[_t5_bucket(jnp.zeros((1,), i32))].T[:, :, None])
    return bs, b0


def _router_weights(ffn_norm_i, wgr, bgr, wer, ber):
    w = jnp.zeros((D_MODEL, LANES), f32)
    w = w.at[:, :N_EGROUPS].set(wgr).at[:, N_EGROUPS:N_EGROUPS + N_EXPERTS].set(wer)
    b = jnp.zeros((1, LANES), f32)
    b = b.at[0, :N_EGROUPS].set(bgr).at[0, N_EGROUPS:N_EGROUPS + N_EXPERTS].set(ber)
    return ffn_norm_i.reshape(1, D_MODEL), w.astype(bf16), b


def _norm_mats():
    head_of_col = np.arange(HD) // HEAD_DIM
    s = (head_of_col[:, None] == np.arange(LANES)[None, :]).astype(np.float32)
    e = s.T
    return jnp.asarray(s, bf16), jnp.asarray(np.concatenate([e, e], axis=0), bf16)


def kernel(x_prompt, x_sample, cache_kv_w128, cache_kv_w512, cache_kv_w2048, state_pool, w_qkv, q_norm, k_norm, w_o, rel_bias, attn_norm, pool_norm, pool_w, pool_b, pool_scale, ffn_norm, router_group_w, router_group_b, router_expert_w, router_expert_b, w_gate, w_up, w_down):
    batch, seq, _ = x_prompt.shape
    nb = x_sample.shape[0]
    assert x_sample.shape[1] == 1
    n = batch * seq
    srows = LANES
    caches6 = (cache_kv_w128, cache_kv_w512, cache_kv_w2048)

    smat, e2mat = _norm_mats()
    wqkv_bf = w_qkv[0].astype(bf16)
    wo_bf = w_o[0].astype(bf16)
    gq = jnp.tile(q_norm[0], (1, N_HEADS)).reshape(N_DGROUPS, 1, HD)
    gk = jnp.tile(k_norm[0], (1, N_HEADS)).reshape(N_DGROUPS, 1, HD)
    an = attn_norm[0].reshape(1, D_MODEL)
    routers = [_router_weights(ffn_norm[i], router_group_w[i], router_group_b[i],
                               router_expert_w[i], router_expert_b[i]) for i in range(2)]
    wg_bf = [w_gate[i].astype(bf16) for i in range(2)]
    wu_bf = [w_up[i].astype(bf16) for i in range(2)]
    wd_bf = [w_down[i].astype(bf16) for i in range(2)]
    pw_bf = pool_w[0].astype(bf16)
    pn = pool_norm[0].reshape(1, D_MODEL)
    pb = pool_b[0].reshape(1, D_MODEL)
    psc = pool_scale[0].reshape(1, D_MODEL)

    x2 = x_prompt.reshape(n, D_MODEL)
    planes, kv2, kv1, kv0 = _qkv_prompt(x2, an, wqkv_bf, gq, gk, smat, e2mat, batch, seq)
    o_bf = _attn_prompt(planes, _prompt_bias_tables(rel_bias), batch, seq)
    h0, slab0, metat0 = _wo_call(o_bf, x2, wo_bf, routers[0], tm=512)
    ytt0 = _moe(slab0, metat0, wg_bf[0], wu_bf[0], wd_bf[0], te=256, tr=512)
    h1, slab1, metat1, pst = _pool_call(h0, ytt0, pn, pw_bf, pb, psc, routers[1], batch, seq, tm=512)
    ytt1 = _moe(slab1, metat1, wg_bf[1], wu_bf[1], wd_bf[1], te=256, tr=512)
    y_prompt = _final_call(h1, ytt1, tm=512).reshape(batch, seq, D_MODEL)

    new_kv_p = [jnp.transpose(t, (0, 1, 5, 2, 3, 4)) for t in (kv0, kv1, kv2)]
    new_pool_p = pst.reshape(batch, 16, D_MODEL)[:, 16 - POOL_STATE:][None]

    xs = jnp.zeros((srows, D_MODEL), f32).at[:nb].set(x_sample[:, 0, :])
    qs, ks, vs = _qkv_sample(xs, an, wqkv_bf, gq, gk, smat, e2mat)
    to4 = lambda a: a[:, :nb].reshape(N_DGROUPS, nb, N_HEADS, HEAD_DIM)
    qs4, ks4, vs4 = to4(qs), to4(ks), to4(vs)
    bias_s, bias_s0 = _sample_bias_tables(rel_bias, [c.shape[2] for c in caches6])
    new_kv_s, ms, ls, accs = [], [], [], []
    for g in range(N_DGROUPS):
        cache_t = jnp.transpose(caches6[g], (0, 1, 3, 4, 5, 2))
        new_t, m_g, l_g, acc_g = _sample_kv_call(cache_t, qs4[g], ks4[g], vs4[g], bias_s[g], bias_s0[g], nb)
        new_kv_s.append(jnp.transpose(new_t, (0, 1, 5, 2, 3, 4)))
        ms.append(m_g)
        ls.append(l_g)
        accs.append(acc_g)
    os_ = _sample_merge(jnp.stack(ms), jnp.stack(ls), jnp.stack(accs))
    os_pad = jnp.zeros((srows, HD), bf16).at[:nb].set(os_.reshape(nb, HD).astype(bf16))
    hs0, sslab0, smetat0 = _wo_call(os_pad, xs, wo_bf, routers[0], tm=srows)
    ystt0 = _moe(sslab0, smetat0, wg_bf[0], wu_bf[0], wd_bf[0], te=64, tr=srows)
    hs1, sslab1, smetat1, new_pool_s = _pool_sample_call(
        hs0, ystt0, state_pool[0], pn, pw_bf, pb, psc, routers[1], nb)
    ystt1 = _moe(sslab1, smetat1, wg_bf[1], wu_bf[1], wd_bf[1], te=64, tr=srows)
    y_sample = _final_call(hs1, ystt1, tm=srows)[:nb].reshape(nb, 1, D_MODEL)

    return (y_prompt, y_sample, new_kv_p[0], new_kv_p[1], new_kv_p[2], new_pool_p,
            new_kv_s[0], new_kv_s[1], new_kv_s[2], new_pool_s[None])
```

```python
import functools
import math

import numpy as np
import jax
import jax.numpy as jnp
from jax import lax
from jax.experimental import pallas as pl
from jax.experimental.pallas import tpu as pltpu

f32 = jnp.float32
bf16 = jnp.bfloat16
u32 = jnp.uint32
i32 = jnp.int32

D_MODEL = 1024
N_HEADS = 16
HEAD_DIM = 64
HD = N_HEADS * HEAD_DIM
N_DGROUPS = 3
WINDOWS = (128, 512, 2048)
DILATIONS = (1, 4, 16)
BAND = 128
N_BUCKETS = 32
MAX_DISTANCE = 2048
POOL_WINDOWS = (2, 4, 8, 16)
POOL_CH = D_MODEL // 4
POOL_STATE = max(POOL_WINDOWS) - 1
N_EGROUPS = 4
EXPERTS_PER_GROUP = 4
N_EXPERTS = 16
D_EXPERT = D_MODEL // 2
RMS_EPS = 1e-6
NEG = -1e30

LANES = 128
SUBLANES = 8
HI_MASK = 0xFFFF0000
N_CLASSES = N_EGROUPS * 6
PLANES_PER_KIND = HD // (2 * LANES)
VMEM_LIMIT = 56 * 1024 * 1024

_PAIRS = [(a, b) for a in range(4) for b in range(a + 1, 4)]
_CLASS_E1 = np.array([g * 4 + _PAIRS[p][0] for g in range(4) for p in range(6)], np.int32)
_CLASS_E2 = np.array([g * 4 + _PAIRS[p][1] for g in range(4) for p in range(6)], np.int32)


def _cparams(sem=None, vmem=VMEM_LIMIT):
    kw = dict(vmem_limit_bytes=vmem)
    if sem is not None:
        kw["dimension_semantics"] = sem
    return pltpu.CompilerParams(**kw)


def _dot(a, b):
    return jnp.dot(a, b, preferred_element_type=f32)


def _rms(x, g):
    return x * lax.rsqrt(jnp.mean(x * x, axis=-1, keepdims=True) + RMS_EPS) * g


def _pack_words(lo, hi):
    lb = lax.bitcast_convert_type(lo.astype(bf16).astype(f32), u32) >> 16
    hb = lax.bitcast_convert_type(hi.astype(bf16).astype(f32), u32) & jnp.uint32(HI_MASK)
    return lb | hb


def _unpack_words(w):
    lo = lax.bitcast_convert_type(w << 16, f32)
    hi = lax.bitcast_convert_type(w & jnp.uint32(HI_MASK), f32)
    return lo, hi


def _head_norm(z, gain, s_ref, e2_ref):
    zz = z * z
    zz_hi = zz.astype(bf16)
    zz_lo = (zz - zz_hi.astype(f32)).astype(bf16)
    ss = _dot(zz_hi, s_ref[...]) + _dot(zz_lo, s_ref[...])
    r = lax.rsqrt(ss * (1.0 / HEAD_DIM) + RMS_EPS)
    r_hi = r.astype(bf16)
    r_lo = (r - r_hi.astype(f32)).astype(bf16)
    rb = _dot(jnp.concatenate([r_hi, r_lo], axis=1), e2_ref[...])
    return z * rb * gain


def _qkv_kernel(x_ref, gn_ref, w_ref, gq_ref, gk_ref, s_ref, e2_ref,
                planes_ref, kv2_ref, kv1_ref, kv0_ref, *, tm, tps, t512):
    j = pl.program_id(0)
    r = pl.program_id(1) % tps
    u = _rms(x_ref[...], gn_ref[...]).astype(bf16)
    q = _head_norm(_dot(u, w_ref[:, :HD]), gq_ref[0], s_ref, e2_ref) * (HEAD_DIM ** -0.5)
    k = _head_norm(_dot(u, w_ref[:, HD:2 * HD]), gk_ref[0], s_ref, e2_ref)
    v = _dot(u, w_ref[:, 2 * HD:])
    for kind, arr in enumerate((q, k, v)):
        for p in range(PLANES_PER_KIND):
            c0 = 2 * LANES * p
            planes_ref[kind * PLANES_PER_KIND + p] = _pack_words(
                arr[:, c0:c0 + LANES], arr[:, c0 + LANES:c0 + 2 * LANES])

    def store_t(ref, kk, vv):
        ref[0, 0, 0] = kk.T.reshape(N_HEADS, HEAD_DIM, kk.shape[0])
        ref[0, 0, 1] = vv.T.reshape(N_HEADS, HEAD_DIM, vv.shape[0])

    @pl.when(j == 2)
    def _():
        store_t(kv2_ref, k, v)

    @pl.when((j == 1) & (r >= tps - t512))
    def _():
        store_t(kv1_ref, k, v)

    @pl.when((j == 0) & (r == tps - 1))
    def _():
        store_t(kv0_ref, k[tm - BAND:, :], v[tm - BAND:, :])


def _qkv_prompt(x2, gn, w_bf, gq, gk, smat, e2mat, batch, seq, tm=256):
    n = x2.shape[0]
    nt = n // tm
    tps = seq // tm
    t512 = WINDOWS[1] // tm
    assert seq % tm == 0 and WINDOWS[1] % tm == 0 and tm >= BAND and seq == WINDOWS[2]

    def kv2_map(j, i):
        on = j == 2
        return (0, jnp.where(on, i // tps, 0), 0, 0, 0, jnp.where(on, i % tps, 0))

    def kv1_map(j, i):
        inside = jnp.maximum(i % tps - (tps - t512), 0)
        b = jnp.where(j == 1, i // tps, jnp.where(j < 1, 0, batch - 1))
        t = jnp.where(j == 1, inside, jnp.where(j < 1, 0, t512 - 1))
        return (0, b, 0, 0, 0, t)

    def kv0_map(j, i):
        return (0, jnp.where(j == 0, i // tps, batch - 1), 0, 0, 0, 0)

    nplanes = 3 * PLANES_PER_KIND
    kvt = lambda keep: (1, batch, 2, N_HEADS, HEAD_DIM, keep)
    kvblock = lambda cols: (1, 1, 2, N_HEADS, HEAD_DIM, cols)
    return pl.pallas_call(
        functools.partial(_qkv_kernel, tm=tm, tps=tps, t512=t512),
        grid=(N_DGROUPS, nt),
        in_specs=[
            pl.BlockSpec((tm, D_MODEL), lambda j, i: (i, 0)),
            pl.BlockSpec((1, D_MODEL), lambda j, i: (0, 0)),
            pl.BlockSpec((D_MODEL, 3 * HD), lambda j, i: (0, j)),
            pl.BlockSpec((1, 1, HD), lambda j, i: (j, 0, 0)),
            pl.BlockSpec((1, 1, HD), lambda j, i: (j, 0, 0)),
            pl.BlockSpec((HD, LANES), lambda j, i: (0, 0)),
            pl.BlockSpec((2 * LANES, HD), lambda j, i: (0, 0)),
        ],
        out_specs=[
            pl.BlockSpec((nplanes, tm, LANES), lambda j, i: (j, i, 0)),
            pl.BlockSpec(kvblock(tm), kv2_map),
            pl.BlockSpec(kvblock(tm), kv1_map),
            pl.BlockSpec(kvblock(BAND), kv0_map),
        ],
        out_shape=[
            jax.ShapeDtypeStruct((N_DGROUPS * nplanes, n, LANES), u32),
            jax.ShapeDtypeStruct(kvt(WINDOWS[2]), f32),
            jax.ShapeDtypeStruct(kvt(WINDOWS[1]), f32),
            jax.ShapeDtypeStruct(kvt(WINDOWS[0]), f32),
        ],
        compiler_params=_cparams(("arbitrary", "arbitrary")),
        name="qkv_prompt",
    )(x2, gn, w_bf, gq, gk, smat, e2mat)


def _qkv_sample_kernel(x_ref, gn_ref, w_ref, gq_ref, gk_ref, s_ref, e2_ref, q_ref, k_ref, v_ref):
    u = _rms(x_ref[...], gn_ref[...]).astype(bf16)
    z = _dot(u, w_ref[...])
    q_ref[0] = _head_norm(z[:, :HD], gq_ref[0], s_ref, e2_ref) * (HEAD_DIM ** -0.5)
    k_ref[0] = _head_norm(z[:, HD:2 * HD], gk_ref[0], s_ref, e2_ref)
    v_ref[0] = z[:, 2 * HD:]


def _qkv_sample(xs, gn, w_bf, gq, gk, smat, e2mat):
    rows = xs.shape[0]
    out = jax.ShapeDtypeStruct((N_DGROUPS, rows, HD), f32)
    ospec = pl.BlockSpec((1, rows, HD), lambda j: (j, 0, 0))
    return pl.pallas_call(
        _qkv_sample_kernel,
        grid=(N_DGROUPS,),
        in_specs=[
            pl.BlockSpec((rows, D_MODEL), lambda j: (0, 0)),
            pl.BlockSpec((1, D_MODEL), lambda j: (0, 0)),
            pl.BlockSpec((D_MODEL, 3 * HD), lambda j: (0, j)),
            pl.BlockSpec((1, 1, HD), lambda j: (j, 0, 0)),
            pl.BlockSpec((1, 1, HD), lambda j: (j, 0, 0)),
            pl.BlockSpec((HD, LANES), lambda j: (0, 0)),
            pl.BlockSpec((2 * LANES, HD), lambda j: (0, 0)),
        ],
        out_specs=[ospec, ospec, ospec],
        out_shape=[out, out, out],
        compiler_params=_cparams(("arbitrary",)),
        name="qkv_sample",
    )(xs, gn, w_bf, gq, gk, smat, e2mat)


ATTN_BLOCKS_PER_STEP = 8


def _attn_plane(qw, kcw, kpw, vcw, vpw, bias_fn, prev):
    q2 = _unpack_words(qw)
    kc2 = _unpack_words(kcw)
    vc2 = _unpack_words(vcw)
    if kpw is not None:
        kp2 = _unpack_words(kpw)
        vp2 = _unpack_words(vpw)
    lane = lax.broadcasted_iota(i32, (BAND, LANES), 1)
    left = lane < HEAD_DIM
    res = []
    for pr in range(2):
        q = q2[pr].astype(bf16)
        if kpw is not None:
            kk = jnp.concatenate([kp2[pr], kc2[pr]], axis=0).astype(bf16)
            vv = jnp.concatenate([vp2[pr], vc2[pr]], axis=0).astype(bf16)
        else:
            kk = kc2[pr].astype(bf16)
            vv = vc2[pr].astype(bf16)
        zq = jnp.zeros_like(q)
        qs = jnp.concatenate([jnp.where(left, q, zq), jnp.where(left, zq, q)], axis=0)
        s = lax.dot_general(qs, kk, (((1,), (1,)), ((), ())), preferred_element_type=f32)
        s = s + bias_fn(pr)
        m = jnp.max(s, axis=1, keepdims=True)
        p = jnp.exp(s - m)
        l = jnp.sum(p, axis=1, keepdims=True)
        o2 = _dot(p.astype(bf16), vv)
        l_pair = jnp.where(left, l[:BAND], l[BAND:])
        m_pair = jnp.where(left, m[:BAND], m[BAND:])
        o = jnp.where(left, o2[:BAND], o2[BAND:]) / l_pair
        lse = m_pair + jnp.log(l_pair)
        if prev is not None:
            po, pl_ = prev[pr]
            mx = jnp.maximum(pl_, lse)
            ea = jnp.exp(pl_ - mx)
            eb = jnp.exp(lse - mx)
            tot = ea + eb
            o = po * (ea / tot) + o * (eb / tot)
            lse = mx + jnp.log(tot)
        res.append((o, lse))
    return res


def _attn_kernel(q0, k0, v0, q1, k1, v1, q2, k2, v2, bias_ref, o_ref,
                 acc_lo, acc_hi, lse_lo, lse_hi, *, seq):
    state = ((acc_lo, lse_lo), (acc_hi, lse_hi))

    def load_state(rows):
        return tuple((a[rows, :], l[rows, :]) for a, l in state)

    def store_state(rows, res):
        for (a, l), (o, lse) in zip(state, res):
            a[rows, :] = o
            l[rows, :] = lse

    def bias2(g, first, pr, cols=None):
        if cols is None:
            b = bias_ref[g, first, pl.ds(2 * pr, 2)]
        else:
            b = bias_ref[g, first, pl.ds(2 * pr, 2), :, cols]
        return b.reshape(2 * BAND, b.shape[-1])

    d2 = DILATIONS[2]
    assert seq == d2 * BAND

    def body2(i, c):
        work = []
        for u in range(ATTN_BLOCKS_PER_STEP):
            cur = pl.ds(i * ATTN_BLOCKS_PER_STEP + u, BAND, stride=d2)
            work.append((cur, (q2[0, cur, :], k2[0, cur, :], None, v2[0, cur, :], None)))
        outs = [_attn_plane(*args, lambda pr: bias2(2, 0, pr, pl.ds(BAND, BAND)), None)
                for _, args in work]
        for (cur, _), res in zip(work, outs):
            store_state(cur, res)
        return c

    lax.fori_loop(0, d2 // ATTN_BLOCKS_PER_STEP, body2, 0)

    d1 = DILATIONS[1]
    nb1 = seq // (d1 * BAND)
    assert (d1 * nb1) % ATTN_BLOCKS_PER_STEP == 0 and d2 % ATTN_BLOCKS_PER_STEP == 0

    def body1(i, c):
        work = []
        for u in range(ATTN_BLOCKS_PER_STEP):
            idx = i * ATTN_BLOCKS_PER_STEP + u
            r = idx // nb1
            jb = idx % nb1
            st = r + jb * (d1 * BAND)
            sp = r + jnp.maximum(jb - 1, 0) * (d1 * BAND)
            first = jnp.where(jb == 0, 1, 0)
            cur = pl.ds(st, BAND, stride=d1)
            prv = pl.ds(sp, BAND, stride=d1)
            work.append((cur, first, (q1[0, cur, :], k1[0, cur, :], k1[0, prv, :],
                                      v1[0, cur, :], v1[0, prv, :]), load_state(cur)))
        outs = [_attn_plane(*args, lambda pr, first=first: bias2(1, first, pr), prev)
                for _, first, args, prev in work]
        for (cur, _, _, _), res in zip(work, outs):
            store_state(cur, res)
        return c

    lax.fori_loop(0, d1 * nb1 // ATTN_BLOCKS_PER_STEP, body1, 0)

    def body0(i, c):
        work = []
        for u in range(ATTN_BLOCKS_PER_STEP):
            jb = i * ATTN_BLOCKS_PER_STEP + u
            st = pl.multiple_of(jb * BAND, BAND)
            sp = pl.multiple_of(jnp.maximum(jb - 1, 0) * BAND, BAND)
            first = jnp.where(jb == 0, 1, 0)
            cur = pl.ds(st, BAND)
            work.append((cur, first, (q0[0, cur, :], k0[0, cur, :], k0[0, pl.ds(sp, BAND), :],
                                      v0[0, cur, :], v0[0, pl.ds(sp, BAND), :]), load_state(cur)))
        outs = [_attn_plane(*args, lambda pr, first=first: bias2(0, first, pr), prev)
                for _, first, args, prev in work]
        for (cur, _, _, _), res in zip(work, outs):
            o_ref[cur, :LANES] = res[0][0].astype(bf16)
            o_ref[cur, LANES:] = res[1][0].astype(bf16)
        return c

    lax.fori_loop(0, seq // BAND // ATTN_BLOCKS_PER_STEP, body0, 0)


def _attn_prompt(planes, bias_tab, batch, seq):
    n = batch * seq
    ppk = PLANES_PER_KIND

    def spec(g, kind):
        return pl.BlockSpec((1, seq, LANES), lambda b, p: ((g * 3 + kind) * ppk + p, b, 0))

    in_specs = [spec(g, kind) for g in range(N_DGROUPS) for kind in range(3)]
    in_specs.append(pl.BlockSpec((N_DGROUPS, 2, 4, BAND, 2 * BAND), lambda b, p: (0, 0, p, 0, 0)))
    return pl.pallas_call(
        functools.partial(_attn_kernel, seq=seq),
        grid=(batch, ppk),
        in_specs=in_specs,
        out_specs=pl.BlockSpec((seq, 2 * LANES), lambda b, p: (b, p)),
        out_shape=jax.ShapeDtypeStruct((n, HD), bf16),
        scratch_shapes=[pltpu.VMEM((seq, LANES), f32)] * 4,
        compiler_params=_cparams(("arbitrary", "arbitrary")),
        name="attn_prompt",
    )(*([planes] * 9), bias_tab)


SAMPLE_HEADS_PER_STEP = 4


def _sample_kv_kernel(c_ref, q_ref, kn_ref, vn_ref, knc_ref, vnc_ref, bias_ref, b0_ref,
                      o_ref, m_ref, l_ref, acc_ref, *, length):
    k = c_ref[0, 0, 0]
    v = c_ref[0, 0, 1]
    hb = k.shape[0]
    q = q_ref[0, 0]
    kn = kn_ref[0, 0]
    vn = vn_ref[0, 0]
    rb = lambda a: a.astype(bf16).astype(f32)
    q8 = jnp.broadcast_to(q[:, None, :], (hb, SUBLANES, HEAD_DIM)).astype(bf16)
    s = jnp.einsum("hqe,hel->hql", q8, k.astype(bf16), preferred_element_type=f32)[:, 0:1, :]
    s = s + bias_ref[...]
    s0 = jnp.sum(rb(kn) * rb(q), axis=-1, keepdims=True)[:, :, None] + b0_ref[...]
    m = jnp.maximum(jnp.max(s, axis=2, keepdims=True), s0)
    p = jnp.exp(s - m)
    p0 = jnp.exp(s0 - m)
    l = jnp.sum(p, axis=2, keepdims=True) + p0
    p8 = jnp.broadcast_to(p, (hb, SUBLANES, length)).astype(bf16)
    acc = jnp.einsum("hql,hel->hqe", p8, v.astype(bf16), preferred_element_type=f32)[:, 0, :]
    acc = acc + rb(p0[:, 0, :]) * rb(vn)
    m_ref[0, 0] = m[:, 0, :]
    l_ref[0, 0] = l[:, 0, :]
    acc_ref[0, 0] = acc
    last = lax.broadcasted_iota(i32, k.shape, 2) == length - 1
    o_ref[0, 0, 0] = jnp.where(last, knc_ref[0], pltpu.roll(k, length - 1, 2))
    o_ref[0, 0, 1] = jnp.where(last, vnc_ref[0], pltpu.roll(v, length - 1, 2))


def _sample_kv_call(cache_t, q, kn, vn, bias, b0, nb):
    length = cache_t.shape[-1]
    hb = SAMPLE_HEADS_PER_STEP
    nh = N_HEADS // hb
    row = pl.BlockSpec((1, 1, hb, HEAD_DIM), lambda b, h: (b, h, 0, 0))
    col = pl.BlockSpec((1, hb, HEAD_DIM, 1), lambda b, h: (b, h, 0, 0))
    stat = pl.BlockSpec((1, 1, hb, 1), lambda b, h: (b, h, 0, 0))
    cspec = pl.BlockSpec((1, 1, 2, hb, HEAD_DIM, length), lambda b, h: (0, b, 0, h, 0, 0))
    grp = lambda a: a.reshape(nb, nh, hb, HEAD_DIM)
    new_t, m, l, acc = pl.pallas_call(
        functools.partial(_sample_kv_kernel, length=length),
        grid=(nb, nh),
        in_specs=[cspec, row, row, row, col, col,
                  pl.BlockSpec((hb, 1, length), lambda b, h: (h, 0, 0)),
                  pl.BlockSpec((hb, 1, 1), lambda b, h: (h, 0, 0))],
        out_specs=[cspec, stat, stat, row],
        out_shape=[jax.ShapeDtypeStruct(cache_t.shape, f32),
                   jax.ShapeDtypeStruct((nb, nh, hb, 1), f32),
                   jax.ShapeDtypeStruct((nb, nh, hb, 1), f32),
                   jax.ShapeDtypeStruct((nb, nh, hb, HEAD_DIM), f32)],
        compiler_params=_cparams(("arbitrary", "arbitrary")),
        name="sample_kv_group",
    )(cache_t, grp(q), grp(kn), grp(vn), kn[..., None], vn[..., None], bias, b0)
    return new_t, m.reshape(nb, N_HEADS, 1), l.reshape(nb, N_HEADS, 1), acc.reshape(nb, N_HEADS, HEAD_DIM)


def _sample_merge_kernel(m_ref, l_ref, acc_ref, o_ref):
    m = m_ref[...]
    mm = jnp.max(m, axis=0)
    w = jnp.exp(m - mm[None])
    num = jnp.sum(w * acc_ref[...], axis=0)
    den = jnp.sum(w * l_ref[...], axis=0)
    o_ref[...] = num / den


def _sample_merge(ms, ls, accs):
    return pl.pallas_call(
        _sample_merge_kernel,
        out_shape=jax.ShapeDtypeStruct(accs.shape[1:], f32),
        name="sample_merge",
    )(ms, ls, accs)


def _route(u, wr_ref, br_ref):
    lg = _dot(u.astype(bf16), wr_ref[...]) + br_ref[...]
    rows = u.shape[0]
    lane = lax.broadcasted_iota(i32, (rows, LANES), 1)
    lanef = lane.astype(f32)
    big = jnp.float32(1e9)
    isg = lane < N_EGROUPS
    lgg = jnp.where(isg, lg, NEG)
    mg = jnp.max(lgg, axis=1, keepdims=True)
    eg = jnp.where(isg, jnp.exp(lgg - mg), 0.0)
    pg = eg / jnp.sum(eg, axis=1, keepdims=True)
    gw = jnp.max(pg, axis=1, keepdims=True)
    gi = jnp.min(jnp.where(isg & (pg == gw), lanef, big), axis=1, keepdims=True)
    grp_of_lane = ((lane - N_EGROUPS) >> 2).astype(f32)
    sel = (lane >= N_EGROUPS) & (lane < N_EGROUPS + N_EXPERTS) & (grp_of_lane == gi)
    les = jnp.where(sel, lg, NEG)
    v1 = jnp.max(les, axis=1, keepdims=True)
    i1 = jnp.min(jnp.where(sel & (les == v1), lanef, big), axis=1, keepdims=True)
    sel2 = sel & (lanef != i1)
    les2 = jnp.where(sel2, lg, NEG)
    v2 = jnp.max(les2, axis=1, keepdims=True)
    i2 = jnp.min(jnp.where(sel2 & (les2 == v2), lanef, big), axis=1, keepdims=True)
    ex = jnp.exp(v2 - v1)
    den = 1.0 + ex
    pw1 = (1.0 / den) * gw
    pw2 = (ex / den) * gw
    e1 = i1 - N_EGROUPS
    e2 = i2 - N_EGROUPS
    first_low = e1 < e2
    lo = jnp.minimum(e1, e2) - 4.0 * gi
    hi = jnp.maximum(e1, e2) - 4.0 * gi
    pair = jnp.where(lo == 0.0, hi - 1.0, jnp.where(lo == 1.0, hi + 1.0, 5.0))
    cls = gi * 6.0 + pair
    g_lo = jnp.where(first_low, pw1, pw2)
    g_hi = jnp.where(first_low, pw2, pw1)
    zero = jnp.zeros((rows, LANES), f32)
    return jnp.where(lane == 0, cls, jnp.where(lane == 1, g_lo, jnp.where(lane == 2, g_hi, zero)))


def _write_slab(slab_ref, u, meta):
    rows = u.shape[0]
    half = D_MODEL // 2
    for c in range(4):
        w = _pack_words(u[:, c * LANES:(c + 1) * LANES], u[:, half + c * LANES:half + (c + 1) * LANES])
        slab_ref[pl.ds(c, rows, stride=SUBLANES), :] = w
    slab_ref[pl.ds(4, rows, stride=SUBLANES), :] = lax.bitcast_convert_type(meta, u32)
    z = jnp.zeros((rows, LANES), u32)
    for c in range(5, SUBLANES):
        slab_ref[pl.ds(c, rows, stride=SUBLANES), :] = z


def _ffn_prep(h, gn_ref, wr_ref, br_ref, slab_ref, metat_ref):
    u = _rms(h, gn_ref[...])
    meta = _route(u, wr_ref, br_ref)
    _write_slab(slab_ref, u, meta)
    metat_ref[...] = meta.T[:SUBLANES, :]


def _from_slab_f32(ref, rows):
    return jnp.concatenate([ref[pl.ds(c, rows, stride=SUBLANES), :] for c in range(SUBLANES)], axis=1)


def _wo_kernel(o_ref, x_ref, wo_ref, gn_ref, wr_ref, br_ref, h_ref, slab_ref, metat_ref):
    h = x_ref[...] + _dot(o_ref[...], wo_ref[...])
    h_ref[...] = h
    _ffn_prep(h, gn_ref, wr_ref, br_ref, slab_ref, metat_ref)


def _router_specs():
    cmap = lambda i: (0, 0)
    return [pl.BlockSpec((1, D_MODEL), cmap),
            pl.BlockSpec((D_MODEL, LANES), cmap),
            pl.BlockSpec((1, LANES), cmap)]


def _prep_out(n, tm):
    specs = [pl.BlockSpec((tm, D_MODEL), lambda i: (i, 0)),
             pl.BlockSpec((tm * SUBLANES, LANES), lambda i: (i, 0)),
             pl.BlockSpec((SUBLANES, tm), lambda i: (0, i))]
    shapes = [jax.ShapeDtypeStruct((n, D_MODEL), f32),
              jax.ShapeDtypeStruct((n * SUBLANES, LANES), u32),
              jax.ShapeDtypeStruct((SUBLANES, n), f32)]
    return specs, shapes


def _wo_call(o_bf, x2, wo_bf, router, tm):
    n = x2.shape[0]
    ospecs, oshapes = _prep_out(n, tm)
    return pl.pallas_call(
        _wo_kernel,
        grid=(n // tm,),
        in_specs=[pl.BlockSpec((tm, HD), lambda i: (i, 0)),
                  pl.BlockSpec((tm, D_MODEL), lambda i: (i, 0)),
                  pl.BlockSpec((HD, D_MODEL), lambda i: (0, 0))] + _router_specs(),
        out_specs=ospecs,
        out_shape=oshapes,
        compiler_params=_cparams(("arbitrary",)),
        name="wo_residual",
    )(o_bf, x2, wo_bf, *router)


def _rank_kernel(metat_ref, rank_ref, cnt_ref, carry_ref, *, tr):
    @pl.when(pl.program_id(0) == 0)
    def _():
        carry_ref[...] = jnp.zeros_like(carry_ref)

    cls = metat_ref[0:1, :]
    sub = lax.broadcasted_iota(i32, (32, tr), 0).astype(f32)
    oh = jnp.where(sub == cls, 1.0, 0.0)
    rr = lax.broadcasted_iota(i32, (tr, tr), 0)
    cc = lax.broadcasted_iota(i32, (tr, tr), 1)
    upper = jnp.where(rr < cc, 1.0, 0.0).astype(bf16)
    pre = _dot(oh.astype(bf16), upper)
    carry = carry_ref[:, 0:1]
    rank_ref[...] = jnp.sum(oh * (pre + carry), axis=0, keepdims=True).astype(i32)
    newc = carry + jnp.sum(oh, axis=1, keepdims=True)
    carry_ref[...] = jnp.broadcast_to(newc, carry_ref.shape)
    cnt_ref[...] = carry_ref[...]


def _rank_call(metat, tr):
    n = metat.shape[1]
    return pl.pallas_call(
        functools.partial(_rank_kernel, tr=tr),
        grid=(n // tr,),
        in_specs=[pl.BlockSpec((SUBLANES, tr), lambda i: (0, i))],
        out_specs=[pl.BlockSpec((1, tr), lambda i: (0, i)),
                   pl.BlockSpec((32, LANES), lambda i: (0, 0))],
        out_shape=[jax.ShapeDtypeStruct((1, n), i32), jax.ShapeDtypeStruct((32, LANES), f32)],
        scratch_shapes=[pltpu.VMEM((32, LANES), f32)],
        compiler_params=_cparams(("arbitrary",)),
        name="class_rank",
    )(metat)


ROWS_PER_GATHER_STEP = 256


GATHER_UNROLL = 8


def _gather_kernel(idx_ref, src_ref, o_ref, sem, *, rows):
    base = pl.program_id(0) * rows

    def issue(kk, c):
        for u in range(GATHER_UNROLL):
            k = kk * GATHER_UNROLL + u
            t = idx_ref[base + k]
            pltpu.make_async_copy(
                src_ref.at[pl.ds(pl.multiple_of(t * SUBLANES, SUBLANES), SUBLANES)],
                o_ref.at[pl.ds(pl.multiple_of(k * SUBLANES, SUBLANES), SUBLANES)], sem).start()
        return c

    lax.fori_loop(0, rows // GATHER_UNROLL, issue, 0)
    pltpu.make_async_copy(src_ref.at[pl.ds(0, rows * SUBLANES)], o_ref, sem).wait()


def _gather_slabs(idx, src):
    n_out = idx.shape[0]
    rows = math.gcd(ROWS_PER_GATHER_STEP, n_out)
    assert rows % GATHER_UNROLL == 0 and src.shape[0] >= rows * SUBLANES
    return pl.pallas_call(
        functools.partial(_gather_kernel, rows=rows),
        grid_spec=pltpu.PrefetchScalarGridSpec(
            num_scalar_prefetch=1, grid=(n_out // rows,),
            in_specs=[pl.BlockSpec(memory_space=pl.ANY)],
            out_specs=pl.BlockSpec((rows * SUBLANES, LANES), lambda i, idx: (i, 0)),
            scratch_shapes=[pltpu.SemaphoreType.DMA(())]),
        out_shape=jax.ShapeDtypeStruct((n_out * SUBLANES, LANES), src.dtype),
        compiler_params=_cparams(("arbitrary",)),
        name="gather_slabs",
    )(idx, src)


def _start_slab_gather(idx_ref, base, rows, src_ref, buf, sl, sems):
    def issue(kk, c):
        for u in range(GATHER_UNROLL):
            k = kk * GATHER_UNROLL + u
            t = idx_ref[base + k]
            pltpu.make_async_copy(
                src_ref.at[pl.ds(pl.multiple_of(t * SUBLANES, SUBLANES), SUBLANES)],
                buf.at[sl, pl.ds(pl.multiple_of(k * SUBLANES, SUBLANES), SUBLANES)],
                sems.at[sl]).start()
        return c

    lax.fori_loop(0, rows // GATHER_UNROLL, issue, 0)


def _wait_slab_gather(rows, src_ref, buf, sl, sems):
    pltpu.make_async_copy(src_ref.at[pl.ds(0, rows * SUBLANES)], buf.at[sl], sems.at[sl]).wait()


def _prefetched_slabs(idx_ref, src_ref, buf, sems, rows):
    i = pl.program_id(0)
    slot = i % 2

    @pl.when(i == 0)
    def _():
        _start_slab_gather(idx_ref, 0, rows, src_ref, buf, slot, sems)

    @pl.when(i + 1 < pl.num_programs(0))
    def _():
        _start_slab_gather(idx_ref, (i + 1) * rows, rows, src_ref, buf, 1 - slot, sems)

    _wait_slab_gather(rows, src_ref, buf, slot, sems)
    return buf.at[slot]


def _expert_kernel(e1_ref, e2_ref, valid_ref, src_ref, slab_ref, wg1, wu1, wd1, wg2, wu2, wd2,
                   y_ref, xbuf, sems, *, te):
    del e1_ref, e2_ref
    i = pl.program_id(0)
    nt = pl.num_programs(0)
    slot = i % 2

    def start_gather(tile, sl):
        _start_slab_gather(src_ref, tile * te, te, slab_ref, xbuf, sl, sems)

    valid = valid_ref[i]
    nxt = jnp.minimum(i + 1, nt - 1)

    @pl.when((i == 0) & (valid > 0))
    def _():
        start_gather(i, slot)

    @pl.when((i + 1 < nt) & (valid_ref[nxt] > 0))
    def _():
        start_gather(i + 1, 1 - slot)

    @pl.when(valid == 0)
    def _():
        y_ref[...] = jnp.zeros_like(y_ref)

    @pl.when(valid > 0)
    def _():
        _wait_slab_gather(te, slab_ref, xbuf, slot, sems)
        xs = xbuf.at[slot]
        los, his = [], []
        for c in range(4):
            lo, hi = _unpack_words(xs[pl.ds(c, te, stride=SUBLANES), :])
            los.append(lo)
            his.append(hi)
        x = jnp.concatenate(los + his, axis=1).astype(bf16)
        meta = lax.bitcast_convert_type(xs[pl.ds(4, te, stride=SUBLANES), :], f32)
        y = None
        for wg, wu, wd, lane in ((wg1, wu1, wd1, 1), (wg2, wu2, wd2, 2)):
            a = _dot(x, wg[0])
            c = _dot(x, wu[0])
            hh = (a * (1.0 / (1.0 + jnp.exp(-a)))) * c * meta[:, lane:lane + 1]
            part = _dot(hh.astype(bf16), wd[0])
            y = part if y is None else y + part
        for c in range(SUBLANES):
            y_ref[pl.ds(c, te, stride=SUBLANES), :] = y[:, c * LANES:(c + 1) * LANES]


def _expert_call(te1, te2, tvalid, src, slab, wg, wu, wd, te):
    ntiles = te1.shape[0]
    assert te % GATHER_UNROLL == 0 and slab.shape[0] >= te * SUBLANES

    def wspec(shape, which):
        if which == 1:
            return pl.BlockSpec(shape, lambda i, e1, e2, v, s: (e1[i], 0, 0))
        return pl.BlockSpec(shape, lambda i, e1, e2, v, s: (e2[i], 0, 0))

    gs = (1, D_MODEL, D_EXPERT)
    ds_ = (1, D_EXPERT, D_MODEL)
    return pl.pallas_call(
        functools.partial(_expert_kernel, te=te),
        grid_spec=pltpu.PrefetchScalarGridSpec(
            num_scalar_prefetch=4, grid=(ntiles,),
            in_specs=[pl.BlockSpec(memory_space=pl.ANY),
                      wspec(gs, 1), wspec(gs, 1), wspec(ds_, 1),
                      wspec(gs, 2), wspec(gs, 2), wspec(ds_, 2)],
            out_specs=pl.BlockSpec((te * SUBLANES, LANES), lambda i, e1, e2, v, s: (i, 0)),
            scratch_shapes=[pltpu.VMEM((2, te * SUBLANES, LANES), u32),
                            pltpu.SemaphoreType.DMA((2,))]),
        out_shape=jax.ShapeDtypeStruct((ntiles * te * SUBLANES, LANES), f32),
        compiler_params=_cparams(("arbitrary",)),
        name="expert_pair_ffn",
    )(te1, te2, tvalid, src, slab, wg, wu, wd, wg, wu, wd)


def _moe(slab, metat, wg, wu, wd, te, tr):
    n = metat.shape[1]
    rank, cnt = _rank_call(metat, tr)
    counts = cnt[:N_CLASSES, 0].astype(i32)
    padded = ((counts + te - 1) // te) * te
    ends = jnp.cumsum(padded)
    off = ends - padded
    cls = metat[0].astype(i32)
    class_ids = jnp.arange(N_CLASSES, dtype=i32)
    pos = rank[0] + jnp.sum(jnp.where(cls[:, None] == class_ids[None, :], off[None, :], 0), axis=1)
    ntiles = -(-n // te) + N_CLASSES
    tstart = jnp.arange(ntiles, dtype=i32) * te
    tcls = jnp.minimum(jnp.sum((tstart[:, None] >= ends[None, :]).astype(i32), axis=1), N_CLASSES - 1)
    tvalid = (tstart < ends[-1]).astype(i32)
    te1 = jnp.asarray(_CLASS_E1)[tcls]
    te2 = jnp.asarray(_CLASS_E2)[tcls]
    src = jnp.zeros((ntiles * te,), i32).at[pos].set(jnp.arange(n, dtype=i32), unique_indices=True)
    return pos, _expert_call(te1, te2, tvalid, src, slab, wg, wu, wd, te)


def _pool_mix_tail(hf, up, ext, tpos0, pw_ref, pb_ref, ps_ref):
    rows = up.shape[0]
    tpos = tpos0 + lax.broadcasted_iota(i32, (rows, 1), 0)
    parts = []
    for gi, win in enumerate(POOL_WINDOWS):
        sl = slice(gi * POOL_CH, (gi + 1) * POOL_CH)
        s = ext[:, sl]
        for kk in range(gi + 1):
            s = s + pltpu.roll(s, 1 << kk, 0)
        cnt = jnp.minimum(win, tpos + 1).astype(f32)
        z = s[16:, :] / cnt - up[:, sl]
        parts.append(_dot(z.astype(bf16), pw_ref[gi]))
    y = (jnp.concatenate(parts, axis=1) + pb_ref[...]) * ps_ref[...]
    return hf + y


def _pool_kernel(pos_ref, h_ref, ys_ref, pn_ref, pw_ref, pb_ref, ps_ref, gn_ref, wr_ref, br_ref,
                 h2_ref, slab_ref, metat_ref, st_ref, ybuf, sems, halo_ref, *, tm, tps):
    r = pl.program_id(0) % tps

    @pl.when(pl.program_id(0) == 0)
    def _():
        halo_ref[...] = jnp.zeros_like(halo_ref)

    y = _prefetched_slabs(pos_ref, ys_ref, ybuf, sems, tm)
    hf = h_ref[...] + _from_slab_f32(y, tm)
    up = _rms(hf, pn_ref[...])
    uph = jnp.where(r > 0, halo_ref[...], 0.0)
    ext = jnp.concatenate([uph, up], axis=0)
    h2 = _pool_mix_tail(hf, up, ext, r * tm, pw_ref, pb_ref, ps_ref)
    h2_ref[...] = h2
    st_ref[...] = up[tm - 16:, :]
    halo_ref[...] = up[tm - 16:, :]
    _ffn_prep(h2, gn_ref, wr_ref, br_ref, slab_ref, metat_ref)


def _pool_call(h, pos, ys, pn, pw_bf, pb, psc, router, batch, seq, tm):
    n = h.shape[0]
    tps = seq // tm
    assert tm % GATHER_UNROLL == 0 and ys.shape[0] >= tm * SUBLANES
    c2 = lambda i, p: (0, 0)
    return pl.pallas_call(
        functools.partial(_pool_kernel, tm=tm, tps=tps),
        grid_spec=pltpu.PrefetchScalarGridSpec(
            num_scalar_prefetch=1, grid=(n // tm,),
            in_specs=[pl.BlockSpec((tm, D_MODEL), lambda i, p: (i, 0)),
                      pl.BlockSpec(memory_space=pl.ANY),
                      pl.BlockSpec((1, D_MODEL), c2),
                      pl.BlockSpec((4, POOL_CH, POOL_CH), lambda i, p: (0, 0, 0)),
                      pl.BlockSpec((1, D_MODEL), c2),
                      pl.BlockSpec((1, D_MODEL), c2),
                      pl.BlockSpec((1, D_MODEL), c2),
                      pl.BlockSpec((D_MODEL, LANES), c2),
                      pl.BlockSpec((1, LANES), c2)],
            out_specs=[pl.BlockSpec((tm, D_MODEL), lambda i, p: (i, 0)),
                       pl.BlockSpec((tm * SUBLANES, LANES), lambda i, p: (i, 0)),
                       pl.BlockSpec((SUBLANES, tm), lambda i, p: (0, i)),
                       pl.BlockSpec((16, D_MODEL), lambda i, p: (i // tps, 0))],
            scratch_shapes=[pltpu.VMEM((2, tm * SUBLANES, LANES), f32),
                            pltpu.SemaphoreType.DMA((2,)),
                            pltpu.VMEM((16, D_MODEL), f32)]),
        out_shape=[jax.ShapeDtypeStruct((n, D_MODEL), f32),
                   jax.ShapeDtypeStruct((n * SUBLANES, LANES), u32),
                   jax.ShapeDtypeStruct((SUBLANES, n), f32),
                   jax.ShapeDtypeStruct((batch * 16, D_MODEL), f32)],
        compiler_params=_cparams(("arbitrary",)),
        name="pool_mixer",
    )(pos, h, ys, pn, pw_bf, pb, psc, *router)


def _pool_sample_kernel(h_ref, y_ref, st_ref, pn_ref, pw_ref, pb_ref, ps_ref,
                        gn_ref, wr_ref, br_ref, h2_ref, slab_ref, metat_ref, nst_ref, *, nb):
    rows = h_ref.shape[0]
    hf = h_ref[...] + _from_slab_f32(y_ref, rows)
    up = _rms(hf, pn_ref[...])
    st = st_ref[...]
    ub = up[:nb, :]
    parts = []
    for gi, win in enumerate(POOL_WINDOWS):
        sl = slice(gi * POOL_CH, (gi + 1) * POOL_CH)
        tot = ub[:, sl] + jnp.sum(st[:, POOL_STATE - (win - 1):, sl], axis=1)
        z = tot / float(win) - ub[:, sl]
        parts.append(_dot(z.astype(bf16), pw_ref[gi]))
    y = (jnp.concatenate(parts, axis=1) + pb_ref[...]) * ps_ref[...]
    if rows > nb:
        y = jnp.concatenate([y, jnp.zeros((rows - nb, D_MODEL), f32)], axis=0)
    h2 = hf + y
    h2_ref[...] = h2
    nst_ref[:, :POOL_STATE - 1, :] = st[:, 1:, :]
    nst_ref[:, POOL_STATE - 1, :] = ub
    _ffn_prep(h2, gn_ref, wr_ref, br_ref, slab_ref, metat_ref)


def _pool_sample_call(h, ytt, state, pn, pw_bf, pb, psc, router, nb):
    rows = h.shape[0]
    ospecs, oshapes = _prep_out(rows, rows)
    ospecs.append(pl.BlockSpec((nb, POOL_STATE, D_MODEL), lambda i: (0, 0, 0)))
    oshapes.append(jax.ShapeDtypeStruct((nb, POOL_STATE, D_MODEL), f32))
    cm = lambda i: (0, 0)
    return pl.pallas_call(
        functools.partial(_pool_sample_kernel, nb=nb),
        grid=(1,),
        in_specs=[pl.BlockSpec((rows, D_MODEL), cm),
                  pl.BlockSpec((rows * SUBLANES, LANES), cm),
                  pl.BlockSpec((nb, POOL_STATE, D_MODEL), lambda i: (0, 0, 0)),
                  pl.BlockSpec((1, D_MODEL), cm),
                  pl.BlockSpec((4, POOL_CH, POOL_CH), lambda i: (0, 0, 0)),
                  pl.BlockSpec((1, D_MODEL), cm),
                  pl.BlockSpec((1, D_MODEL), cm)] + _router_specs(),
        out_specs=ospecs,
        out_shape=oshapes,
        compiler_params=_cparams(("arbitrary",)),
        name="pool_mixer_sample",
    )(h, ytt, state, pn, pw_bf, pb, psc, *router)


def _final_kernel(h_ref, y_ref, o_ref, *, tm):
    o_ref[...] = h_ref[...] + _from_slab_f32(y_ref, tm)


def _final_call(h, ytt, tm):
    n = h.shape[0]
    return pl.pallas_call(
        functools.partial(_final_kernel, tm=tm),
        grid=(n // tm,),
        in_specs=[pl.BlockSpec((tm, D_MODEL), lambda i: (i, 0)),
                  pl.BlockSpec((tm * SUBLANES, LANES), lambda i: (i, 0))],
        out_specs=pl.BlockSpec((tm, D_MODEL), lambda i: (i, 0)),
        out_shape=jax.ShapeDtypeStruct((n, D_MODEL), f32),
        compiler_params=_cparams(("arbitrary",)),
        name="final_residual",
    )(h, ytt)


def _final_gather_kernel(pos_ref, h_ref, ys_ref, o_ref, ybuf, sems, *, tm):
    y = _prefetched_slabs(pos_ref, ys_ref, ybuf, sems, tm)
    o_ref[...] = h_ref[...] + _from_slab_f32(y, tm)


def _final_gather_call(h, pos, ys, tm):
    n = h.shape[0]
    assert tm % GATHER_UNROLL == 0 and ys.shape[0] >= tm * SUBLANES
    return pl.pallas_call(
        functools.partial(_final_gather_kernel, tm=tm),
        grid_spec=pltpu.PrefetchScalarGridSpec(
            num_scalar_prefetch=1, grid=(n // tm,),
            in_specs=[pl.BlockSpec((tm, D_MODEL), lambda i, p: (i, 0)),
                      pl.BlockSpec(memory_space=pl.ANY)],
            out_specs=pl.BlockSpec((tm, D_MODEL), lambda i, p: (i, 0)),
            scratch_shapes=[pltpu.VMEM((2, tm * SUBLANES, LANES), f32),
                            pltpu.SemaphoreType.DMA((2,))]),
        out_shape=jax.ShapeDtypeStruct((n, D_MODEL), f32),
        compiler_params=_cparams(("arbitrary",)),
        name="final_residual_gather",
    )(pos, h, ys)


def _t5_bucket(dist):
    max_exact = N_BUCKETS // 2
    df = jnp.maximum(dist, max_exact).astype(f32)
    large = max_exact + (jnp.log(df / max_exact) / math.log(MAX_DISTANCE / max_exact)
                         * (N_BUCKETS - max_exact)).astype(i32)
    large = jnp.minimum(large, N_BUCKETS - 1)
    return jnp.where(dist < max_exact, dist, large)


def _prompt_bias_tables(rel_bias):
    qi = jnp.arange(BAND)[:, None]
    kj = jnp.arange(2 * BAND)[None, :]
    step = BAND + qi - kj
    tabs = []
    for g, (win, dil) in enumerate(zip(WINDOWS, DILATIONS)):
        n_step = win // dil
        tab = rel_bias[:, g * N_HEADS:(g + 1) * N_HEADS].astype(f32)
        svals = jnp.maximum(2 * BAND - 1 - jnp.arange(3 * BAND), 0)
        bvec = tab[_t5_bucket(svals * dil)].T
        bias = jnp.stack([bvec[:, BAND - 1 - q:3 * BAND - 1 - q] for q in range(BAND)], axis=1)
        band = (step >= 0) & (step <= n_step)
        reg = jnp.where(band[None], bias, NEG)
        first = jnp.where((band & (kj >= BAND))[None], bias, NEG)
        tabs.append(jnp.stack([reg, first]))
    return jnp.stack(tabs)


def _sample_bias_tables(rel_bias, lengths):
    bs, b0 = [], []
    for g, (win, dil) in enumerate(zip(WINDOWS, DILATIONS)):
        length = lengths[g]
        n_step = win // dil
        tab = rel_bias[:, g * N_HEADS:(g + 1) * N_HEADS].astype(f32)
        dist = length - jnp.arange(length)
        hit = (dist % dil == 0) & (dist <= n_step * dil)
        b = jnp.where(hit[:, None], tab[_t5_bucket(dist)], NEG)
        bs.append(b.T[:, None, :])
        b0.append(tab[_t5_bucket(jnp.zeros((1,), i32))].T[:, :, None])
    return bs, b0


def _router_weights(ffn_norm_i, wgr, bgr, wer, ber):
    w = jnp.zeros((D_MODEL, LANES), f32)
    w = w.at[:, :N_EGROUPS].set(wgr).at[:, N_EGROUPS:N_EGROUPS + N_EXPERTS].set(wer)
    b = jnp.zeros((1, LANES), f32)
    b = b.at[0, :N_EGROUPS].set(bgr).at[0, N_EGROUPS:N_EGROUPS + N_EXPERTS].set(ber)
    return ffn_norm_i.reshape(1, D_MODEL), w.astype(bf16), b


def _norm_mats():
    head_of_col = np.arange(HD) // HEAD_DIM
    s = (head_of_col[:, None] == np.arange(LANES)[None, :]).astype(np.float32)
    e = s.T
    return jnp.asarray(s, bf16), jnp.asarray(np.concatenate([e, e], axis=0), bf16)


def kernel(x_prompt, x_sample, cache_kv_w128, cache_kv_w512, cache_kv_w2048, state_pool, w_qkv, q_norm, k_norm, w_o, rel_bias, attn_norm, pool_norm, pool_w, pool_b, pool_scale, ffn_norm, router_group_w, router_group_b, router_expert_w, router_expert_b, w_gate, w_up, w_down):
    batch, seq, _ = x_prompt.shape
    nb = x_sample.shape[0]
    assert x_sample.shape[1] == 1
    n = batch * seq
    srows = LANES
    caches6 = (cache_kv_w128, cache_kv_w512, cache_kv_w2048)

    smat, e2mat = _norm_mats()
    wqkv_bf = w_qkv[0].astype(bf16)
    wo_bf = w_o[0].astype(bf16)
    gq = jnp.tile(q_norm[0], (1, N_HEADS)).reshape(N_DGROUPS, 1, HD)
    gk = jnp.tile(k_norm[0], (1, N_HEADS)).reshape(N_DGROUPS, 1, HD)
    an = attn_norm[0].reshape(1, D_MODEL)
    routers = [_router_weights(ffn_norm[i], router_group_w[i], router_group_b[i],
                               router_expert_w[i], router_expert_b[i]) for i in range(2)]
    wg_bf = [w_gate[i].astype(bf16) for i in range(2)]
    wu_bf = [w_up[i].astype(bf16) for i in range(2)]
    wd_bf = [w_down[i].astype(bf16) for i in range(2)]
    pw_bf = pool_w[0].astype(bf16)
    pn = pool_norm[0].reshape(1, D_MODEL)
    pb = pool_b[0].reshape(1, D_MODEL)
    psc = pool_scale[0].reshape(1, D_MODEL)

    x2 = x_prompt.reshape(n, D_MODEL)
    planes, kv2, kv1, kv0 = _qkv_prompt(x2, an, wqkv_bf, gq, gk, smat, e2mat, batch, seq)
    o_bf = _attn_prompt(planes, _prompt_bias_tables(rel_bias), batch, seq)
    h0, slab0, metat0 = _wo_call(o_bf, x2, wo_bf, routers[0], tm=512)
    pos0, ys0 = _moe(slab0, metat0, wg_bf[0], wu_bf[0], wd_bf[0], te=256, tr=512)
    h1, slab1, metat1, pst = _pool_call(h0, pos0, ys0, pn, pw_bf, pb, psc, routers[1], batch, seq, tm=512)
    pos1, ys1 = _moe(slab1, metat1, wg_bf[1], wu_bf[1], wd_bf[1], te=256, tr=512)
    y_prompt = _final_gather_call(h1, pos1, ys1, tm=512).reshape(batch, seq, D_MODEL)

    new_kv_p = [jnp.transpose(t, (0, 1, 5, 2, 3, 4)) for t in (kv0, kv1, kv2)]
    new_pool_p = pst.reshape(batch, 16, D_MODEL)[:, 16 - POOL_STATE:][None]

    xs = jnp.zeros((srows, D_MODEL), f32).at[:nb].set(x_sample[:, 0, :])
    qs, ks, vs = _qkv_sample(xs, an, wqkv_bf, gq, gk, smat, e2mat)
    to4 = lambda a: a[:, :nb].reshape(N_DGROUPS, nb, N_HEADS, HEAD_DIM)
    qs4, ks4, vs4 = to4(qs), to4(ks), to4(vs)
    bias_s, bias_s0 = _sample_bias_tables(rel_bias, [c.shape[2] for c in caches6])
    new_kv_s, ms, ls, accs = [], [], [], []
    for g in range(N_DGROUPS):
        cache_t = jnp.transpose(caches6[g], (0, 1, 3, 4, 5, 2))
        new_t, m_g, l_g, acc_g = _sample_kv_call(cache_t, qs4[g], ks4[g], vs4[g], bias_s[g], bias_s0[g], nb)
        new_kv_s.append(jnp.transpose(new_t, (0, 1, 5, 2, 3, 4)))
        ms.append(m_g)
        ls.append(l_g)
        accs.append(acc_g)
    os_ = _sample_merge(jnp.stack(ms), jnp.stack(ls), jnp.stack(accs))
    os_pad = jnp.zeros((srows, HD), bf16).at[:nb].set(os_.reshape(nb, HD).astype(bf16))
    hs0, sslab0, smetat0 = _wo_call(os_pad, xs, wo_bf, routers[0], tm=srows)
    ystt0 = _gather_slabs(*_moe(sslab0, smetat0, wg_bf[0], wu_bf[0], wd_bf[0], te=64, tr=srows))
    hs1, sslab1, smetat1, new_pool_s = _pool_sample_call(
        hs0, ystt0, state_pool[0], pn, pw_bf, pb, psc, routers[1], nb)
    ystt1 = _gather_slabs(*_moe(sslab1, smetat1, wg_bf[1], wu_bf[1], wd_bf[1], te=64, tr=srows))
    y_sample = _final_call(hs1, ystt1, tm=srows)[:nb].reshape(nb, 1, D_MODEL)

    return (y_prompt, y_sample, new_kv_p[0], new_kv_p[1], new_kv_p[2], new_pool_p,
            new_kv_s[0], new_kv_s[1], new_kv_s[2], new_pool_s[None])
```

```python
import functools
import math

import numpy as np
import jax
import jax.numpy as jnp
from jax import lax
from jax.experimental import pallas as pl
from jax.experimental.pallas import tpu as pltpu

f32 = jnp.float32
bf16 = jnp.bfloat16
u32 = jnp.uint32
i32 = jnp.int32

D_MODEL = 1024
N_HEADS = 16
HEAD_DIM = 64
HD = N_HEADS * HEAD_DIM
N_DGROUPS = 3
WINDOWS = (128, 512, 2048)
DILATIONS = (1, 4, 16)
BAND = 128
N_BUCKETS = 32
MAX_DISTANCE = 2048
POOL_WINDOWS = (2, 4, 8, 16)
POOL_CH = D_MODEL // 4
POOL_STATE = max(POOL_WINDOWS) - 1
N_EGROUPS = 4
EXPERTS_PER_GROUP = 4
N_EXPERTS = 16
D_EXPERT = D_MODEL // 2
RMS_EPS = 1e-6
NEG = -1e30

LANES = 128
SUBLANES = 8
HI_MASK = 0xFFFF0000
N_CLASSES = N_EGROUPS * 6
PLANES_PER_KIND = HD // (2 * LANES)
VMEM_LIMIT = 56 * 1024 * 1024

_PAIRS = [(a, b) for a in range(4) for b in range(a + 1, 4)]
_CLASS_E1 = np.array([g * 4 + _PAIRS[p][0] for g in range(4) for p in range(6)], np.int32)
_CLASS_E2 = np.array([g * 4 + _PAIRS[p][1] for g in range(4) for p in range(6)], np.int32)


def _cparams(sem=None, vmem=VMEM_LIMIT):
    kw = dict(vmem_limit_bytes=vmem)
    if sem is not None:
        kw["dimension_semantics"] = sem
    return pltpu.CompilerParams(**kw)


def _dot(a, b):
    return jnp.dot(a, b, preferred_element_type=f32)


def _rms(x, g):
    return x * lax.rsqrt(jnp.mean(x * x, axis=-1, keepdims=True) + RMS_EPS) * g


def _pack_words(lo, hi):
    lb = lax.bitcast_convert_type(lo.astype(bf16).astype(f32), u32) >> 16
    hb = lax.bitcast_convert_type(hi.astype(bf16).astype(f32), u32) & jnp.uint32(HI_MASK)
    return lb | hb


def _unpack_words(w):
    lo = lax.bitcast_convert_type(w << 16, f32)
    hi = lax.bitcast_convert_type(w & jnp.uint32(HI_MASK), f32)
    return lo, hi


def _head_norm(z, gain, s_ref, e2_ref):
    zz = z * z
    zz_hi = zz.astype(bf16)
    zz_lo = (zz - zz_hi.astype(f32)).astype(bf16)
    ss = _dot(zz_hi, s_ref[...]) + _dot(zz_lo, s_ref[...])
    r = lax.rsqrt(ss * (1.0 / HEAD_DIM) + RMS_EPS)
    r_hi = r.astype(bf16)
    r_lo = (r - r_hi.astype(f32)).astype(bf16)
    rb = _dot(jnp.concatenate([r_hi, r_lo], axis=1), e2_ref[...])
    return z * rb * gain


def _qkv_kernel(x_ref, gn_ref, w_ref, gq_ref, gk_ref, s_ref, e2_ref,
                planes_ref, kv2_ref, kv1_ref, kv0_ref, *, tm, tps, t512):
    j = pl.program_id(0)
    r = pl.program_id(1) % tps
    u = _rms(x_ref[...], gn_ref[...]).astype(bf16)
    q = _head_norm(_dot(u, w_ref[:, :HD]), gq_ref[0], s_ref, e2_ref) * (HEAD_DIM ** -0.5)
    k = _head_norm(_dot(u, w_ref[:, HD:2 * HD]), gk_ref[0], s_ref, e2_ref)
    v = _dot(u, w_ref[:, 2 * HD:])
    for kind, arr in enumerate((q, k, v)):
        for p in range(PLANES_PER_KIND):
            c0 = 2 * LANES * p
            planes_ref[kind * PLANES_PER_KIND + p] = _pack_words(
                arr[:, c0:c0 + LANES], arr[:, c0 + LANES:c0 + 2 * LANES])

    def store_t(ref, kk, vv):
        ref[0, 0, 0] = kk.T.reshape(N_HEADS, HEAD_DIM, kk.shape[0])
        ref[0, 0, 1] = vv.T.reshape(N_HEADS, HEAD_DIM, vv.shape[0])

    @pl.when(j == 2)
    def _():
        store_t(kv2_ref, k, v)

    @pl.when((j == 1) & (r >= tps - t512))
    def _():
        store_t(kv1_ref, k, v)

    @pl.when((j == 0) & (r == tps - 1))
    def _():
        store_t(kv0_ref, k[tm - BAND:, :], v[tm - BAND:, :])


def _qkv_prompt(x2, gn, w_bf, gq, gk, smat, e2mat, batch, seq, tm=256):
    n = x2.shape[0]
    nt = n // tm
    tps = seq // tm
    t512 = WINDOWS[1] // tm
    assert seq % tm == 0 and WINDOWS[1] % tm == 0 and tm >= BAND and seq == WINDOWS[2]

    def kv2_map(j, i):
        on = j == 2
        return (0, jnp.where(on, i // tps, 0), 0, 0, 0, jnp.where(on, i % tps, 0))

    def kv1_map(j, i):
        inside = jnp.maximum(i % tps - (tps - t512), 0)
        b = jnp.where(j == 1, i // tps, jnp.where(j < 1, 0, batch - 1))
        t = jnp.where(j == 1, inside, jnp.where(j < 1, 0, t512 - 1))
        return (0, b, 0, 0, 0, t)

    def kv0_map(j, i):
        return (0, jnp.where(j == 0, i // tps, batch - 1), 0, 0, 0, 0)

    nplanes = 3 * PLANES_PER_KIND
    kvt = lambda keep: (1, batch, 2, N_HEADS, HEAD_DIM, keep)
    kvblock = lambda cols: (1, 1, 2, N_HEADS, HEAD_DIM, cols)
    return pl.pallas_call(
        functools.partial(_qkv_kernel, tm=tm, tps=tps, t512=t512),
        grid=(N_DGROUPS, nt),
        in_specs=[
            pl.BlockSpec((tm, D_MODEL), lambda j, i: (i, 0)),
            pl.BlockSpec((1, D_MODEL), lambda j, i: (0, 0)),
            pl.BlockSpec((D_MODEL, 3 * HD), lambda j, i: (0, j)),
            pl.BlockSpec((1, 1, HD), lambda j, i: (j, 0, 0)),
            pl.BlockSpec((1, 1, HD), lambda j, i: (j, 0, 0)),
            pl.BlockSpec((HD, LANES), lambda j, i: (0, 0)),
            pl.BlockSpec((2 * LANES, HD), lambda j, i: (0, 0)),
        ],
        out_specs=[
            pl.BlockSpec((nplanes, tm, LANES), lambda j, i: (j, i, 0)),
            pl.BlockSpec(kvblock(tm), kv2_map),
            pl.BlockSpec(kvblock(tm), kv1_map),
            pl.BlockSpec(kvblock(BAND), kv0_map),
        ],
        out_shape=[
            jax.ShapeDtypeStruct((N_DGROUPS * nplanes, n, LANES), u32),
            jax.ShapeDtypeStruct(kvt(WINDOWS[2]), f32),
            jax.ShapeDtypeStruct(kvt(WINDOWS[1]), f32),
            jax.ShapeDtypeStruct(kvt(WINDOWS[0]), f32),
        ],
        compiler_params=_cparams(("arbitrary", "arbitrary")),
        name="qkv_prompt",
    )(x2, gn, w_bf, gq, gk, smat, e2mat)


def _qkv_sample_kernel(x_ref, gn_ref, w_ref, gq_ref, gk_ref, s_ref, e2_ref, q_ref, k_ref, v_ref):
    u = _rms(x_ref[...], gn_ref[...]).astype(bf16)
    z = _dot(u, w_ref[...])
    q_ref[0] = _head_norm(z[:, :HD], gq_ref[0], s_ref, e2_ref) * (HEAD_DIM ** -0.5)
    k_ref[0] = _head_norm(z[:, HD:2 * HD], gk_ref[0], s_ref, e2_ref)
    v_ref[0] = z[:, 2 * HD:]


def _qkv_sample(xs, gn, w_bf, gq, gk, smat, e2mat):
    rows = xs.shape[0]
    out = jax.ShapeDtypeStruct((N_DGROUPS, rows, HD), f32)
    ospec = pl.BlockSpec((1, rows, HD), lambda j: (j, 0, 0))
    return pl.pallas_call(
        _qkv_sample_kernel,
        grid=(N_DGROUPS,),
        in_specs=[
            pl.BlockSpec((rows, D_MODEL), lambda j: (0, 0)),
            pl.BlockSpec((1, D_MODEL), lambda j: (0, 0)),
            pl.BlockSpec((D_MODEL, 3 * HD), lambda j: (0, j)),
            pl.BlockSpec((1, 1, HD), lambda j: (j, 0, 0)),
            pl.BlockSpec((1, 1, HD), lambda j: (j, 0, 0)),
            pl.BlockSpec((HD, LANES), lambda j: (0, 0)),
            pl.BlockSpec((2 * LANES, HD), lambda j: (0, 0)),
        ],
        out_specs=[ospec, ospec, ospec],
        out_shape=[out, out, out],
        compiler_params=_cparams(("arbitrary",)),
        name="qkv_sample",
    )(xs, gn, w_bf, gq, gk, smat, e2mat)


ATTN_BLOCKS_PER_STEP = 8


def _attn_plane(qw, kcw, kpw, vcw, vpw, bias_fn, prev):
    q2 = _unpack_words(qw)
    kc2 = _unpack_words(kcw)
    vc2 = _unpack_words(vcw)
    if kpw is not None:
        kp2 = _unpack_words(kpw)
        vp2 = _unpack_words(vpw)
    lane = lax.broadcasted_iota(i32, (BAND, LANES), 1)
    left = lane < HEAD_DIM
    res = []
    for pr in range(2):
        q = q2[pr].astype(bf16)
        if kpw is not None:
            kk = jnp.concatenate([kp2[pr], kc2[pr]], axis=0).astype(bf16)
            vv = jnp.concatenate([vp2[pr], vc2[pr]], axis=0).astype(bf16)
        else:
            kk = kc2[pr].astype(bf16)
            vv = vc2[pr].astype(bf16)
        zq = jnp.zeros_like(q)
        qs = jnp.concatenate([jnp.where(left, q, zq), jnp.where(left, zq, q)], axis=0)
        s = lax.dot_general(qs, kk, (((1,), (1,)), ((), ())), preferred_element_type=f32)
        s = s + bias_fn(pr)
        m = jnp.max(s, axis=1, keepdims=True)
        p = jnp.exp(s - m)
        l = jnp.sum(p, axis=1, keepdims=True)
        o2 = _dot(p.astype(bf16), vv)
        l_pair = jnp.where(left, l[:BAND], l[BAND:])
        m_pair = jnp.where(left, m[:BAND], m[BAND:])
        o = jnp.where(left, o2[:BAND], o2[BAND:]) / l_pair
        lse = m_pair + jnp.log(l_pair)
        if prev is not None:
            po, pl_ = prev[pr]
            mx = jnp.maximum(pl_, lse)
            ea = jnp.exp(pl_ - mx)
            eb = jnp.exp(lse - mx)
            tot = ea + eb
            o = po * (ea / tot) + o * (eb / tot)
            lse = mx + jnp.log(tot)
        res.append((o, lse))
    return res


def _attn_kernel(q0, k0, v0, q1, k1, v1, q2, k2, v2, bias_ref, o_ref,
                 acc_lo, acc_hi, lse_lo, lse_hi, *, seq):
    state = ((acc_lo, lse_lo), (acc_hi, lse_hi))

    def load_state(rows):
        return tuple((a[rows, :], l[rows, :]) for a, l in state)

    def store_state(rows, res):
        for (a, l), (o, lse) in zip(state, res):
            a[rows, :] = o
            l[rows, :] = lse

    def bias2(g, first, pr, cols=None):
        if cols is None:
            b = bias_ref[g, first, pl.ds(2 * pr, 2)]
        else:
            b = bias_ref[g, first, pl.ds(2 * pr, 2), :, cols]
        return b.reshape(2 * BAND, b.shape[-1])

    d2 = DILATIONS[2]
    assert seq == d2 * BAND

    def body2(i, c):
        work = []
        for u in range(ATTN_BLOCKS_PER_STEP):
            cur = pl.ds(i * ATTN_BLOCKS_PER_STEP + u, BAND, stride=d2)
            work.append((cur, (q2[0, cur, :], k2[0, cur, :], None, v2[0, cur, :], None)))
        outs = [_attn_plane(*args, lambda pr: bias2(2, 0, pr, pl.ds(BAND, BAND)), None)
                for _, args in work]
        for (cur, _), res in zip(work, outs):
            store_state(cur, res)
        return c

    lax.fori_loop(0, d2 // ATTN_BLOCKS_PER_STEP, body2, 0)

    d1 = DILATIONS[1]
    nb1 = seq // (d1 * BAND)
    assert (d1 * nb1) % ATTN_BLOCKS_PER_STEP == 0 and d2 % ATTN_BLOCKS_PER_STEP == 0

    def body1(i, c):
        work = []
        for u in range(ATTN_BLOCKS_PER_STEP):
            idx = i * ATTN_BLOCKS_PER_STEP + u
            r = idx // nb1
            jb = idx % nb1
            st = r + jb * (d1 * BAND)
            sp = r + jnp.maximum(jb - 1, 0) * (d1 * BAND)
            first = jnp.where(jb == 0, 1, 0)
            cur = pl.ds(st, BAND, stride=d1)
            prv = pl.ds(sp, BAND, stride=d1)
            work.append((cur, first, (q1[0, cur, :], k1[0, cur, :], k1[0, prv, :],
                                      v1[0, cur, :], v1[0, prv, :]), load_state(cur)))
        outs = [_attn_plane(*args, lambda pr, first=first: bias2(1, first, pr), prev)
                for _, first, args, prev in work]
        for (cur, _, _, _), res in zip(work, outs):
            store_state(cur, res)
        return c

    lax.fori_loop(0, d1 * nb1 // ATTN_BLOCKS_PER_STEP, body1, 0)

    def body0(i, c):
        work = []
        for u in range(ATTN_BLOCKS_PER_STEP):
            jb = i * ATTN_BLOCKS_PER_STEP + u
            st = pl.multiple_of(jb * BAND, BAND)
            sp = pl.multiple_of(jnp.maximum(jb - 1, 0) * BAND, BAND)
            first = jnp.where(jb == 0, 1, 0)
            cur = pl.ds(st, BAND)
            work.append((cur, first, (q0[0, cur, :], k0[0, cur, :], k0[0, pl.ds(sp, BAND), :],
                                      v0[0, cur, :], v0[0, pl.ds(sp, BAND), :]), load_state(cur)))
        outs = [_attn_plane(*args, lambda pr, first=first: bias2(0, first, pr), prev)
                for _, first, args, prev in work]
        for (cur, _, _, _), res in zip(work, outs):
            o_ref[cur, :LANES] = res[0][0].astype(bf16)
            o_ref[cur, LANES:] = res[1][0].astype(bf16)
        return c

    lax.fori_loop(0, seq // BAND // ATTN_BLOCKS_PER_STEP, body0, 0)


def _attn_prompt(planes, bias_tab, batch, seq):
    n = batch * seq
    ppk = PLANES_PER_KIND

    def spec(g, kind):
        return pl.BlockSpec((1, seq, LANES), lambda b, p: ((g * 3 + kind) * ppk + p, b, 0))

    in_specs = [spec(g, kind) for g in range(N_DGROUPS) for kind in range(3)]
    in_specs.append(pl.BlockSpec((N_DGROUPS, 2, 4, BAND, 2 * BAND), lambda b, p: (0, 0, p, 0, 0)))
    return pl.pallas_call(
        functools.partial(_attn_kernel, seq=seq),
        grid=(batch, ppk),
        in_specs=in_specs,
        out_specs=pl.BlockSpec((seq, 2 * LANES), lambda b, p: (b, p)),
        out_shape=jax.ShapeDtypeStruct((n, HD), bf16),
        scratch_shapes=[pltpu.VMEM((seq, LANES), f32)] * 4,
        compiler_params=_cparams(("arbitrary", "arbitrary")),
        name="attn_prompt",
    )(*([planes] * 9), bias_tab)


SAMPLE_BLOCK_BYTES = 4 * 1024 * 1024


def _sample_kv_kernel(c_ref, q_ref, kn_ref, vn_ref, knc_ref, vnc_ref, bias_ref, b0_ref,
                      o_ref, m_ref, l_ref, acc_ref, *, length):
    k = c_ref[0, 0, 0]
    v = c_ref[0, 0, 1]
    hb = k.shape[0]
    q = q_ref[0, 0]
    kn = kn_ref[0, 0]
    vn = vn_ref[0, 0]
    rb = lambda a: a.astype(bf16).astype(f32)
    q8 = jnp.broadcast_to(q[:, None, :], (hb, SUBLANES, HEAD_DIM)).astype(bf16)
    s = jnp.einsum("hqe,hel->hql", q8, k.astype(bf16), preferred_element_type=f32)[:, 0:1, :]
    s = s + bias_ref[...]
    s0 = jnp.sum(rb(kn) * rb(q), axis=-1, keepdims=True)[:, :, None] + b0_ref[...]
    m = jnp.maximum(jnp.max(s, axis=2, keepdims=True), s0)
    p = jnp.exp(s - m)
    p0 = jnp.exp(s0 - m)
    l = jnp.sum(p, axis=2, keepdims=True) + p0
    p8 = jnp.broadcast_to(p, (hb, SUBLANES, length)).astype(bf16)
    acc = jnp.einsum("hql,hel->hqe", p8, v.astype(bf16), preferred_element_type=f32)[:, 0, :]
    acc = acc + rb(p0[:, 0, :]) * rb(vn)
    m_ref[0, 0] = m[:, 0, :]
    l_ref[0, 0] = l[:, 0, :]
    acc_ref[0, 0] = acc
    last = lax.broadcasted_iota(i32, k.shape, 2) == length - 1
    o_ref[0, 0, 0] = jnp.where(last, knc_ref[0], pltpu.roll(k, length - 1, 2))
    o_ref[0, 0, 1] = jnp.where(last, vnc_ref[0], pltpu.roll(v, length - 1, 2))


def _sample_kv_call(cache_t, q, kn, vn, bias, b0, nb):
    length = cache_t.shape[-1]
    hb = max(1, min(N_HEADS, SAMPLE_BLOCK_BYTES // (2 * HEAD_DIM * length * 4)))
    assert N_HEADS % hb == 0
    nh = N_HEADS // hb
    row = pl.BlockSpec((1, 1, hb, HEAD_DIM), lambda b, h: (b, h, 0, 0))
    col = pl.BlockSpec((1, hb, HEAD_DIM, 1), lambda b, h: (b, h, 0, 0))
    stat = pl.BlockSpec((1, 1, hb, 1), lambda b, h: (b, h, 0, 0))
    cspec = pl.BlockSpec((1, 1, 2, hb, HEAD_DIM, length), lambda b, h: (0, b, 0, h, 0, 0))
    grp = lambda a: a.reshape(nb, nh, hb, HEAD_DIM)
    new_t, m, l, acc = pl.pallas_call(
        functools.partial(_sample_kv_kernel, length=length),
        grid=(nb, nh),
        in_specs=[cspec, row, row, row, col, col,
                  pl.BlockSpec((hb, 1, length), lambda b, h: (h, 0, 0)),
                  pl.BlockSpec((hb, 1, 1), lambda b, h: (h, 0, 0))],
        out_specs=[cspec, stat, stat, row],
        out_shape=[jax.ShapeDtypeStruct(cache_t.shape, f32),
                   jax.ShapeDtypeStruct((nb, nh, hb, 1), f32),
                   jax.ShapeDtypeStruct((nb, nh, hb, 1), f32),
                   jax.ShapeDtypeStruct((nb, nh, hb, HEAD_DIM), f32)],
        compiler_params=_cparams(("arbitrary", "arbitrary")),
        name="sample_kv_group",
    )(cache_t, grp(q), grp(kn), grp(vn), kn[..., None], vn[..., None], bias, b0)
    return new_t, m.reshape(nb, N_HEADS, 1), l.reshape(nb, N_HEADS, 1), acc.reshape(nb, N_HEADS, HEAD_DIM)


def _sample_merge_kernel(m_ref, l_ref, acc_ref, o_ref):
    m = m_ref[...]
    mm = jnp.max(m, axis=0)
    w = jnp.exp(m - mm[None])
    num = jnp.sum(w * acc_ref[...], axis=0)
    den = jnp.sum(w * l_ref[...], axis=0)
    o_ref[...] = num / den


def _sample_merge(ms, ls, accs):
    return pl.pallas_call(
        _sample_merge_kernel,
        out_shape=jax.ShapeDtypeStruct(accs.shape[1:], f32),
        name="sample_merge",
    )(ms, ls, accs)


def _route(u, wr_ref, br_ref):
    lg = _dot(u.astype(bf16), wr_ref[...]) + br_ref[...]
    rows = u.shape[0]
    lane = lax.broadcasted_iota(i32, (rows, LANES), 1)
    lanef = lane.astype(f32)
    big = jnp.float32(1e9)
    isg = lane < N_EGROUPS
    lgg = jnp.where(isg, lg, NEG)
    mg = jnp.max(lgg, axis=1, keepdims=True)
    eg = jnp.where(isg, jnp.exp(lgg - mg), 0.0)
    pg = eg / jnp.sum(eg, axis=1, keepdims=True)
    gw = jnp.max(pg, axis=1, keepdims=True)
    gi = jnp.min(jnp.where(isg & (pg == gw), lanef, big), axis=1, keepdims=True)
    grp_of_lane = ((lane - N_EGROUPS) >> 2).astype(f32)
    sel = (lane >= N_EGROUPS) & (lane < N_EGROUPS + N_EXPERTS) & (grp_of_lane == gi)
    les = jnp.where(sel, lg, NEG)
    v1 = jnp.max(les, axis=1, keepdims=True)
    i1 = jnp.min(jnp.where(sel & (les == v1), lanef, big), axis=1, keepdims=True)
    sel2 = sel & (lanef != i1)
    les2 = jnp.where(sel2, lg, NEG)
    v2 = jnp.max(les2, axis=1, keepdims=True)
    i2 = jnp.min(jnp.where(sel2 & (les2 == v2), lanef, big), axis=1, keepdims=True)
    ex = jnp.exp(v2 - v1)
    den = 1.0 + ex
    pw1 = (1.0 / den) * gw
    pw2 = (ex / den) * gw
    e1 = i1 - N_EGROUPS
    e2 = i2 - N_EGROUPS
    first_low = e1 < e2
    lo = jnp.minimum(e1, e2) - 4.0 * gi
    hi = jnp.maximum(e1, e2) - 4.0 * gi
    pair = jnp.where(lo == 0.0, hi - 1.0, jnp.where(lo == 1.0, hi + 1.0, 5.0))
    cls = gi * 6.0 + pair
    g_lo = jnp.where(first_low, pw1, pw2)
    g_hi = jnp.where(first_low, pw2, pw1)
    zero = jnp.zeros((rows, LANES), f32)
    return jnp.where(lane == 0, cls, jnp.where(lane == 1, g_lo, jnp.where(lane == 2, g_hi, zero)))


def _write_slab(slab_ref, u, meta):
    rows = u.shape[0]
    half = D_MODEL // 2
    for c in range(4):
        w = _pack_words(u[:, c * LANES:(c + 1) * LANES], u[:, half + c * LANES:half + (c + 1) * LANES])
        slab_ref[pl.ds(c, rows, stride=SUBLANES), :] = w
    slab_ref[pl.ds(4, rows, stride=SUBLANES), :] = lax.bitcast_convert_type(meta, u32)
    z = jnp.zeros((rows, LANES), u32)
    for c in range(5, SUBLANES):
        slab_ref[pl.ds(c, rows, stride=SUBLANES), :] = z


def _ffn_prep(h, gn_ref, wr_ref, br_ref, slab_ref, metat_ref):
    u = _rms(h, gn_ref[...])
    meta = _route(u, wr_ref, br_ref)
    _write_slab(slab_ref, u, meta)
    metat_ref[...] = meta.T[:SUBLANES, :]


def _from_slab_f32(ref, rows):
    return jnp.concatenate([ref[pl.ds(c, rows, stride=SUBLANES), :] for c in range(SUBLANES)], axis=1)


def _wo_kernel(o_ref, x_ref, wo_ref, gn_ref, wr_ref, br_ref, h_ref, slab_ref, metat_ref):
    h = x_ref[...] + _dot(o_ref[...], wo_ref[...])
    h_ref[...] = h
    _ffn_prep(h, gn_ref, wr_ref, br_ref, slab_ref, metat_ref)


def _router_specs():
    cmap = lambda i: (0, 0)
    return [pl.BlockSpec((1, D_MODEL), cmap),
            pl.BlockSpec((D_MODEL, LANES), cmap),
            pl.BlockSpec((1, LANES), cmap)]


def _prep_out(n, tm):
    specs = [pl.BlockSpec((tm, D_MODEL), lambda i: (i, 0)),
             pl.BlockSpec((tm * SUBLANES, LANES), lambda i: (i, 0)),
             pl.BlockSpec((SUBLANES, tm), lambda i: (0, i))]
    shapes = [jax.ShapeDtypeStruct((n, D_MODEL), f32),
              jax.ShapeDtypeStruct((n * SUBLANES, LANES), u32),
              jax.ShapeDtypeStruct((SUBLANES, n), f32)]
    return specs, shapes


def _wo_call(o_bf, x2, wo_bf, router, tm):
    n = x2.shape[0]
    ospecs, oshapes = _prep_out(n, tm)
    return pl.pallas_call(
        _wo_kernel,
        grid=(n // tm,),
        in_specs=[pl.BlockSpec((tm, HD), lambda i: (i, 0)),
                  pl.BlockSpec((tm, D_MODEL), lambda i: (i, 0)),
                  pl.BlockSpec((HD, D_MODEL), lambda i: (0, 0))] + _router_specs(),
        out_specs=ospecs,
        out_shape=oshapes,
        compiler_params=_cparams(("arbitrary",)),
        name="wo_residual",
    )(o_bf, x2, wo_bf, *router)


def _rank_kernel(metat_ref, rank_ref, cnt_ref, carry_ref, *, tr):
    @pl.when(pl.program_id(0) == 0)
    def _():
        carry_ref[...] = jnp.zeros_like(carry_ref)

    cls = metat_ref[0:1, :]
    sub = lax.broadcasted_iota(i32, (32, tr), 0).astype(f32)
    oh = jnp.where(sub == cls, 1.0, 0.0)
    rr = lax.broadcasted_iota(i32, (tr, tr), 0)
    cc = lax.broadcasted_iota(i32, (tr, tr), 1)
    upper = jnp.where(rr < cc, 1.0, 0.0).astype(bf16)
    pre = _dot(oh.astype(bf16), upper)
    carry = carry_ref[:, 0:1]
    rank_ref[...] = jnp.sum(oh * (pre + carry), axis=0, keepdims=True).astype(i32)
    newc = carry + jnp.sum(oh, axis=1, keepdims=True)
    carry_ref[...] = jnp.broadcast_to(newc, carry_ref.shape)
    cnt_ref[...] = carry_ref[...]


def _rank_call(metat, tr):
    n = metat.shape[1]
    return pl.pallas_call(
        functools.partial(_rank_kernel, tr=tr),
        grid=(n // tr,),
        in_specs=[pl.BlockSpec((SUBLANES, tr), lambda i: (0, i))],
        out_specs=[pl.BlockSpec((1, tr), lambda i: (0, i)),
                   pl.BlockSpec((32, LANES), lambda i: (0, 0))],
        out_shape=[jax.ShapeDtypeStruct((1, n), i32), jax.ShapeDtypeStruct((32, LANES), f32)],
        scratch_shapes=[pltpu.VMEM((32, LANES), f32)],
        compiler_params=_cparams(("arbitrary",)),
        name="class_rank",
    )(metat)


ROWS_PER_GATHER_STEP = 256


GATHER_UNROLL = 8


def _gather_kernel(idx_ref, src_ref, o_ref, sem, *, rows):
    base = pl.program_id(0) * rows

    def issue(kk, c):
        for u in range(GATHER_UNROLL):
            k = kk * GATHER_UNROLL + u
            t = idx_ref[base + k]
            pltpu.make_async_copy(
                src_ref.at[pl.ds(pl.multiple_of(t * SUBLANES, SUBLANES), SUBLANES)],
                o_ref.at[pl.ds(pl.multiple_of(k * SUBLANES, SUBLANES), SUBLANES)], sem).start()
        return c

    lax.fori_loop(0, rows // GATHER_UNROLL, issue, 0)
    pltpu.make_async_copy(src_ref.at[pl.ds(0, rows * SUBLANES)], o_ref, sem).wait()


def _gather_slabs(idx, src):
    n_out = idx.shape[0]
    rows = math.gcd(ROWS_PER_GATHER_STEP, n_out)
    assert rows % GATHER_UNROLL == 0 and src.shape[0] >= rows * SUBLANES
    return pl.pallas_call(
        functools.partial(_gather_kernel, rows=rows),
        grid_spec=pltpu.PrefetchScalarGridSpec(
            num_scalar_prefetch=1, grid=(n_out // rows,),
            in_specs=[pl.BlockSpec(memory_space=pl.ANY)],
            out_specs=pl.BlockSpec((rows * SUBLANES, LANES), lambda i, idx: (i, 0)),
            scratch_shapes=[pltpu.SemaphoreType.DMA(())]),
        out_shape=jax.ShapeDtypeStruct((n_out * SUBLANES, LANES), src.dtype),
        compiler_params=_cparams(("arbitrary",)),
        name="gather_slabs",
    )(idx, src)


def _start_slab_gather(idx_ref, base, rows, src_ref, buf, sl, sems):
    def issue(kk, c):
        for u in range(GATHER_UNROLL):
            k = kk * GATHER_UNROLL + u
            t = idx_ref[base + k]
            pltpu.make_async_copy(
                src_ref.at[pl.ds(pl.multiple_of(t * SUBLANES, SUBLANES), SUBLANES)],
                buf.at[sl, pl.ds(pl.multiple_of(k * SUBLANES, SUBLANES), SUBLANES)],
                sems.at[sl]).start()
        return c

    lax.fori_loop(0, rows // GATHER_UNROLL, issue, 0)


def _wait_slab_gather(rows, src_ref, buf, sl, sems):
    pltpu.make_async_copy(src_ref.at[pl.ds(0, rows * SUBLANES)], buf.at[sl], sems.at[sl]).wait()


def _prefetched_slabs(idx_ref, src_ref, buf, sems, rows):
    i = pl.program_id(0)
    slot = i % 2

    @pl.when(i == 0)
    def _():
        _start_slab_gather(idx_ref, 0, rows, src_ref, buf, slot, sems)

    @pl.when(i + 1 < pl.num_programs(0))
    def _():
        _start_slab_gather(idx_ref, (i + 1) * rows, rows, src_ref, buf, 1 - slot, sems)

    _wait_slab_gather(rows, src_ref, buf, slot, sems)
    return buf.at[slot]


def _expert_kernel(e1_ref, e2_ref, valid_ref, src_ref, slab_ref, wg1, wu1, wd1, wg2, wu2, wd2,
                   y_ref, xbuf, sems, *, te):
    del e1_ref, e2_ref
    i = pl.program_id(0)
    nt = pl.num_programs(0)
    slot = i % 2
    valid = valid_ref[i]

    @pl.when((i == 0) & (valid > 0))
    def _():
        _start_slab_gather(src_ref, 0, te, slab_ref, xbuf, slot, sems)

    @pl.when((valid == 0) & (i > 0) & (valid_ref[jnp.maximum(i - 1, 0)] > 0))
    def _():
        _wait_slab_gather(te, slab_ref, xbuf, slot, sems)

    @pl.when(valid == 0)
    def _():
        y_ref[...] = jnp.zeros_like(y_ref)

    @pl.when(valid > 0)
    def _():
        _wait_slab_gather(te, slab_ref, xbuf, slot, sems)
        base = jnp.minimum(i + 1, nt - 1) * te
        for k in range(te):
            t = src_ref[base + k]
            pltpu.make_async_copy(
                slab_ref.at[pl.ds(pl.multiple_of(t * SUBLANES, SUBLANES), SUBLANES)],
                xbuf.at[1 - slot, pl.ds(k * SUBLANES, SUBLANES)], sems.at[1 - slot]).start()
        xs = xbuf.at[slot]
        los, his = [], []
        for c in range(4):
            lo, hi = _unpack_words(xs[pl.ds(c, te, stride=SUBLANES), :])
            los.append(lo)
            his.append(hi)
        x = jnp.concatenate(los + his, axis=1).astype(bf16)
        meta = lax.bitcast_convert_type(xs[pl.ds(4, te, stride=SUBLANES), :], f32)
        y = None
        for wg, wu, wd, lane in ((wg1, wu1, wd1, 1), (wg2, wu2, wd2, 2)):
            a = _dot(x, wg[0])
            c = _dot(x, wu[0])
            hh = (a * (1.0 / (1.0 + jnp.exp(-a)))) * c * meta[:, lane:lane + 1]
            part = _dot(hh.astype(bf16), wd[0])
            y = part if y is None else y + part
        for c in range(SUBLANES):
            y_ref[pl.ds(c, te, stride=SUBLANES), :] = y[:, c * LANES:(c + 1) * LANES]

    @pl.when((valid > 0) & (i == nt - 1))
    def _():
        _wait_slab_gather(te, slab_ref, xbuf, 1 - slot, sems)


def _expert_call(te1, te2, tvalid, src, slab, wg, wu, wd, te):
    ntiles = te1.shape[0]
    assert te % GATHER_UNROLL == 0 and slab.shape[0] >= te * SUBLANES

    def wspec(shape, which):
        if which == 1:
            return pl.BlockSpec(shape, lambda i, e1, e2, v, s: (e1[i], 0, 0))
        return pl.BlockSpec(shape, lambda i, e1, e2, v, s: (e2[i], 0, 0))

    gs = (1, D_MODEL, D_EXPERT)
    ds_ = (1, D_EXPERT, D_MODEL)
    return pl.pallas_call(
        functools.partial(_expert_kernel, te=te),
        grid_spec=pltpu.PrefetchScalarGridSpec(
            num_scalar_prefetch=4, grid=(ntiles,),
            in_specs=[pl.BlockSpec(memory_space=pl.ANY),
                      wspec(gs, 1), wspec(gs, 1), wspec(ds_, 1),
                      wspec(gs, 2), wspec(gs, 2), wspec(ds_, 2)],
            out_specs=pl.BlockSpec((te * SUBLANES, LANES), lambda i, e1, e2, v, s: (i, 0)),
            scratch_shapes=[pltpu.VMEM((2, te * SUBLANES, LANES), u32),
                            pltpu.SemaphoreType.DMA((2,))]),
        out_shape=jax.ShapeDtypeStruct((ntiles * te * SUBLANES, LANES), f32),
        compiler_params=_cparams(("arbitrary",)),
        name="expert_pair_ffn",
    )(te1, te2, tvalid, src, slab, wg, wu, wd, wg, wu, wd)


def _moe(slab, metat, wg, wu, wd, layer, te, tr):
    n = metat.shape[1]
    rank, cnt = _rank_call(metat, tr)
    counts = cnt[:N_CLASSES, 0].astype(i32)
    padded = ((counts + te - 1) // te) * te
    ends = jnp.cumsum(padded)
    off = ends - padded
    cls = metat[0].astype(i32)
    class_ids = jnp.arange(N_CLASSES, dtype=i32)
    pos = rank[0] + jnp.sum(jnp.where(cls[:, None] == class_ids[None, :], off[None, :], 0), axis=1)
    ntiles = -(-n // te) + N_CLASSES
    tstart = jnp.arange(ntiles, dtype=i32) * te
    tcls = jnp.minimum(jnp.sum((tstart[:, None] >= ends[None, :]).astype(i32), axis=1), N_CLASSES - 1)
    tvalid = (tstart < ends[-1]).astype(i32)
    te1 = jnp.asarray(_CLASS_E1 + layer * N_EXPERTS)[tcls]
    te2 = jnp.asarray(_CLASS_E2 + layer * N_EXPERTS)[tcls]
    src = jnp.zeros((ntiles * te,), i32).at[pos].set(jnp.arange(n, dtype=i32), unique_indices=True)
    return pos, _expert_call(te1, te2, tvalid, src, slab, wg, wu, wd, te)


def _pool_mix_tail(hf, up, ext, tpos0, pw_ref, pb_ref, ps_ref):
    rows = up.shape[0]
    tpos = tpos0 + lax.broadcasted_iota(i32, (rows, 1), 0)
    parts = []
    for gi, win in enumerate(POOL_WINDOWS):
        sl = slice(gi * POOL_CH, (gi + 1) * POOL_CH)
        s = ext[:, sl]
        for kk in range(gi + 1):
            s = s + pltpu.roll(s, 1 << kk, 0)
        cnt = jnp.minimum(win, tpos + 1).astype(f32)
        z = s[16:, :] / cnt - up[:, sl]
        parts.append(_dot(z.astype(bf16), pw_ref[gi]))
    y = (jnp.concatenate(parts, axis=1) + pb_ref[...]) * ps_ref[...]
    return hf + y


def _pool_kernel(pos_ref, h_ref, ys_ref, pn_ref, pw_ref, pb_ref, ps_ref, gn_ref, wr_ref, br_ref,
                 h2_ref, slab_ref, metat_ref, st_ref, ybuf, sems, halo_ref, *, tm, tps):
    r = pl.program_id(0) % tps

    @pl.when(pl.program_id(0) == 0)
    def _():
        halo_ref[...] = jnp.zeros_like(halo_ref)

    y = _prefetched_slabs(pos_ref, ys_ref, ybuf, sems, tm)
    hf = h_ref[...] + _from_slab_f32(y, tm)
    up = _rms(hf, pn_ref[...])
    uph = jnp.where(r > 0, halo_ref[...], 0.0)
    ext = jnp.concatenate([uph, up], axis=0)
    h2 = _pool_mix_tail(hf, up, ext, r * tm, pw_ref, pb_ref, ps_ref)
    h2_ref[...] = h2
    st_ref[...] = up[tm - 16:, :]
    halo_ref[...] = up[tm - 16:, :]
    _ffn_prep(h2, gn_ref, wr_ref, br_ref, slab_ref, metat_ref)


def _pool_call(h, pos, ys, pn, pw_bf, pb, psc, router, batch, seq, tm):
    n = h.shape[0]
    tps = seq // tm
    assert tm % GATHER_UNROLL == 0 and ys.shape[0] >= tm * SUBLANES
    c2 = lambda i, p: (0, 0)
    return pl.pallas_call(
        functools.partial(_pool_kernel, tm=tm, tps=tps),
        grid_spec=pltpu.PrefetchScalarGridSpec(
            num_scalar_prefetch=1, grid=(n // tm,),
            in_specs=[pl.BlockSpec((tm, D_MODEL), lambda i, p: (i, 0)),
                      pl.BlockSpec(memory_space=pl.ANY),
                      pl.BlockSpec((1, D_MODEL), c2),
                      pl.BlockSpec((4, POOL_CH, POOL_CH), lambda i, p: (0, 0, 0)),
                      pl.BlockSpec((1, D_MODEL), c2),
                      pl.BlockSpec((1, D_MODEL), c2),
                      pl.BlockSpec((1, D_MODEL), c2),
                      pl.BlockSpec((D_MODEL, LANES), c2),
                      pl.BlockSpec((1, LANES), c2)],
            out_specs=[pl.BlockSpec((tm, D_MODEL), lambda i, p: (i, 0)),
                       pl.BlockSpec((tm * SUBLANES, LANES), lambda i, p: (i, 0)),
                       pl.BlockSpec((SUBLANES, tm), lambda i, p: (0, i)),
                       pl.BlockSpec((16, D_MODEL), lambda i, p: (i // tps, 0))],
            scratch_shapes=[pltpu.VMEM((2, tm * SUBLANES, LANES), f32),
                            pltpu.SemaphoreType.DMA((2,)),
                            pltpu.VMEM((16, D_MODEL), f32)]),
        out_shape=[jax.ShapeDtypeStruct((n, D_MODEL), f32),
                   jax.ShapeDtypeStruct((n * SUBLANES, LANES), u32),
                   jax.ShapeDtypeStruct((SUBLANES, n), f32),
                   jax.ShapeDtypeStruct((batch * 16, D_MODEL), f32)],
        compiler_params=_cparams(("arbitrary",)),
        name="pool_mixer",
    )(pos, h, ys, pn, pw_bf, pb, psc, *router)


def _pool_sample_kernel(h_ref, y_ref, st_ref, pn_ref, pw_ref, pb_ref, ps_ref,
                        gn_ref, wr_ref, br_ref, h2_ref, slab_ref, metat_ref, nst_ref, *, nb):
    rows = h_ref.shape[0]
    hf = h_ref[...] + _from_slab_f32(y_ref, rows)
    up = _rms(hf, pn_ref[...])
    st = st_ref[...]
    ub = up[:nb, :]
    parts = []
    for gi, win in enumerate(POOL_WINDOWS):
        sl = slice(gi * POOL_CH, (gi + 1) * POOL_CH)
        tot = ub[:, sl] + jnp.sum(st[:, POOL_STATE - (win - 1):, sl], axis=1)
        z = tot / float(win) - ub[:, sl]
        parts.append(_dot(z.astype(bf16), pw_ref[gi]))
    y = (jnp.concatenate(parts, axis=1) + pb_ref[...]) * ps_ref[...]
    if rows > nb:
        y = jnp.concatenate([y, jnp.zeros((rows - nb, D_MODEL), f32)], axis=0)
    h2 = hf + y
    h2_ref[...] = h2
    nst_ref[:, :POOL_STATE - 1, :] = st[:, 1:, :]
    nst_ref[:, POOL_STATE - 1, :] = ub
    _ffn_prep(h2, gn_ref, wr_ref, br_ref, slab_ref, metat_ref)


def _pool_sample_call(h, ytt, state, pn, pw_bf, pb, psc, router, nb):
    rows = h.shape[0]
    ospecs, oshapes = _prep_out(rows, rows)
    ospecs.append(pl.BlockSpec((nb, POOL_STATE, D_MODEL), lambda i: (0, 0, 0)))
    oshapes.append(jax.ShapeDtypeStruct((nb, POOL_STATE, D_MODEL), f32))
    cm = lambda i: (0, 0)
    return pl.pallas_call(
        functools.partial(_pool_sample_kernel, nb=nb),
        grid=(1,),
        in_specs=[pl.BlockSpec((rows, D_MODEL), cm),
                  pl.BlockSpec((rows * SUBLANES, LANES), cm),
                  pl.BlockSpec((nb, POOL_STATE, D_MODEL), lambda i: (0, 0, 0)),
                  pl.BlockSpec((1, D_MODEL), cm),
                  pl.BlockSpec((4, POOL_CH, POOL_CH), lambda i: (0, 0, 0)),
                  pl.BlockSpec((1, D_MODEL), cm),
                  pl.BlockSpec((1, D_MODEL), cm)] + _router_specs(),
        out_specs=ospecs,
        out_shape=oshapes,
        compiler_params=_cparams(("arbitrary",)),
        name="pool_mixer_sample",
    )(h, ytt, state, pn, pw_bf, pb, psc, *router)


def _final_kernel(h_ref, y_ref, o_ref, *, tm):
    o_ref[...] = h_ref[...] + _from_slab_f32(y_ref, tm)


def _final_call(h, ytt, tm):
    n = h.shape[0]
    return pl.pallas_call(
        functools.partial(_final_kernel, tm=tm),
        grid=(n // tm,),
        in_specs=[pl.BlockSpec((tm, D_MODEL), lambda i: (i, 0)),
                  pl.BlockSpec((tm * SUBLANES, LANES), lambda i: (i, 0))],
        out_specs=pl.BlockSpec((tm, D_MODEL), lambda i: (i, 0)),
        out_shape=jax.ShapeDtypeStruct((n, D_MODEL), f32),
        compiler_params=_cparams(("arbitrary",)),
        name="final_residual",
    )(h, ytt)


def _final_gather_kernel(pos_ref, h_ref, ys_ref, o_ref, ybuf, sems, *, tm):
    y = _prefetched_slabs(pos_ref, ys_ref, ybuf, sems, tm)
    o_ref[...] = h_ref[...] + _from_slab_f32(y, tm)


def _final_gather_call(h, pos, ys, tm):
    n = h.shape[0]
    assert tm % GATHER_UNROLL == 0 and ys.shape[0] >= tm * SUBLANES
    return pl.pallas_call(
        functools.partial(_final_gather_kernel, tm=tm),
        grid_spec=pltpu.PrefetchScalarGridSpec(
            num_scalar_prefetch=1, grid=(n // tm,),
            in_specs=[pl.BlockSpec((tm, D_MODEL), lambda i, p: (i, 0)),
                      pl.BlockSpec(memory_space=pl.ANY)],
            out_specs=pl.BlockSpec((tm, D_MODEL), lambda i, p: (i, 0)),
            scratch_shapes=[pltpu.VMEM((2, tm * SUBLANES, LANES), f32),
                            pltpu.SemaphoreType.DMA((2,))]),
        out_shape=jax.ShapeDtypeStruct((n, D_MODEL), f32),
        compiler_params=_cparams(("arbitrary",)),
        name="final_residual_gather",
    )(pos, h, ys)


def _t5_bucket(dist):
    max_exact = N_BUCKETS // 2
    df = jnp.maximum(dist, max_exact).astype(f32)
    large = max_exact + (jnp.log(df / max_exact) / math.log(MAX_DISTANCE / max_exact)
                         * (N_BUCKETS - max_exact)).astype(i32)
    large = jnp.minimum(large, N_BUCKETS - 1)
    return jnp.where(dist < max_exact, dist, large)


def _prompt_bias_tables(rel_bias):
    qi = jnp.arange(BAND)[:, None]
    kj = jnp.arange(2 * BAND)[None, :]
    step = BAND + qi - kj
    tabs = []
    for g, (win, dil) in enumerate(zip(WINDOWS, DILATIONS)):
        n_step = win // dil
        tab = rel_bias[:, g * N_HEADS:(g + 1) * N_HEADS].astype(f32)
        svals = jnp.maximum(2 * BAND - 1 - jnp.arange(3 * BAND), 0)
        bvec = tab[_t5_bucket(svals * dil)].T
        bias = jnp.stack([bvec[:, BAND - 1 - q:3 * BAND - 1 - q] for q in range(BAND)], axis=1)
        band = (step >= 0) & (step <= n_step)
        reg = jnp.where(band[None], bias, NEG)
        first = jnp.where((band & (kj >= BAND))[None], bias, NEG)
        tabs.append(jnp.stack([reg, first]))
    return jnp.stack(tabs)


def _sample_bias_tables(rel_bias, lengths):
    bs, b0 = [], []
    for g, (win, dil) in enumerate(zip(WINDOWS, DILATIONS)):
        length = lengths[g]
        n_step = win // dil
        tab = rel_bias[:, g * N_HEADS:(g + 1) * N_HEADS].astype(f32)
        dist = length - jnp.arange(length)
        hit = (dist % dil == 0) & (dist <= n_step * dil)
        b = jnp.where(hit[:, None], tab[_t5_bucket(dist)], NEG)
        bs.append(b.T[:, None, :])
        b0.append(tab[_t5_bucket(jnp.zeros((1,), i32))].T[:, :, None])
    return bs, b0


def _router_weights(ffn_norm_i, wgr, bgr, wer, ber):
    w = jnp.zeros((D_MODEL, LANES), f32)
    w = w.at[:, :N_EGROUPS].set(wgr).at[:, N_EGROUPS:N_EGROUPS + N_EXPERTS].set(wer)
    b = jnp.zeros((1, LANES), f32)
    b = b.at[0, :N_EGROUPS].set(bgr).at[0, N_EGROUPS:N_EGROUPS + N_EXPERTS].set(ber)
    return ffn_norm_i.reshape(1, D_MODEL), w.astype(bf16), b


def _norm_mats():
    head_of_col = np.arange(HD) // HEAD_DIM
    s = (head_of_col[:, None] == np.arange(LANES)[None, :]).astype(np.float32)
    e = s.T
    return jnp.asarray(s, bf16), jnp.asarray(np.concatenate([e, e], axis=0), bf16)


def kernel(x_prompt, x_sample, cache_kv_w128, cache_kv_w512, cache_kv_w2048, state_pool, w_qkv, q_norm, k_norm, w_o, rel_bias, attn_norm, pool_norm, pool_w, pool_b, pool_scale, ffn_norm, router_group_w, router_group_b, router_expert_w, router_expert_b, w_gate, w_up, w_down):
    batch, seq, _ = x_prompt.shape
    nb = x_sample.shape[0]
    assert x_sample.shape[1] == 1
    n = batch * seq
    srows = LANES
    caches6 = (cache_kv_w128, cache_kv_w512, cache_kv_w2048)

    smat, e2mat = _norm_mats()
    wqkv_bf = w_qkv[0].astype(bf16)
    wo_bf = w_o[0].astype(bf16)
    gq = jnp.tile(q_norm[0], (1, N_HEADS)).reshape(N_DGROUPS, 1, HD)
    gk = jnp.tile(k_norm[0], (1, N_HEADS)).reshape(N_DGROUPS, 1, HD)
    an = attn_norm[0].reshape(1, D_MODEL)
    routers = [_router_weights(ffn_norm[i], router_group_w[i], router_group_b[i],
                               router_expert_w[i], router_expert_b[i]) for i in range(2)]
    allx = lambda w: w.astype(bf16).reshape((-1,) + w.shape[2:])
    wg_bf, wu_bf, wd_bf = allx(w_gate), allx(w_up), allx(w_down)
    pw_bf = pool_w[0].astype(bf16)
    pn = pool_norm[0].reshape(1, D_MODEL)
    pb = pool_b[0].reshape(1, D_MODEL)
    psc = pool_scale[0].reshape(1, D_MODEL)

    x2 = x_prompt.reshape(n, D_MODEL)
    planes, kv2, kv1, kv0 = _qkv_prompt(x2, an, wqkv_bf, gq, gk, smat, e2mat, batch, seq)
    o_bf = _attn_prompt(planes, _prompt_bias_tables(rel_bias), batch, seq)
    h0, slab0, metat0 = _wo_call(o_bf, x2, wo_bf, routers[0], tm=512)
    pos0, ys0 = _moe(slab0, metat0, wg_bf, wu_bf, wd_bf, 0, te=256, tr=512)
    h1, slab1, metat1, pst = _pool_call(h0, pos0, ys0, pn, pw_bf, pb, psc, routers[1], batch, seq, tm=512)
    pos1, ys1 = _moe(slab1, metat1, wg_bf, wu_bf, wd_bf, 1, te=256, tr=512)
    y_prompt = _final_gather_call(h1, pos1, ys1, tm=512).reshape(batch, seq, D_MODEL)

    new_kv_p = [jnp.transpose(t, (0, 1, 5, 2, 3, 4)) for t in (kv0, kv1, kv2)]
    new_pool_p = pst.reshape(batch, 16, D_MODEL)[:, 16 - POOL_STATE:][None]

    xs = jnp.zeros((srows, D_MODEL), f32).at[:nb].set(x_sample[:, 0, :])
    qs, ks, vs = _qkv_sample(xs, an, wqkv_bf, gq, gk, smat, e2mat)
    to4 = lambda a: a[:, :nb].reshape(N_DGROUPS, nb, N_HEADS, HEAD_DIM)
    qs4, ks4, vs4 = to4(qs), to4(ks), to4(vs)
    bias_s, bias_s0 = _sample_bias_tables(rel_bias, [c.shape[2] for c in caches6])
    new_kv_s, ms, ls, accs = [], [], [], []
    for g in range(N_DGROUPS):
        cache_t = jnp.transpose(caches6[g], (0, 1, 3, 4, 5, 2))
        new_t, m_g, l_g, acc_g = _sample_kv_call(cache_t, qs4[g], ks4[g], vs4[g], bias_s[g], bias_s0[g], nb)
        new_kv_s.append(jnp.transpose(new_t, (0, 1, 5, 2, 3, 4)))
        ms.append(m_g)
        ls.append(l_g)
        accs.append(acc_g)
    os_ = _sample_merge(jnp.stack(ms), jnp.stack(ls), jnp.stack(accs))
    os_pad = jnp.zeros((srows, HD), bf16).at[:nb].set(os_.reshape(nb, HD).astype(bf16))
    hs0, sslab0, smetat0 = _wo_call(os_pad, xs, wo_bf, routers[0], tm=srows)
    ystt0 = _gather_slabs(*_moe(sslab0, smetat0, wg_bf, wu_bf, wd_bf, 0, te=64, tr=srows))
    hs1, sslab1, smetat1, new_pool_s = _pool_sample_call(
        hs0, ystt0, state_pool[0], pn, pw_bf, pb, psc, routers[1], nb)
    ystt1 = _gather_slabs(*_moe(sslab1, smetat1, wg_bf, wu_bf, wd_bf, 1, te=64, tr=srows))
    y_sample = _final_call(hs1, ystt1, tm=srows)[:nb].reshape(nb, 1, D_MODEL)

    return (y_prompt, y_sample, new_kv_p[0], new_kv_p[1], new_kv_p[2], new_pool_p,
            new_kv_s[0], new_kv_s[1], new_kv_s[2], new_pool_s[None])
```

```python
import functools
import math

import numpy as np
import jax
import jax.numpy as jnp
from jax import lax
from jax.experimental import pallas as pl
from jax.experimental.pallas import tpu as pltpu

f32 = jnp.float32
bf16 = jnp.bfloat16
u32 = jnp.uint32
i32 = jnp.int32

D_MODEL = 1024
N_HEADS = 16
HEAD_DIM = 64
HD = N_HEADS * HEAD_DIM
N_DGROUPS = 3
WINDOWS = (128, 512, 2048)
DILATIONS = (1, 4, 16)
BAND = 128
N_BUCKETS = 32
MAX_DISTANCE = 2048
POOL_WINDOWS = (2, 4, 8, 16)
POOL_CH = D_MODEL // 4
POOL_STATE = max(POOL_WINDOWS) - 1
N_EGROUPS = 4
EXPERTS_PER_GROUP = 4
N_EXPERTS = 16
D_EXPERT = D_MODEL // 2
RMS_EPS = 1e-6
NEG = -1e30

LANES = 128
SUBLANES = 8
HI_MASK = 0xFFFF0000
N_CLASSES = N_EGROUPS * 6
PLANES_PER_KIND = HD // (2 * LANES)
VMEM_LIMIT = 56 * 1024 * 1024

_PAIRS = [(a, b) for a in range(4) for b in range(a + 1, 4)]
_CLASS_E1 = np.array([g * 4 + _PAIRS[p][0] for g in range(4) for p in range(6)], np.int32)
_CLASS_E2 = np.array([g * 4 + _PAIRS[p][1] for g in range(4) for p in range(6)], np.int32)


def _cparams(sem=None, vmem=VMEM_LIMIT):
    kw = dict(vmem_limit_bytes=vmem)
    if sem is not None:
        kw["dimension_semantics"] = sem
    return pltpu.CompilerParams(**kw)


def _dot(a, b):
    return jnp.dot(a, b, preferred_element_type=f32)


def _rms(x, g):
    return x * lax.rsqrt(jnp.mean(x * x, axis=-1, keepdims=True) + RMS_EPS) * g


def _pack_words(lo, hi):
    lb = lax.bitcast_convert_type(lo.astype(bf16).astype(f32), u32) >> 16
    hb = lax.bitcast_convert_type(hi.astype(bf16).astype(f32), u32) & jnp.uint32(HI_MASK)
    return lb | hb


def _unpack_words(w):
    lo = lax.bitcast_convert_type(w << 16, f32)
    hi = lax.bitcast_convert_type(w & jnp.uint32(HI_MASK), f32)
    return lo, hi


def _head_norm(z, gain, s_ref, e2_ref):
    zz = z * z
    zz_hi = zz.astype(bf16)
    zz_lo = (zz - zz_hi.astype(f32)).astype(bf16)
    ss = _dot(zz_hi, s_ref[...]) + _dot(zz_lo, s_ref[...])
    r = lax.rsqrt(ss * (1.0 / HEAD_DIM) + RMS_EPS)
    r_hi = r.astype(bf16)
    r_lo = (r - r_hi.astype(f32)).astype(bf16)
    rb = _dot(jnp.concatenate([r_hi, r_lo], axis=1), e2_ref[...])
    return z * rb * gain


def _qkv_kernel(x_ref, gn_ref, w_ref, gq_ref, gk_ref, s_ref, e2_ref,
                planes_ref, kv2_ref, kv1_ref, kv0_ref, *, tm, tps, t512):
    j = pl.program_id(0)
    r = pl.program_id(1) % tps
    u = _rms(x_ref[...], gn_ref[...]).astype(bf16)
    q = _head_norm(_dot(u, w_ref[:, :HD]), gq_ref[0], s_ref, e2_ref) * (HEAD_DIM ** -0.5)
    k = _head_norm(_dot(u, w_ref[:, HD:2 * HD]), gk_ref[0], s_ref, e2_ref)
    v = _dot(u, w_ref[:, 2 * HD:])
    for kind, arr in enumerate((q, k, v)):
        for p in range(PLANES_PER_KIND):
            c0 = 2 * LANES * p
            planes_ref[kind * PLANES_PER_KIND + p] = _pack_words(
                arr[:, c0:c0 + LANES], arr[:, c0 + LANES:c0 + 2 * LANES])

    def store_t(ref, kk, vv):
        ref[0, 0, 0] = kk.T.reshape(N_HEADS, HEAD_DIM, kk.shape[0])
        ref[0, 0, 1] = vv.T.reshape(N_HEADS, HEAD_DIM, vv.shape[0])

    @pl.when(j == 2)
    def _():
        store_t(kv2_ref, k, v)

    @pl.when((j == 1) & (r >= tps - t512))
    def _():
        store_t(kv1_ref, k, v)

    @pl.when((j == 0) & (r == tps - 1))
    def _():
        store_t(kv0_ref, k[tm - BAND:, :], v[tm - BAND:, :])


def _qkv_prompt(x2, gn, w_bf, gq, gk, smat, e2mat, batch, seq, tm=256):
    n = x2.shape[0]
    nt = n // tm
    tps = seq // tm
    t512 = WINDOWS[1] // tm
    assert seq % tm == 0 and WINDOWS[1] % tm == 0 and tm >= BAND and seq == WINDOWS[2]

    def kv2_map(j, i):
        on = j == 2
        return (0, jnp.where(on, i // tps, 0), 0, 0, 0, jnp.where(on, i % tps, 0))

    def kv1_map(j, i):
        inside = jnp.maximum(i % tps - (tps - t512), 0)
        b = jnp.where(j == 1, i // tps, jnp.where(j < 1, 0, batch - 1))
        t = jnp.where(j == 1, inside, jnp.where(j < 1, 0, t512 - 1))
        return (0, b, 0, 0, 0, t)

    def kv0_map(j, i):
        return (0, jnp.where(j == 0, i // tps, batch - 1), 0, 0, 0, 0)

    nplanes = 3 * PLANES_PER_KIND
    kvt = lambda keep: (1, batch, 2, N_HEADS, HEAD_DIM, keep)
    kvblock = lambda cols: (1, 1, 2, N_HEADS, HEAD_DIM, cols)
    return pl.pallas_call(
        functools.partial(_qkv_kernel, tm=tm, tps=tps, t512=t512),
        grid=(N_DGROUPS, nt),
        in_specs=[
            pl.BlockSpec((tm, D_MODEL), lambda j, i: (i, 0)),
            pl.BlockSpec((1, D_MODEL), lambda j, i: (0, 0)),
            pl.BlockSpec((D_MODEL, 3 * HD), lambda j, i: (0, j)),
            pl.BlockSpec((1, 1, HD), lambda j, i: (j, 0, 0)),
            pl.BlockSpec((1, 1, HD), lambda j, i: (j, 0, 0)),
            pl.BlockSpec((HD, LANES), lambda j, i: (0, 0)),
            pl.BlockSpec((2 * LANES, HD), lambda j, i: (0, 0)),
        ],
        out_specs=[
            pl.BlockSpec((nplanes, tm, LANES), lambda j, i: (j, i, 0)),
            pl.BlockSpec(kvblock(tm), kv2_map),
            pl.BlockSpec(kvblock(tm), kv1_map),
            pl.BlockSpec(kvblock(BAND), kv0_map),
        ],
        out_shape=[
            jax.ShapeDtypeStruct((N_DGROUPS * nplanes, n, LANES), u32),
            jax.ShapeDtypeStruct(kvt(WINDOWS[2]), f32),
            jax.ShapeDtypeStruct(kvt(WINDOWS[1]), f32),
            jax.ShapeDtypeStruct(kvt(WINDOWS[0]), f32),
        ],
        compiler_params=_cparams(("arbitrary", "arbitrary")),
        name="qkv_prompt",
    )(x2, gn, w_bf, gq, gk, smat, e2mat)


def _qkv_sample_kernel(x_ref, gn_ref, w_ref, gq_ref, gk_ref, s_ref, e2_ref, q_ref, k_ref, v_ref):
    u = _rms(x_ref[...], gn_ref[...]).astype(bf16)
    z = _dot(u, w_ref[...])
    q_ref[0] = _head_norm(z[:, :HD], gq_ref[0], s_ref, e2_ref) * (HEAD_DIM ** -0.5)
    k_ref[0] = _head_norm(z[:, HD:2 * HD], gk_ref[0], s_ref, e2_ref)
    v_ref[0] = z[:, 2 * HD:]


def _qkv_sample(xs, gn, w_bf, gq, gk, smat, e2mat):
    rows = xs.shape[0]
    out = jax.ShapeDtypeStruct((N_DGROUPS, rows, HD), f32)
    ospec = pl.BlockSpec((1, rows, HD), lambda j: (j, 0, 0))
    return pl.pallas_call(
        _qkv_sample_kernel,
        grid=(N_DGROUPS,),
        in_specs=[
            pl.BlockSpec((rows, D_MODEL), lambda j: (0, 0)),
            pl.BlockSpec((1, D_MODEL), lambda j: (0, 0)),
            pl.BlockSpec((D_MODEL, 3 * HD), lambda j: (0, j)),
            pl.BlockSpec((1, 1, HD), lambda j: (j, 0, 0)),
            pl.BlockSpec((1, 1, HD), lambda j: (j, 0, 0)),
            pl.BlockSpec((HD, LANES), lambda j: (0, 0)),
            pl.BlockSpec((2 * LANES, HD), lambda j: (0, 0)),
        ],
        out_specs=[ospec, ospec, ospec],
        out_shape=[out, out, out],
        compiler_params=_cparams(("arbitrary",)),
        name="qkv_sample",
    )(xs, gn, w_bf, gq, gk, smat, e2mat)


ATTN_BLOCKS_PER_STEP = 8


def _attn_plane(qw, kcw, kpw, vcw, vpw, bias_fn, prev):
    q2 = _unpack_words(qw)
    kc2 = _unpack_words(kcw)
    vc2 = _unpack_words(vcw)
    if kpw is not None:
        kp2 = _unpack_words(kpw)
        vp2 = _unpack_words(vpw)
    lane = lax.broadcasted_iota(i32, (BAND, LANES), 1)
    left = lane < HEAD_DIM
    res = []
    for pr in range(2):
        q = q2[pr].astype(bf16)
        if kpw is not None:
            kk = jnp.concatenate([kp2[pr], kc2[pr]], axis=0).astype(bf16)
            vv = jnp.concatenate([vp2[pr], vc2[pr]], axis=0).astype(bf16)
        else:
            kk = kc2[pr].astype(bf16)
            vv = vc2[pr].astype(bf16)
        zq = jnp.zeros_like(q)
        qs = jnp.concatenate([jnp.where(left, q, zq), jnp.where(left, zq, q)], axis=0)
        s = lax.dot_general(qs, kk, (((1,), (1,)), ((), ())), preferred_element_type=f32)
        s = s + bias_fn(pr)
        m = jnp.max(s, axis=1, keepdims=True)
        p = jnp.exp(s - m)
        l = jnp.sum(p, axis=1, keepdims=True)
        o2 = _dot(p.astype(bf16), vv)
        l_pair = jnp.where(left, l[:BAND], l[BAND:])
        m_pair = jnp.where(left, m[:BAND], m[BAND:])
        o = jnp.where(left, o2[:BAND], o2[BAND:]) / l_pair
        lse = m_pair + jnp.log(l_pair)
        if prev is not None:
            po, pl_ = prev[pr]
            mx = jnp.maximum(pl_, lse)
            ea = jnp.exp(pl_ - mx)
            eb = jnp.exp(lse - mx)
            tot = ea + eb
            o = po * (ea / tot) + o * (eb / tot)
            lse = mx + jnp.log(tot)
        res.append((o, lse))
    return res


def _attn_kernel(q0, k0, v0, q1, k1, v1, q2, k2, v2, bias_ref, o_ref,
                 acc_lo, acc_hi, lse_lo, lse_hi, *, seq):
    state = ((acc_lo, lse_lo), (acc_hi, lse_hi))

    def load_state(rows):
        return tuple((a[rows, :], l[rows, :]) for a, l in state)

    def store_state(rows, res):
        for (a, l), (o, lse) in zip(state, res):
            a[rows, :] = o
            l[rows, :] = lse

    def bias2(g, first, pr, cols=None):
        if cols is None:
            b = bias_ref[g, first, pl.ds(2 * pr, 2)]
        else:
            b = bias_ref[g, first, pl.ds(2 * pr, 2), :, cols]
        return b.reshape(2 * BAND, b.shape[-1])

    d2 = DILATIONS[2]
    assert seq == d2 * BAND

    def body2(i, c):
        work = []
        for u in range(ATTN_BLOCKS_PER_STEP):
            cur = pl.ds(i * ATTN_BLOCKS_PER_STEP + u, BAND, stride=d2)
            work.append((cur, (q2[0, cur, :], k2[0, cur, :], None, v2[0, cur, :], None)))
        outs = [_attn_plane(*args, lambda pr: bias2(2, 0, pr, pl.ds(BAND, BAND)), None)
                for _, args in work]
        for (cur, _), res in zip(work, outs):
            store_state(cur, res)
        return c

    lax.fori_loop(0, d2 // ATTN_BLOCKS_PER_STEP, body2, 0)

    d1 = DILATIONS[1]
    nb1 = seq // (d1 * BAND)
    assert (d1 * nb1) % ATTN_BLOCKS_PER_STEP == 0 and d2 % ATTN_BLOCKS_PER_STEP == 0

    def body1(i, c):
        work = []
        for u in range(ATTN_BLOCKS_PER_STEP):
            idx = i * ATTN_BLOCKS_PER_STEP + u
            r = idx // nb1
            jb = idx % nb1
            st = r + jb * (d1 * BAND)
            sp = r + jnp.maximum(jb - 1, 0) * (d1 * BAND)
            first = jnp.where(jb == 0, 1, 0)
            cur = pl.ds(st, BAND, stride=d1)
            prv = pl.ds(sp, BAND, stride=d1)
            work.append((cur, first, (q1[0, cur, :], k1[0, cur, :], k1[0, prv, :],
                                      v1[0, cur, :], v1[0, prv, :]), load_state(cur)))
        outs = [_attn_plane(*args, lambda pr, first=first: bias2(1, first, pr), prev)
                for _, first, args, prev in work]
        for (cur, _, _, _), res in zip(work, outs):
            store_state(cur, res)
        return c

    lax.fori_loop(0, d1 * nb1 // ATTN_BLOCKS_PER_STEP, body1, 0)

    def body0(i, c):
        work = []
        for u in range(ATTN_BLOCKS_PER_STEP):
            jb = i * ATTN_BLOCKS_PER_STEP + u
            st = pl.multiple_of(jb * BAND, BAND)
            sp = pl.multiple_of(jnp.maximum(jb - 1, 0) * BAND, BAND)
            first = jnp.where(jb == 0, 1, 0)
            cur = pl.ds(st, BAND)
            work.append((cur, first, (q0[0, cur, :], k0[0, cur, :], k0[0, pl.ds(sp, BAND), :],
                                      v0[0, cur, :], v0[0, pl.ds(sp, BAND), :]), load_state(cur)))
        outs = [_attn_plane(*args, lambda pr, first=first: bias2(0, first, pr), prev)
                for _, first, args, prev in work]
        for (cur, _, _, _), res in zip(work, outs):
            o_ref[cur, :LANES] = res[0][0].astype(bf16)
            o_ref[cur, LANES:] = res[1][0].astype(bf16)
        return c

    lax.fori_loop(0, seq // BAND // ATTN_BLOCKS_PER_STEP, body0, 0)


def _attn_prompt(planes, bias_tab, batch, seq):
    n = batch * seq
    ppk = PLANES_PER_KIND

    def spec(g, kind):
        return pl.BlockSpec((1, seq, LANES), lambda b, p: ((g * 3 + kind) * ppk + p, b, 0))

    in_specs = [spec(g, kind) for g in range(N_DGROUPS) for kind in range(3)]
    in_specs.append(pl.BlockSpec((N_DGROUPS, 2, 4, BAND, 2 * BAND), lambda b, p: (0, 0, p, 0, 0)))
    return pl.pallas_call(
        functools.partial(_attn_kernel, seq=seq),
        grid=(batch, ppk),
        in_specs=in_specs,
        out_specs=pl.BlockSpec((seq, 2 * LANES), lambda b, p: (b, p)),
        out_shape=jax.ShapeDtypeStruct((n, HD), bf16),
        scratch_shapes=[pltpu.VMEM((seq, LANES), f32)] * 4,
        compiler_params=_cparams(("arbitrary", "arbitrary")),
        name="attn_prompt",
    )(*([planes] * 9), bias_tab)


SAMPLE_BLOCK_BYTES = 4 * 1024 * 1024


def _sample_kv_kernel(c_ref, q_ref, kn_ref, vn_ref, knc_ref, vnc_ref, bias_ref, b0_ref,
                      o_ref, m_ref, l_ref, acc_ref, *, length):
    k = c_ref[0, 0, 0]
    v = c_ref[0, 0, 1]
    hb = k.shape[0]
    q = q_ref[0, 0]
    kn = kn_ref[0, 0]
    vn = vn_ref[0, 0]
    rb = lambda a: a.astype(bf16).astype(f32)
    q8 = jnp.broadcast_to(q[:, None, :], (hb, SUBLANES, HEAD_DIM)).astype(bf16)
    s = jnp.einsum("hqe,hel->hql", q8, k.astype(bf16), preferred_element_type=f32)[:, 0:1, :]
    s = s + bias_ref[...]
    s0 = jnp.sum(rb(kn) * rb(q), axis=-1, keepdims=True)[:, :, None] + b0_ref[...]
    m = jnp.maximum(jnp.max(s, axis=2, keepdims=True), s0)
    p = jnp.exp(s - m)
    p0 = jnp.exp(s0 - m)
    l = jnp.sum(p, axis=2, keepdims=True) + p0
    p8 = jnp.broadcast_to(p, (hb, SUBLANES, length)).astype(bf16)
    acc = jnp.einsum("hql,hel->hqe", p8, v.astype(bf16), preferred_element_type=f32)[:, 0, :]
    acc = acc + rb(p0[:, 0, :]) * rb(vn)
    m_ref[0, 0] = m[:, 0, :]
    l_ref[0, 0] = l[:, 0, :]
    acc_ref[0, 0] = acc
    last = lax.broadcasted_iota(i32, k.shape, 2) == length - 1
    o_ref[0, 0, 0] = jnp.where(last, knc_ref[0], pltpu.roll(k, length - 1, 2))
    o_ref[0, 0, 1] = jnp.where(last, vnc_ref[0], pltpu.roll(v, length - 1, 2))


def _sample_kv_call(cache_t, q, kn, vn, bias, b0, nb):
    length = cache_t.shape[-1]
    hb = max(1, min(N_HEADS, SAMPLE_BLOCK_BYTES // (2 * HEAD_DIM * length * 4)))
    assert N_HEADS % hb == 0
    nh = N_HEADS // hb
    row = pl.BlockSpec((1, 1, hb, HEAD_DIM), lambda b, h: (b, h, 0, 0))
    col = pl.BlockSpec((1, hb, HEAD_DIM, 1), lambda b, h: (b, h, 0, 0))
    stat = pl.BlockSpec((1, 1, hb, 1), lambda b, h: (b, h, 0, 0))
    cspec = pl.BlockSpec((1, 1, 2, hb, HEAD_DIM, length), lambda b, h: (0, b, 0, h, 0, 0))
    grp = lambda a: a.reshape(nb, nh, hb, HEAD_DIM)
    new_t, m, l, acc = pl.pallas_call(
        functools.partial(_sample_kv_kernel, length=length),
        grid=(nb, nh),
        in_specs=[cspec, row, row, row, col, col,
                  pl.BlockSpec((hb, 1, length), lambda b, h: (h, 0, 0)),
                  pl.BlockSpec((hb, 1, 1), lambda b, h: (h, 0, 0))],
        out_specs=[cspec, stat, stat, row],
        out_shape=[jax.ShapeDtypeStruct(cache_t.shape, f32),
                   jax.ShapeDtypeStruct((nb, nh, hb, 1), f32),
                   jax.ShapeDtypeStruct((nb, nh, hb, 1), f32),
                   jax.ShapeDtypeStruct((nb, nh, hb, HEAD_DIM), f32)],
        compiler_params=_cparams(("arbitrary", "arbitrary")),
        name="sample_kv_group",
    )(cache_t, grp(q), grp(kn), grp(vn), kn[..., None], vn[..., None], bias, b0)
    return new_t, m.reshape(nb, N_HEADS, 1), l.reshape(nb, N_HEADS, 1), acc.reshape(nb, N_HEADS, HEAD_DIM)


def _sample_merge_kernel(m_ref, l_ref, acc_ref, o_ref):
    m = m_ref[...]
    mm = jnp.max(m, axis=0)
    w = jnp.exp(m - mm[None])
    num = jnp.sum(w * acc_ref[...], axis=0)
    den = jnp.sum(w * l_ref[...], axis=0)
    o_ref[...] = num / den


def _sample_merge(ms, ls, accs):
    return pl.pallas_call(
        _sample_merge_kernel,
        out_shape=jax.ShapeDtypeStruct(accs.shape[1:], f32),
        name="sample_merge",
    )(ms, ls, accs)


def _route(u, wr_ref, br_ref):
    lg = _dot(u.astype(bf16), wr_ref[...]) + br_ref[...]
    rows = u.shape[0]
    lane = lax.broadcasted_iota(i32, (rows, LANES), 1)
    lanef = lane.astype(f32)
    big = jnp.float32(1e9)
    isg = lane < N_EGROUPS
    lgg = jnp.where(isg, lg, NEG)
    mg = jnp.max(lgg, axis=1, keepdims=True)
    eg = jnp.where(isg, jnp.exp(lgg - mg), 0.0)
    pg = eg / jnp.sum(eg, axis=1, keepdims=True)
    gw = jnp.max(pg, axis=1, keepdims=True)
    gi = jnp.min(jnp.where(isg & (pg == gw), lanef, big), axis=1, keepdims=True)
    grp_of_lane = ((lane - N_EGROUPS) >> 2).astype(f32)
    sel = (lane >= N_EGROUPS) & (lane < N_EGROUPS + N_EXPERTS) & (grp_of_lane == gi)
    les = jnp.where(sel, lg, NEG)
    v1 = jnp.max(les, axis=1, keepdims=True)
    i1 = jnp.min(jnp.where(sel & (les == v1), lanef, big), axis=1, keepdims=True)
    sel2 = sel & (lanef != i1)
    les2 = jnp.where(sel2, lg, NEG)
    v2 = jnp.max(les2, axis=1, keepdims=True)
    i2 = jnp.min(jnp.where(sel2 & (les2 == v2), lanef, big), axis=1, keepdims=True)
    ex = jnp.exp(v2 - v1)
    den = 1.0 + ex
    pw1 = (1.0 / den) * gw
    pw2 = (ex / den) * gw
    e1 = i1 - N_EGROUPS
    e2 = i2 - N_EGROUPS
    first_low = e1 < e2
    lo = jnp.minimum(e1, e2) - 4.0 * gi
    hi = jnp.maximum(e1, e2) - 4.0 * gi
    pair = jnp.where(lo == 0.0, hi - 1.0, jnp.where(lo == 1.0, hi + 1.0, 5.0))
    cls = gi * 6.0 + pair
    g_lo = jnp.where(first_low, pw1, pw2)
    g_hi = jnp.where(first_low, pw2, pw1)
    zero = jnp.zeros((rows, LANES), f32)
    return jnp.where(lane == 0, cls, jnp.where(lane == 1, g_lo, jnp.where(lane == 2, g_hi, zero)))


def _write_slab(slab_ref, u, meta):
    rows = u.shape[0]
    half = D_MODEL // 2
    for c in range(4):
        w = _pack_words(u[:, c * LANES:(c + 1) * LANES], u[:, half + c * LANES:half + (c + 1) * LANES])
        slab_ref[pl.ds(c, rows, stride=SUBLANES), :] = w
    slab_ref[pl.ds(4, rows, stride=SUBLANES), :] = lax.bitcast_convert_type(meta, u32)
    z = jnp.zeros((rows, LANES), u32)
    for c in range(5, SUBLANES):
        slab_ref[pl.ds(c, rows, stride=SUBLANES), :] = z


def _ffn_prep(h, gn_ref, wr_ref, br_ref, slab_ref, metat_ref):
    u = _rms(h, gn_ref[...])
    meta = _route(u, wr_ref, br_ref)
    _write_slab(slab_ref, u, meta)
    metat_ref[...] = meta.T[:SUBLANES, :]


def _from_slab_f32(ref, rows):
    return jnp.concatenate([ref[pl.ds(c, rows, stride=SUBLANES), :] for c in range(SUBLANES)], axis=1)


def _wo_kernel(o_ref, x_ref, wo_ref, gn_ref, wr_ref, br_ref, h_ref, slab_ref, metat_ref):
    h = x_ref[...] + _dot(o_ref[...], wo_ref[...])
    h_ref[...] = h
    _ffn_prep(h, gn_ref, wr_ref, br_ref, slab_ref, metat_ref)


def _router_specs():
    cmap = lambda i: (0, 0)
    return [pl.BlockSpec((1, D_MODEL), cmap),
            pl.BlockSpec((D_MODEL, LANES), cmap),
            pl.BlockSpec((1, LANES), cmap)]


def _prep_out(n, tm):
    specs = [pl.BlockSpec((tm, D_MODEL), lambda i: (i, 0)),
             pl.BlockSpec((tm * SUBLANES, LANES), lambda i: (i, 0)),
             pl.BlockSpec((SUBLANES, tm), lambda i: (0, i))]
    shapes = [jax.ShapeDtypeStruct((n, D_MODEL), f32),
              jax.ShapeDtypeStruct((n * SUBLANES, LANES), u32),
              jax.ShapeDtypeStruct((SUBLANES, n), f32)]
    return specs, shapes


def _wo_call(o_bf, x2, wo_bf, router, tm):
    n = x2.shape[0]
    ospecs, oshapes = _prep_out(n, tm)
    return pl.pallas_call(
        _wo_kernel,
        grid=(n // tm,),
        in_specs=[pl.BlockSpec((tm, HD), lambda i: (i, 0)),
                  pl.BlockSpec((tm, D_MODEL), lambda i: (i, 0)),
                  pl.BlockSpec((HD, D_MODEL), lambda i: (0, 0))] + _router_specs(),
        out_specs=ospecs,
        out_shape=oshapes,
        compiler_params=_cparams(("arbitrary",)),
        name="wo_residual",
    )(o_bf, x2, wo_bf, *router)


def _rank_kernel(metat_ref, rank_ref, cnt_ref, carry_ref, *, tr):
    @pl.when(pl.program_id(0) == 0)
    def _():
        carry_ref[...] = jnp.zeros_like(carry_ref)

    cls = metat_ref[0:1, :]
    sub = lax.broadcasted_iota(i32, (32, tr), 0).astype(f32)
    oh = jnp.where(sub == cls, 1.0, 0.0)
    rr = lax.broadcasted_iota(i32, (tr, tr), 0)
    cc = lax.broadcasted_iota(i32, (tr, tr), 1)
    upper = jnp.where(rr < cc, 1.0, 0.0).astype(bf16)
    pre = _dot(oh.astype(bf16), upper)
    carry = carry_ref[:, 0:1]
    rank_ref[...] = jnp.sum(oh * (pre + carry), axis=0, keepdims=True).astype(i32)
    newc = carry + jnp.sum(oh, axis=1, keepdims=True)
    carry_ref[...] = jnp.broadcast_to(newc, carry_ref.shape)
    cnt_ref[...] = carry_ref[...]


def _rank_call(metat, tr):
    n = metat.shape[1]
    return pl.pallas_call(
        functools.partial(_rank_kernel, tr=tr),
        grid=(n // tr,),
        in_specs=[pl.BlockSpec((SUBLANES, tr), lambda i: (0, i))],
        out_specs=[pl.BlockSpec((1, tr), lambda i: (0, i)),
                   pl.BlockSpec((32, LANES), lambda i: (0, 0))],
        out_shape=[jax.ShapeDtypeStruct((1, n), i32), jax.ShapeDtypeStruct((32, LANES), f32)],
        scratch_shapes=[pltpu.VMEM((32, LANES), f32)],
        compiler_params=_cparams(("arbitrary",)),
        name="class_rank",
    )(metat)


ROWS_PER_GATHER_STEP = 256


GATHER_UNROLL = 8


def _gather_kernel(idx_ref, src_ref, o_ref, sem, *, rows):
    base = pl.program_id(0) * rows

    def issue(kk, c):
        for u in range(GATHER_UNROLL):
            k = kk * GATHER_UNROLL + u
            t = idx_ref[base + k]
            pltpu.make_async_copy(
                src_ref.at[pl.ds(pl.multiple_of(t * SUBLANES, SUBLANES), SUBLANES)],
                o_ref.at[pl.ds(pl.multiple_of(k * SUBLANES, SUBLANES), SUBLANES)], sem).start()
        return c

    lax.fori_loop(0, rows // GATHER_UNROLL, issue, 0)
    pltpu.make_async_copy(src_ref.at[pl.ds(0, rows * SUBLANES)], o_ref, sem).wait()


def _gather_slabs(idx, src):
    n_out = idx.shape[0]
    rows = math.gcd(ROWS_PER_GATHER_STEP, n_out)
    assert rows % GATHER_UNROLL == 0 and src.shape[0] >= rows * SUBLANES
    return pl.pallas_call(
        functools.partial(_gather_kernel, rows=rows),
        grid_spec=pltpu.PrefetchScalarGridSpec(
            num_scalar_prefetch=1, grid=(n_out // rows,),
            in_specs=[pl.BlockSpec(memory_space=pl.ANY)],
            out_specs=pl.BlockSpec((rows * SUBLANES, LANES), lambda i, idx: (i, 0)),
            scratch_shapes=[pltpu.SemaphoreType.DMA(())]),
        out_shape=jax.ShapeDtypeStruct((n_out * SUBLANES, LANES), src.dtype),
        compiler_params=_cparams(("arbitrary",)),
        name="gather_slabs",
    )(idx, src)


def _start_slab_gather(idx_ref, base, chunks, src_ref, buf, sl, sems):
    def issue(kk, c):
        for u in range(GATHER_UNROLL):
            k = kk * GATHER_UNROLL + u
            t = idx_ref[base + k]
            pltpu.make_async_copy(
                src_ref.at[pl.ds(pl.multiple_of(t * SUBLANES, SUBLANES), SUBLANES)],
                buf.at[sl, pl.ds(pl.multiple_of(k * SUBLANES, SUBLANES), SUBLANES)],
                sems.at[sl]).start()
        return c

    lax.fori_loop(0, chunks, issue, 0)


def _wait_slab_gather(chunks, src_ref, buf, sl, sems):
    rows = GATHER_UNROLL * SUBLANES

    def drain(kk, c):
        pltpu.make_async_copy(src_ref.at[pl.ds(0, rows)], buf.at[sl, pl.ds(0, rows)], sems.at[sl]).wait()
        return c

    lax.fori_loop(0, chunks, drain, 0)


def _prefetched_slabs(idx_ref, src_ref, buf, sems, rows):
    i = pl.program_id(0)
    slot = i % 2

    chunks = rows // GATHER_UNROLL

    @pl.when(i == 0)
    def _():
        _start_slab_gather(idx_ref, 0, chunks, src_ref, buf, slot, sems)

    @pl.when(i + 1 < pl.num_programs(0))
    def _():
        _start_slab_gather(idx_ref, (i + 1) * rows, chunks, src_ref, buf, 1 - slot, sems)

    _wait_slab_gather(chunks, src_ref, buf, slot, sems)
    return buf.at[slot]


def _expert_kernel(e1_ref, e2_ref, chunks_ref, src_ref, slab_ref, wg1, wu1, wd1, wg2, wu2, wd2,
                   y_ref, xbuf, sems, *, te):
    del e1_ref, e2_ref
    i = pl.program_id(0)
    nt = pl.num_programs(0)
    slot = i % 2
    chunks = chunks_ref[i]
    nxt = jnp.minimum(i + 1, nt - 1)

    @pl.when(i == 0)
    def _():
        xbuf[...] = jnp.zeros_like(xbuf)
        _start_slab_gather(src_ref, 0, chunks, slab_ref, xbuf, slot, sems)

    @pl.when(i + 1 < nt)
    def _():
        _start_slab_gather(src_ref, (i + 1) * te, chunks_ref[nxt], slab_ref, xbuf, 1 - slot, sems)

    @pl.when(chunks == 0)
    def _():
        y_ref[...] = jnp.zeros_like(y_ref)

    @pl.when(chunks > 0)
    def _():
        _wait_slab_gather(chunks, slab_ref, xbuf, slot, sems)
        xs = xbuf.at[slot]
        los, his = [], []
        for c in range(4):
            lo, hi = _unpack_words(xs[pl.ds(c, te, stride=SUBLANES), :])
            los.append(lo)
            his.append(hi)
        x = jnp.concatenate(los + his, axis=1).astype(bf16)
        meta = lax.bitcast_convert_type(xs[pl.ds(4, te, stride=SUBLANES), :], f32)
        y = None
        for wg, wu, wd, lane in ((wg1, wu1, wd1, 1), (wg2, wu2, wd2, 2)):
            a = _dot(x, wg[0])
            c = _dot(x, wu[0])
            hh = (a * (1.0 / (1.0 + jnp.exp(-a)))) * c * meta[:, lane:lane + 1]
            part = _dot(hh.astype(bf16), wd[0])
            y = part if y is None else y + part
        for c in range(SUBLANES):
            y_ref[pl.ds(c, te, stride=SUBLANES), :] = y[:, c * LANES:(c + 1) * LANES]


def _expert_call(te1, te2, tchunks, src, slab, wg, wu, wd, te):
    ntiles = te1.shape[0]
    assert te % GATHER_UNROLL == 0 and slab.shape[0] >= te * SUBLANES

    def wspec(shape, which):
        if which == 1:
            return pl.BlockSpec(shape, lambda i, e1, e2, v, s: (e1[i], 0, 0))
        return pl.BlockSpec(shape, lambda i, e1, e2, v, s: (e2[i], 0, 0))

    gs = (1, D_MODEL, D_EXPERT)
    ds_ = (1, D_EXPERT, D_MODEL)
    return pl.pallas_call(
        functools.partial(_expert_kernel, te=te),
        grid_spec=pltpu.PrefetchScalarGridSpec(
            num_scalar_prefetch=4, grid=(ntiles,),
            in_specs=[pl.BlockSpec(memory_space=pl.ANY),
                      wspec(gs, 1), wspec(gs, 1), wspec(ds_, 1),
                      wspec(gs, 2), wspec(gs, 2), wspec(ds_, 2)],
            out_specs=pl.BlockSpec((te * SUBLANES, LANES), lambda i, e1, e2, v, s: (i, 0)),
            scratch_shapes=[pltpu.VMEM((2, te * SUBLANES, LANES), u32),
                            pltpu.SemaphoreType.DMA((2,))]),
        out_shape=jax.ShapeDtypeStruct((ntiles * te * SUBLANES, LANES), f32),
        compiler_params=_cparams(("arbitrary",)),
        name="expert_pair_ffn",
    )(te1, te2, tchunks, src, slab, wg, wu, wd, wg, wu, wd)


def _moe(slab, metat, wg, wu, wd, layer, te, tr):
    n = metat.shape[1]
    rank, cnt = _rank_call(metat, tr)
    counts = cnt[:N_CLASSES, 0].astype(i32)
    padded = ((counts + te - 1) // te) * te
    ends = jnp.cumsum(padded)
    off = ends - padded
    cls = metat[0].astype(i32)
    class_ids = jnp.arange(N_CLASSES, dtype=i32)
    pos = rank[0] + jnp.sum(jnp.where(cls[:, None] == class_ids[None, :], off[None, :], 0), axis=1)
    ntiles = -(-n // te) + N_CLASSES
    tstart = jnp.arange(ntiles, dtype=i32) * te
    tcls = jnp.minimum(jnp.sum((tstart[:, None] >= ends[None, :]).astype(i32), axis=1), N_CLASSES - 1)
    real_end = (off + counts)[tcls]
    tchunks = jnp.where(tstart < ends[-1], (jnp.clip(real_end - tstart, 0, te) + GATHER_UNROLL - 1) // GATHER_UNROLL, 0)
    tchunks = tchunks.astype(i32)
    te1 = jnp.asarray(_CLASS_E1 + layer * N_EXPERTS)[tcls]
    te2 = jnp.asarray(_CLASS_E2 + layer * N_EXPERTS)[tcls]
    src = jnp.zeros((ntiles * te,), i32).at[pos].set(jnp.arange(n, dtype=i32), unique_indices=True)
    return pos, _expert_call(te1, te2, tchunks, src, slab, wg, wu, wd, te)


def _pool_mix_tail(hf, up, ext, tpos0, pw_ref, pb_ref, ps_ref):
    rows = up.shape[0]
    tpos = tpos0 + lax.broadcasted_iota(i32, (rows, 1), 0)
    parts = []
    for gi, win in enumerate(POOL_WINDOWS):
        sl = slice(gi * POOL_CH, (gi + 1) * POOL_CH)
        s = ext[:, sl]
        for kk in range(gi + 1):
            s = s + pltpu.roll(s, 1 << kk, 0)
        cnt = jnp.minimum(win, tpos + 1).astype(f32)
        z = s[16:, :] / cnt - up[:, sl]
        parts.append(_dot(z.astype(bf16), pw_ref[gi]))
    y = (jnp.concatenate(parts, axis=1) + pb_ref[...]) * ps_ref[...]
    return hf + y


def _pool_kernel(pos_ref, h_ref, ys_ref, pn_ref, pw_ref, pb_ref, ps_ref, gn_ref, wr_ref, br_ref,
                 h2_ref, slab_ref, metat_ref, st_ref, ybuf, sems, halo_ref, *, tm, tps):
    r = pl.program_id(0) % tps

    @pl.when(pl.program_id(0) == 0)
    def _():
        halo_ref[...] = jnp.zeros_like(halo_ref)

    y = _prefetched_slabs(pos_ref, ys_ref, ybuf, sems, tm)
    hf = h_ref[...] + _from_slab_f32(y, tm)
    up = _rms(hf, pn_ref[...])
    uph = jnp.where(r > 0, halo_ref[...], 0.0)
    ext = jnp.concatenate([uph, up], axis=0)
    h2 = _pool_mix_tail(hf, up, ext, r * tm, pw_ref, pb_ref, ps_ref)
    h2_ref[...] = h2
    st_ref[...] = up[tm - 16:, :]
    halo_ref[...] = up[tm - 16:, :]
    _ffn_prep(h2, gn_ref, wr_ref, br_ref, slab_ref, metat_ref)


def _pool_call(h, pos, ys, pn, pw_bf, pb, psc, router, batch, seq, tm):
    n = h.shape[0]
    tps = seq // tm
    assert tm % GATHER_UNROLL == 0 and ys.shape[0] >= tm * SUBLANES
    c2 = lambda i, p: (0, 0)
    return pl.pallas_call(
        functools.partial(_pool_kernel, tm=tm, tps=tps),
        grid_spec=pltpu.PrefetchScalarGridSpec(
            num_scalar_prefetch=1, grid=(n // tm,),
            in_specs=[pl.BlockSpec((tm, D_MODEL), lambda i, p: (i, 0)),
                      pl.BlockSpec(memory_space=pl.ANY),
                      pl.BlockSpec((1, D_MODEL), c2),
                      pl.BlockSpec((4, POOL_CH, POOL_CH), lambda i, p: (0, 0, 0)),
                      pl.BlockSpec((1, D_MODEL), c2),
                      pl.BlockSpec((1, D_MODEL), c2),
                      pl.BlockSpec((1, D_MODEL), c2),
                      pl.BlockSpec((D_MODEL, LANES), c2),
                      pl.BlockSpec((1, LANES), c2)],
            out_specs=[pl.BlockSpec((tm, D_MODEL), lambda i, p: (i, 0)),
                       pl.BlockSpec((tm * SUBLANES, LANES), lambda i, p: (i, 0)),
                       pl.BlockSpec((SUBLANES, tm), lambda i, p: (0, i)),
                       pl.BlockSpec((16, D_MODEL), lambda i, p: (i // tps, 0))],
            scratch_shapes=[pltpu.VMEM((2, tm * SUBLANES, LANES), f32),
                            pltpu.SemaphoreType.DMA((2,)),
                            pltpu.VMEM((16, D_MODEL), f32)]),
        out_shape=[jax.ShapeDtypeStruct((n, D_MODEL), f32),
                   jax.ShapeDtypeStruct((n * SUBLANES, LANES), u32),
                   jax.ShapeDtypeStruct((SUBLANES, n), f32),
                   jax.ShapeDtypeStruct((batch * 16, D_MODEL), f32)],
        compiler_params=_cparams(("arbitrary",)),
        name="pool_mixer",
    )(pos, h, ys, pn, pw_bf, pb, psc, *router)


def _pool_sample_kernel(h_ref, y_ref, st_ref, pn_ref, pw_ref, pb_ref, ps_ref,
                        gn_ref, wr_ref, br_ref, h2_ref, slab_ref, metat_ref, nst_ref, *, nb):
    rows = h_ref.shape[0]
    hf = h_ref[...] + _from_slab_f32(y_ref, rows)
    up = _rms(hf, pn_ref[...])
    st = st_ref[...]
    ub = up[:nb, :]
    parts = []
    for gi, win in enumerate(POOL_WINDOWS):
        sl = slice(gi * POOL_CH, (gi + 1) * POOL_CH)
        tot = ub[:, sl] + jnp.sum(st[:, POOL_STATE - (win - 1):, sl], axis=1)
        z = tot / float(win) - ub[:, sl]
        parts.append(_dot(z.astype(bf16), pw_ref[gi]))
    y = (jnp.concatenate(parts, axis=1) + pb_ref[...]) * ps_ref[...]
    if rows > nb:
        y = jnp.concatenate([y, jnp.zeros((rows - nb, D_MODEL), f32)], axis=0)
    h2 = hf + y
    h2_ref[...] = h2
    nst_ref[:, :POOL_STATE - 1, :] = st[:, 1:, :]
    nst_ref[:, POOL_STATE - 1, :] = ub
    _ffn_prep(h2, gn_ref, wr_ref, br_ref, slab_ref, metat_ref)


def _pool_sample_call(h, ytt, state, pn, pw_bf, pb, psc, router, nb):
    rows = h.shape[0]
    ospecs, oshapes = _prep_out(rows, rows)
    ospecs.append(pl.BlockSpec((nb, POOL_STATE, D_MODEL), lambda i: (0, 0, 0)))
    oshapes.append(jax.ShapeDtypeStruct((nb, POOL_STATE, D_MODEL), f32))
    cm = lambda i: (0, 0)
    return pl.pallas_call(
        functools.partial(_pool_sample_kernel, nb=nb),
        grid=(1,),
        in_specs=[pl.BlockSpec((rows, D_MODEL), cm),
                  pl.BlockSpec((rows * SUBLANES, LANES), cm),
                  pl.BlockSpec((nb, POOL_STATE, D_MODEL), lambda i: (0, 0, 0)),
                  pl.BlockSpec((1, D_MODEL), cm),
                  pl.BlockSpec((4, POOL_CH, POOL_CH), lambda i: (0, 0, 0)),
                  pl.BlockSpec((1, D_MODEL), cm),
                  pl.BlockSpec((1, D_MODEL), cm)] + _router_specs(),
        out_specs=ospecs,
        out_shape=oshapes,
        compiler_params=_cparams(("arbitrary",)),
        name="pool_mixer_sample",
    )(h, ytt, state, pn, pw_bf, pb, psc, *router)


def _final_kernel(h_ref, y_ref, o_ref, *, tm):
    o_ref[...] = h_ref[...] + _from_slab_f32(y_ref, tm)


def _final_call(h, ytt, tm):
    n = h.shape[0]
    return pl.pallas_call(
        functools.partial(_final_kernel, tm=tm),
        grid=(n // tm,),
        in_specs=[pl.BlockSpec((tm, D_MODEL), lambda i: (i, 0)),
                  pl.BlockSpec((tm * SUBLANES, LANES), lambda i: (i, 0))],
        out_specs=pl.BlockSpec((tm, D_MODEL), lambda i: (i, 0)),
        out_shape=jax.ShapeDtypeStruct((n, D_MODEL), f32),
        compiler_params=_cparams(("arbitrary",)),
        name="final_residual",
    )(h, ytt)


def _final_gather_kernel(pos_ref, h_ref, ys_ref, o_ref, ybuf, sems, *, tm):
    y = _prefetched_slabs(pos_ref, ys_ref, ybuf, sems, tm)
    o_ref[...] = h_ref[...] + _from_slab_f32(y, tm)


def _final_gather_call(h, pos, ys, tm):
    n = h.shape[0]
    assert tm % GATHER_UNROLL == 0 and ys.shape[0] >= tm * SUBLANES
    return pl.pallas_call(
        functools.partial(_final_gather_kernel, tm=tm),
        grid_spec=pltpu.PrefetchScalarGridSpec(
            num_scalar_prefetch=1, grid=(n // tm,),
            in_specs=[pl.BlockSpec((tm, D_MODEL), lambda i, p: (i, 0)),
                      pl.BlockSpec(memory_space=pl.ANY)],
            out_specs=pl.BlockSpec((tm, D_MODEL), lambda i, p: (i, 0)),
            scratch_shapes=[pltpu.VMEM((2, tm * SUBLANES, LANES), f32),
                            pltpu.SemaphoreType.DMA((2,))]),
        out_shape=jax.ShapeDtypeStruct((n, D_MODEL), f32),
        compiler_params=_cparams(("arbitrary",)),
        name="final_residual_gather",
    )(pos, h, ys)


def _t5_bucket(dist):
    max_exact = N_BUCKETS // 2
    df = jnp.maximum(dist, max_exact).astype(f32)
    large = max_exact + (jnp.log(df / max_exact) / math.log(MAX_DISTANCE / max_exact)
                         * (N_BUCKETS - max_exact)).astype(i32)
    large = jnp.minimum(large, N_BUCKETS - 1)
    return jnp.where(dist < max_exact, dist, large)


def _prompt_bias_tables(rel_bias):
    qi = jnp.arange(BAND)[:, None]
    kj = jnp.arange(2 * BAND)[None, :]
    step = BAND + qi - kj
    tabs = []
    for g, (win, dil) in enumerate(zip(WINDOWS, DILATIONS)):
        n_step = win // dil
        tab = rel_bias[:, g * N_HEADS:(g + 1) * N_HEADS].astype(f32)
        svals = jnp.maximum(2 * BAND - 1 - jnp.arange(3 * BAND), 0)
        bvec = tab[_t5_bucket(svals * dil)].T
        bias = jnp.stack([bvec[:, BAND - 1 - q:3 * BAND - 1 - q] for q in range(BAND)], axis=1)
        band = (step >= 0) & (step <= n_step)
        reg = jnp.where(band[None], bias, NEG)
        first = jnp.where((band & (kj >= BAND))[None], bias, NEG)
        tabs.append(jnp.stack([reg, first]))
    return jnp.stack(tabs)


def _sample_bias_tables(rel_bias, lengths):
    bs, b0 = [], []
    for g, (win, dil) in enumerate(zip(WINDOWS, DILATIONS)):
        length = lengths[g]
        n_step = win // dil
        tab = rel_bias[:, g * N_HEADS:(g + 1) * N_HEADS].astype(f32)
        dist = length - jnp.arange(length)
        hit = (dist % dil == 0) & (dist <= n_step * dil)
        b = jnp.where(hit[:, None], tab[_t5_bucket(dist)], NEG)
        bs.append(b.T[:, None, :])
        b0.append(tab[_t5_bucket(jnp.zeros((1,), i32))].T[:, :, None])
    return bs, b0


def _router_weights(ffn_norm_i, wgr, bgr, wer, ber):
    w = jnp.zeros((D_MODEL, LANES), f32)
    w = w.at[:, :N_EGROUPS].set(wgr).at[:, N_EGROUPS:N_EGROUPS + N_EXPERTS].set(wer)
    b = jnp.zeros((1, LANES), f32)
    b = b.at[0, :N_EGROUPS].set(bgr).at[0, N_EGROUPS:N_EGROUPS + N_EXPERTS].set(ber)
    return ffn_norm_i.reshape(1, D_MODEL), w.astype(bf16), b


def _norm_mats():
    head_of_col = np.arange(HD) // HEAD_DIM
    s = (head_of_col[:, None] == np.arange(LANES)[None, :]).astype(np.float32)
    e = s.T
    return jnp.asarray(s, bf16), jnp.asarray(np.concatenate([e, e], axis=0), bf16)


def kernel(x_prompt, x_sample, cache_kv_w128, cache_kv_w512, cache_kv_w2048, state_pool, w_qkv, q_norm, k_norm, w_o, rel_bias, attn_norm, pool_norm, pool_w, pool_b, pool_scale, ffn_norm, router_group_w, router_group_b, router_expert_w, router_expert_b, w_gate, w_up, w_down):
    batch, seq, _ = x_prompt.shape
    nb = x_sample.shape[0]
    assert x_sample.shape[1] == 1
    n = batch * seq
    srows = LANES
    caches6 = (cache_kv_w128, cache_kv_w512, cache_kv_w2048)

    smat, e2mat = _norm_mats()
    wqkv_bf = w_qkv[0].astype(bf16)
    wo_bf = w_o[0].astype(bf16)
    gq = jnp.tile(q_norm[0], (1, N_HEADS)).reshape(N_DGROUPS, 1, HD)
    gk = jnp.tile(k_norm[0], (1, N_HEADS)).reshape(N_DGROUPS, 1, HD)
    an = attn_norm[0].reshape(1, D_MODEL)
    routers = [_router_weights(ffn_norm[i], router_group_w[i], router_group_b[i],
                               router_expert_w[i], router_expert_b[i]) for i in range(2)]
    allx = lambda w: w.astype(bf16).reshape((-1,) + w.shape[2:])
    wg_bf, wu_bf, wd_bf = allx(w_gate), allx(w_up), allx(w_down)
    pw_bf = pool_w[0].astype(bf16)
    pn = pool_norm[0].reshape(1, D_MODEL)
    pb = pool_b[0].reshape(1, D_MODEL)
    psc = pool_scale[0].reshape(1, D_MODEL)

    x2 = x_prompt.reshape(n, D_MODEL)
    planes, kv2, kv1, kv0 = _qkv_prompt(x2, an, wqkv_bf, gq, gk, smat, e2mat, batch, seq)
    o_bf = _attn_prompt(planes, _prompt_bias_tables(rel_bias), batch, seq)
    h0, slab0, metat0 = _wo_call(o_bf, x2, wo_bf, routers[0], tm=512)
    pos0, ys0 = _moe(slab0, metat0, wg_bf, wu_bf, wd_bf, 0, te=256, tr=512)
    h1, slab1, metat1, pst = _pool_call(h0, pos0, ys0, pn, pw_bf, pb, psc, routers[1], batch, seq, tm=512)
    pos1, ys1 = _moe(slab1, metat1, wg_bf, wu_bf, wd_bf, 1, te=256, tr=512)
    y_prompt = _final_gather_call(h1, pos1, ys1, tm=512).reshape(batch, seq, D_MODEL)

    new_kv_p = [jnp.transpose(t, (0, 1, 5, 2, 3, 4)) for t in (kv0, kv1, kv2)]
    new_pool_p = pst.reshape(batch, 16, D_MODEL)[:, 16 - POOL_STATE:][None]

    xs = jnp.zeros((srows, D_MODEL), f32).at[:nb].set(x_sample[:, 0, :])
    qs, ks, vs = _qkv_sample(xs, an, wqkv_bf, gq, gk, smat, e2mat)
    to4 = lambda a: a[:, :nb].reshape(N_DGROUPS, nb, N_HEADS, HEAD_DIM)
    qs4, ks4, vs4 = to4(qs), to4(ks), to4(vs)
    bias_s, bias_s0 = _sample_bias_tables(rel_bias, [c.shape[2] for c in caches6])
    new_kv_s, ms, ls, accs = [], [], [], []
    for g in range(N_DGROUPS):
        cache_t = jnp.transpose(caches6[g], (0, 1, 3, 4, 5, 2))
        new_t, m_g, l_g, acc_g = _sample_kv_call(cache_t, qs4[g], ks4[g], vs4[g], bias_s[g], bias_s0[g], nb)
        new_kv_s.append(jnp.transpose(new_t, (0, 1, 5, 2, 3, 4)))
        ms.append(m_g)
        ls.append(l_g)
        accs.append(acc_g)
    os_ = _sample_merge(jnp.stack(ms), jnp.stack(ls), jnp.stack(accs))
    os_pad = jnp.zeros((srows, HD), bf16).at[:nb].set(os_.reshape(nb, HD).astype(bf16))
    hs0, sslab0, smetat0 = _wo_call(os_pad, xs, wo_bf, routers[0], tm=srows)
    ystt0 = _gather_slabs(*_moe(sslab0, smetat0, wg_bf, wu_bf, wd_bf, 0, te=64, tr=srows))
    hs1, sslab1, smetat1, new_pool_s = _pool_sample_call(
        hs0, ystt0, state_pool[0], pn, pw_bf, pb, psc, routers[1], nb)
    ystt1 = _gather_slabs(*_moe(sslab1, smetat1, wg_bf, wu_bf, wd_bf, 1, te=64, tr=srows))
    y_sample = _final_call(hs1, ystt1, tm=srows)[:nb].reshape(nb, 1, D_MODEL)

    return (y_prompt, y_sample, new_kv_p[0], new_kv_p[1], new_kv_p[2], new_pool_p,
            new_kv_s[0], new_kv_s[1], new_kv_s[2], new_pool_s[None])
```

```python
import functools
import math

import numpy as np
import jax
import jax.numpy as jnp
from jax import lax
from jax.experimental import pallas as pl
from jax.experimental.pallas import tpu as pltpu

f32 = jnp.float32
bf16 = jnp.bfloat16
u32 = jnp.uint32
i32 = jnp.int32

D_MODEL = 1024
N_HEADS = 16
HEAD_DIM = 64
HD = N_HEADS * HEAD_DIM
N_DGROUPS = 3
WINDOWS = (128, 512, 2048)
DILATIONS = (1, 4, 16)
BAND = 128
N_BUCKETS = 32
MAX_DISTANCE = 2048
POOL_WINDOWS = (2, 4, 8, 16)
POOL_CH = D_MODEL // 4
POOL_STATE = max(POOL_WINDOWS) - 1
N_EGROUPS = 4
EXPERTS_PER_GROUP = 4
N_EXPERTS = 16
D_EXPERT = D_MODEL // 2
RMS_EPS = 1e-6
NEG = -1e30

LANES = 128
SUBLANES = 8
HI_MASK = 0xFFFF0000
N_CLASSES = N_EGROUPS * 6
PLANES_PER_KIND = HD // (2 * LANES)
VMEM_LIMIT = 56 * 1024 * 1024

_PAIRS = [(a, b) for a in range(4) for b in range(a + 1, 4)]
_CLASS_E1 = np.array([g * 4 + _PAIRS[p][0] for g in range(4) for p in range(6)], np.int32)
_CLASS_E2 = np.array([g * 4 + _PAIRS[p][1] for g in range(4) for p in range(6)], np.int32)


def _cparams(sem=None, vmem=VMEM_LIMIT):
    kw = dict(vmem_limit_bytes=vmem)
    if sem is not None:
        kw["dimension_semantics"] = sem
    return pltpu.CompilerParams(**kw)


def _dot(a, b):
    return jnp.dot(a, b, preferred_element_type=f32)


def _rms(x, g):
    return x * lax.rsqrt(jnp.mean(x * x, axis=-1, keepdims=True) + RMS_EPS) * g


def _pack_words(lo, hi):
    lb = lax.bitcast_convert_type(lo.astype(bf16).astype(f32), u32) >> 16
    hb = lax.bitcast_convert_type(hi.astype(bf16).astype(f32), u32) & jnp.uint32(HI_MASK)
    return lb | hb


def _unpack_words(w):
    lo = lax.bitcast_convert_type(w << 16, f32)
    hi = lax.bitcast_convert_type(w & jnp.uint32(HI_MASK), f32)
    return lo, hi


def _head_norm(z, gain, s_ref, e2_ref):
    zz = z * z
    zz_hi = zz.astype(bf16)
    zz_lo = (zz - zz_hi.astype(f32)).astype(bf16)
    ss = _dot(zz_hi, s_ref[...]) + _dot(zz_lo, s_ref[...])
    r = lax.rsqrt(ss * (1.0 / HEAD_DIM) + RMS_EPS)
    r_hi = r.astype(bf16)
    r_lo = (r - r_hi.astype(f32)).astype(bf16)
    rb = _dot(jnp.concatenate([r_hi, r_lo], axis=1), e2_ref[...])
    return z * rb * gain


def _qkv_kernel(x_ref, gn_ref, w_ref, gq_ref, gk_ref, s_ref, e2_ref,
                planes_ref, kv2_ref, kv1_ref, kv0_ref, *, tm, tps, t512):
    j = pl.program_id(0)
    r = pl.program_id(1) % tps
    u = _rms(x_ref[...], gn_ref[...]).astype(bf16)
    q = _head_norm(_dot(u, w_ref[:, :HD]), gq_ref[0], s_ref, e2_ref) * (HEAD_DIM ** -0.5)
    k = _head_norm(_dot(u, w_ref[:, HD:2 * HD]), gk_ref[0], s_ref, e2_ref)
    v = _dot(u, w_ref[:, 2 * HD:])
    for kind, arr in enumerate((q, k, v)):
        for p in range(PLANES_PER_KIND):
            c0 = 2 * LANES * p
            planes_ref[kind * PLANES_PER_KIND + p] = _pack_words(
                arr[:, c0:c0 + LANES], arr[:, c0 + LANES:c0 + 2 * LANES])

    def store_t(ref, kk, vv):
        ref[0, 0, 0] = kk.T.reshape(N_HEADS, HEAD_DIM, kk.shape[0])
        ref[0, 0, 1] = vv.T.reshape(N_HEADS, HEAD_DIM, vv.shape[0])

    @pl.when(j == 2)
    def _():
        store_t(kv2_ref, k, v)

    @pl.when((j == 1) & (r >= tps - t512))
    def _():
        store_t(kv1_ref, k, v)

    @pl.when((j == 0) & (r == tps - 1))
    def _():
        store_t(kv0_ref, k[tm - BAND:, :], v[tm - BAND:, :])


def _qkv_prompt(x2, gn, w_bf, gq, gk, smat, e2mat, batch, seq, tm=256):
    n = x2.shape[0]
    nt = n // tm
    tps = seq // tm
    t512 = WINDOWS[1] // tm
    assert seq % tm == 0 and WINDOWS[1] % tm == 0 and tm >= BAND and seq == WINDOWS[2]

    def kv2_map(j, i):
        on = j == 2
        return (0, jnp.where(on, i // tps, 0), 0, 0, 0, jnp.where(on, i % tps, 0))

    def kv1_map(j, i):
        inside = jnp.maximum(i % tps - (tps - t512), 0)
        b = jnp.where(j == 1, i // tps, jnp.where(j < 1, 0, batch - 1))
        t = jnp.where(j == 1, inside, jnp.where(j < 1, 0, t512 - 1))
        return (0, b, 0, 0, 0, t)

    def kv0_map(j, i):
        return (0, jnp.where(j == 0, i // tps, batch - 1), 0, 0, 0, 0)

    nplanes = 3 * PLANES_PER_KIND
    kvt = lambda keep: (1, batch, 2, N_HEADS, HEAD_DIM, keep)
    kvblock = lambda cols: (1, 1, 2, N_HEADS, HEAD_DIM, cols)
    return pl.pallas_call(
        functools.partial(_qkv_kernel, tm=tm, tps=tps, t512=t512),
        grid=(N_DGROUPS, nt),
        in_specs=[
            pl.BlockSpec((tm, D_MODEL), lambda j, i: (i, 0)),
            pl.BlockSpec((1, D_MODEL), lambda j, i: (0, 0)),
            pl.BlockSpec((D_MODEL, 3 * HD), lambda j, i: (0, j)),
            pl.BlockSpec((1, 1, HD), lambda j, i: (j, 0, 0)),
            pl.BlockSpec((1, 1, HD), lambda j, i: (j, 0, 0)),
            pl.BlockSpec((HD, LANES), lambda j, i: (0, 0)),
            pl.BlockSpec((2 * LANES, HD), lambda j, i: (0, 0)),
        ],
        out_specs=[
            pl.BlockSpec((nplanes, tm, LANES), lambda j, i: (j, i, 0)),
            pl.BlockSpec(kvblock(tm), kv2_map),
            pl.BlockSpec(kvblock(tm), kv1_map),
            pl.BlockSpec(kvblock(BAND), kv0_map),
        ],
        out_shape=[
            jax.ShapeDtypeStruct((N_DGROUPS * nplanes, n, LANES), u32),
            jax.ShapeDtypeStruct(kvt(WINDOWS[2]), f32),
            jax.ShapeDtypeStruct(kvt(WINDOWS[1]), f32),
            jax.ShapeDtypeStruct(kvt(WINDOWS[0]), f32),
        ],
        compiler_params=_cparams(("arbitrary", "arbitrary")),
        name="qkv_prompt",
    )(x2, gn, w_bf, gq, gk, smat, e2mat)


def _qkv_sample_kernel(x_ref, gn_ref, w_ref, gq_ref, gk_ref, s_ref, e2_ref, q_ref, k_ref, v_ref):
    u = _rms(x_ref[...], gn_ref[...]).astype(bf16)
    z = _dot(u, w_ref[...])
    q_ref[0] = _head_norm(z[:, :HD], gq_ref[0], s_ref, e2_ref) * (HEAD_DIM ** -0.5)
    k_ref[0] = _head_norm(z[:, HD:2 * HD], gk_ref[0], s_ref, e2_ref)
    v_ref[0] = z[:, 2 * HD:]


def _qkv_sample(xs, gn, w_bf, gq, gk, smat, e2mat):
    rows = xs.shape[0]
    out = jax.ShapeDtypeStruct((N_DGROUPS, rows, HD), f32)
    ospec = pl.BlockSpec((1, rows, HD), lambda j: (j, 0, 0))
    return pl.pallas_call(
        _qkv_sample_kernel,
        grid=(N_DGROUPS,),
        in_specs=[
            pl.BlockSpec((rows, D_MODEL), lambda j: (0, 0)),
            pl.BlockSpec((1, D_MODEL), lambda j: (0, 0)),
            pl.BlockSpec((D_MODEL, 3 * HD), lambda j: (0, j)),
            pl.BlockSpec((1, 1, HD), lambda j: (j, 0, 0)),
            pl.BlockSpec((1, 1, HD), lambda j: (j, 0, 0)),
            pl.BlockSpec((HD, LANES), lambda j: (0, 0)),
            pl.BlockSpec((2 * LANES, HD), lambda j: (0, 0)),
        ],
        out_specs=[ospec, ospec, ospec],
        out_shape=[out, out, out],
        compiler_params=_cparams(("arbitrary",)),
        name="qkv_sample",
    )(xs, gn, w_bf, gq, gk, smat, e2mat)


ATTN_BLOCKS_PER_STEP = 8


def _attn_plane(qw, kcw, kpw, vcw, vpw, bias_fn, prev):
    q2 = _unpack_words(qw)
    kc2 = _unpack_words(kcw)
    vc2 = _unpack_words(vcw)
    if kpw is not None:
        kp2 = _unpack_words(kpw)
        vp2 = _unpack_words(vpw)
    lane = lax.broadcasted_iota(i32, (BAND, LANES), 1)
    left = lane < HEAD_DIM
    res = []
    for pr in range(2):
        q = q2[pr].astype(bf16)
        if kpw is not None:
            kk = jnp.concatenate([kp2[pr], kc2[pr]], axis=0).astype(bf16)
            vv = jnp.concatenate([vp2[pr], vc2[pr]], axis=0).astype(bf16)
        else:
            kk = kc2[pr].astype(bf16)
            vv = vc2[pr].astype(bf16)
        zq = jnp.zeros_like(q)
        qs = jnp.concatenate([jnp.where(left, q, zq), jnp.where(left, zq, q)], axis=0)
        s = lax.dot_general(qs, kk, (((1,), (1,)), ((), ())), preferred_element_type=f32)
        s = s + bias_fn(pr)
        m = jnp.max(s, axis=1, keepdims=True)
        p = jnp.exp(s - m)
        l = jnp.sum(p, axis=1, keepdims=True)
        o2 = _dot(p.astype(bf16), vv)
        l_pair = jnp.where(left, l[:BAND], l[BAND:])
        m_pair = jnp.where(left, m[:BAND], m[BAND:])
        o = jnp.where(left, o2[:BAND], o2[BAND:]) / l_pair
        lse = m_pair + jnp.log(l_pair)
        if prev is not None:
            po, pl_ = prev[pr]
            mx = jnp.maximum(pl_, lse)
            ea = jnp.exp(pl_ - mx)
            eb = jnp.exp(lse - mx)
            tot = ea + eb
            o = po * (ea / tot) + o * (eb / tot)
            lse = mx + jnp.log(tot)
        res.append((o, lse))
    return res


def _attn_kernel(q0, k0, v0, q1, k1, v1, q2, k2, v2, bias_ref, o_ref,
                 acc_lo, acc_hi, lse_lo, lse_hi, *, seq):
    state = ((acc_lo, lse_lo), (acc_hi, lse_hi))

    def load_state(rows):
        return tuple((a[rows, :], l[rows, :]) for a, l in state)

    def store_state(rows, res):
        for (a, l), (o, lse) in zip(state, res):
            a[rows, :] = o
            l[rows, :] = lse

    def bias2(g, first, pr, cols=None):
        if cols is None:
            b = bias_ref[g, first, pl.ds(2 * pr, 2)]
        else:
            b = bias_ref[g, first, pl.ds(2 * pr, 2), :, cols]
        return b.reshape(2 * BAND, b.shape[-1])

    d2 = DILATIONS[2]
    assert seq == d2 * BAND

    def body2(i, c):
        work = []
        for u in range(ATTN_BLOCKS_PER_STEP):
            cur = pl.ds(i * ATTN_BLOCKS_PER_STEP + u, BAND, stride=d2)
            work.append((cur, (q2[0, cur, :], k2[0, cur, :], None, v2[0, cur, :], None)))
        outs = [_attn_plane(*args, lambda pr: bias2(2, 0, pr, pl.ds(BAND, BAND)), None)
                for _, args in work]
        for (cur, _), res in zip(work, outs):
            store_state(cur, res)
        return c

    lax.fori_loop(0, d2 // ATTN_BLOCKS_PER_STEP, body2, 0)

    d1 = DILATIONS[1]
    nb1 = seq // (d1 * BAND)
    assert (d1 * nb1) % ATTN_BLOCKS_PER_STEP == 0 and d2 % ATTN_BLOCKS_PER_STEP == 0

    def body1(i, c):
        work = []
        for u in range(ATTN_BLOCKS_PER_STEP):
            idx = i * ATTN_BLOCKS_PER_STEP + u
            r = idx // nb1
            jb = idx % nb1
            st = r + jb * (d1 * BAND)
            sp = r + jnp.maximum(jb - 1, 0) * (d1 * BAND)
            first = jnp.where(jb == 0, 1, 0)
            cur = pl.ds(st, BAND, stride=d1)
            prv = pl.ds(sp, BAND, stride=d1)
            work.append((cur, first, (q1[0, cur, :], k1[0, cur, :], k1[0, prv, :],
                                      v1[0, cur, :], v1[0, prv, :]), load_state(cur)))
        outs = [_attn_plane(*args, lambda pr, first=first: bias2(1, first, pr), prev)
                for _, first, args, prev in work]
        for (cur, _, _, _), res in zip(work, outs):
            store_state(cur, res)
        return c

    lax.fori_loop(0, d1 * nb1 // ATTN_BLOCKS_PER_STEP, body1, 0)

    def body0(i, c):
        work = []
        for u in range(ATTN_BLOCKS_PER_STEP):
            jb = i * ATTN_BLOCKS_PER_STEP + u
            st = pl.multiple_of(jb * BAND, BAND)
            sp = pl.multiple_of(jnp.maximum(jb - 1, 0) * BAND, BAND)
            first = jnp.where(jb == 0, 1, 0)
            cur = pl.ds(st, BAND)
            work.append((cur, first, (q0[0, cur, :], k0[0, cur, :], k0[0, pl.ds(sp, BAND), :],
                                      v0[0, cur, :], v0[0, pl.ds(sp, BAND), :]), load_state(cur)))
        outs = [_attn_plane(*args, lambda pr, first=first: bias2(0, first, pr), prev)
                for _, first, args, prev in work]
        for (cur, _, _, _), res in zip(work, outs):
            o_ref[cur, :LANES] = res[0][0].astype(bf16)
            o_ref[cur, LANES:] = res[1][0].astype(bf16)
        return c

    lax.fori_loop(0, seq // BAND // ATTN_BLOCKS_PER_STEP, body0, 0)


def _attn_prompt(planes, bias_tab, batch, seq):
    n = batch * seq
    ppk = PLANES_PER_KIND

    def spec(g, kind):
        return pl.BlockSpec((1, seq, LANES), lambda b, p: ((g * 3 + kind) * ppk + p, b, 0))

    in_specs = [spec(g, kind) for g in range(N_DGROUPS) for kind in range(3)]
    in_specs.append(pl.BlockSpec((N_DGROUPS, 2, 4, BAND, 2 * BAND), lambda b, p: (0, 0, p, 0, 0)))
    return pl.pallas_call(
        functools.partial(_attn_kernel, seq=seq),
        grid=(batch, ppk),
        in_specs=in_specs,
        out_specs=pl.BlockSpec((seq, 2 * LANES), lambda b, p: (b, p)),
        out_shape=jax.ShapeDtypeStruct((n, HD), bf16),
        scratch_shapes=[pltpu.VMEM((seq, LANES), f32)] * 4,
        compiler_params=_cparams(("arbitrary", "arbitrary")),
        name="attn_prompt",
    )(*([planes] * 9), bias_tab)


SAMPLE_BLOCK_BYTES = 4 * 1024 * 1024


def _sample_kv_kernel(c_ref, q_ref, kn_ref, vn_ref, knc_ref, vnc_ref, bias_ref, b0_ref,
                      o_ref, m_ref, l_ref, acc_ref, *, length):
    k = c_ref[0, 0, 0]
    v = c_ref[0, 0, 1]
    hb = k.shape[0]
    q = q_ref[0, 0]
    kn = kn_ref[0, 0]
    vn = vn_ref[0, 0]
    rb = lambda a: a.astype(bf16).astype(f32)
    q8 = jnp.broadcast_to(q[:, None, :], (hb, SUBLANES, HEAD_DIM)).astype(bf16)
    s = jnp.einsum("hqe,hel->hql", q8, k.astype(bf16), preferred_element_type=f32)[:, 0:1, :]
    s = s + bias_ref[...]
    s0 = jnp.sum(rb(kn) * rb(q), axis=-1, keepdims=True)[:, :, None] + b0_ref[...]
    m = jnp.maximum(jnp.max(s, axis=2, keepdims=True), s0)
    p = jnp.exp(s - m)
    p0 = jnp.exp(s0 - m)
    l = jnp.sum(p, axis=2, keepdims=True) + p0
    p8 = jnp.broadcast_to(p, (hb, SUBLANES, length)).astype(bf16)
    acc = jnp.einsum("hql,hel->hqe", p8, v.astype(bf16), preferred_element_type=f32)[:, 0, :]
    acc = acc + rb(p0[:, 0, :]) * rb(vn)
    m_ref[0, 0] = m[:, 0, :]
    l_ref[0, 0] = l[:, 0, :]
    acc_ref[0, 0] = acc
    last = lax.broadcasted_iota(i32, k.shape, 2) == length - 1
    o_ref[0, 0, 0] = jnp.where(last, knc_ref[0], pltpu.roll(k, length - 1, 2))
    o_ref[0, 0, 1] = jnp.where(last, vnc_ref[0], pltpu.roll(v, length - 1, 2))


def _sample_kv_call(cache_t, q, kn, vn, bias, b0, nb):
    length = cache_t.shape[-1]
    hb = max(1, min(N_HEADS, SAMPLE_BLOCK_BYTES // (2 * HEAD_DIM * length * 4)))
    assert N_HEADS % hb == 0
    nh = N_HEADS // hb
    row = pl.BlockSpec((1, 1, hb, HEAD_DIM), lambda b, h: (b, h, 0, 0))
    col = pl.BlockSpec((1, hb, HEAD_DIM, 1), lambda b, h: (b, h, 0, 0))
    stat = pl.BlockSpec((1, 1, hb, 1), lambda b, h: (b, h, 0, 0))
    cspec = pl.BlockSpec((1, 1, 2, hb, HEAD_DIM, length), lambda b, h: (0, b, 0, h, 0, 0))
    grp = lambda a: a.reshape(nb, nh, hb, HEAD_DIM)
    new_t, m, l, acc = pl.pallas_call(
        functools.partial(_sample_kv_kernel, length=length),
        grid=(nb, nh),
        in_specs=[cspec, row, row, row, col, col,
                  pl.BlockSpec((hb, 1, length), lambda b, h: (h, 0, 0)),
                  pl.BlockSpec((hb, 1, 1), lambda b, h: (h, 0, 0))],
        out_specs=[cspec, stat, stat, row],
        out_shape=[jax.ShapeDtypeStruct(cache_t.shape, f32),
                   jax.ShapeDtypeStruct((nb, nh, hb, 1), f32),
                   jax.ShapeDtypeStruct((nb, nh, hb, 1), f32),
                   jax.ShapeDtypeStruct((nb, nh, hb, HEAD_DIM), f32)],
        compiler_params=_cparams(("arbitrary", "arbitrary")),
        name="sample_kv_group",
    )(cache_t, grp(q), grp(kn), grp(vn), kn[..., None], vn[..., None], bias, b0)
    return new_t, m.reshape(nb, N_HEADS, 1), l.reshape(nb, N_HEADS, 1), acc.reshape(nb, N_HEADS, HEAD_DIM)


def _sample_merge_kernel(m_ref, l_ref, acc_ref, o_ref):
    m = m_ref[...]
    mm = jnp.max(m, axis=0)
    w = jnp.exp(m - mm[None])
    num = jnp.sum(w * acc_ref[...], axis=0)
    den = jnp.sum(w * l_ref[...], axis=0)
    o_ref[...] = num / den


def _sample_merge(ms, ls, accs):
    return pl.pallas_call(
        _sample_merge_kernel,
        out_shape=jax.ShapeDtypeStruct(accs.shape[1:], f32),
        name="sample_merge",
    )(ms, ls, accs)


def _route(u, wr_ref, br_ref):
    lg = _dot(u.astype(bf16), wr_ref[...]) + br_ref[...]
    rows = u.shape[0]
    lane = lax.broadcasted_iota(i32, (rows, LANES), 1)
    lanef = lane.astype(f32)
    big = jnp.float32(1e9)
    isg = lane < N_EGROUPS
    lgg = jnp.where(isg, lg, NEG)
    mg = jnp.max(lgg, axis=1, keepdims=True)
    eg = jnp.where(isg, jnp.exp(lgg - mg), 0.0)
    pg = eg / jnp.sum(eg, axis=1, keepdims=True)
    gw = jnp.max(pg, axis=1, keepdims=True)
    gi = jnp.min(jnp.where(isg & (pg == gw), lanef, big), axis=1, keepdims=True)
    grp_of_lane = ((lane - N_EGROUPS) >> 2).astype(f32)
    sel = (lane >= N_EGROUPS) & (lane < N_EGROUPS + N_EXPERTS) & (grp_of_lane == gi)
    les = jnp.where(sel, lg, NEG)
    v1 = jnp.max(les, axis=1, keepdims=True)
    i1 = jnp.min(jnp.where(sel & (les == v1), lanef, big), axis=1, keepdims=True)
    sel2 = sel & (lanef != i1)
    les2 = jnp.where(sel2, lg, NEG)
    v2 = jnp.max(les2, axis=1, keepdims=True)
    i2 = jnp.min(jnp.where(sel2 & (les2 == v2), lanef, big), axis=1, keepdims=True)
    ex = jnp.exp(v2 - v1)
    den = 1.0 + ex
    pw1 = (1.0 / den) * gw
    pw2 = (ex / den) * gw
    e1 = i1 - N_EGROUPS
    e2 = i2 - N_EGROUPS
    first_low = e1 < e2
    lo = jnp.minimum(e1, e2) - 4.0 * gi
    hi = jnp.maximum(e1, e2) - 4.0 * gi
    pair = jnp.where(lo == 0.0, hi - 1.0, jnp.where(lo == 1.0, hi + 1.0, 5.0))
    cls = gi * 6.0 + pair
    g_lo = jnp.where(first_low, pw1, pw2)
    g_hi = jnp.where(first_low, pw2, pw1)
    zero = jnp.zeros((rows, LANES), f32)
    return jnp.where(lane == 0, cls, jnp.where(lane == 1, g_lo, jnp.where(lane == 2, g_hi, zero)))


def _write_slab(slab_ref, u, meta):
    rows = u.shape[0]
    half = D_MODEL // 2
    for c in range(4):
        w = _pack_words(u[:, c * LANES:(c + 1) * LANES], u[:, half + c * LANES:half + (c + 1) * LANES])
        slab_ref[pl.ds(c, rows, stride=SUBLANES), :] = w
    slab_ref[pl.ds(4, rows, stride=SUBLANES), :] = lax.bitcast_convert_type(meta, u32)
    z = jnp.zeros((rows, LANES), u32)
    for c in range(5, SUBLANES):
        slab_ref[pl.ds(c, rows, stride=SUBLANES), :] = z


def _ffn_prep(h, gn_ref, wr_ref, br_ref, slab_ref, metat_ref):
    u = _rms(h, gn_ref[...])
    meta = _route(u, wr_ref, br_ref)
    _write_slab(slab_ref, u, meta)
    metat_ref[...] = meta.T[:SUBLANES, :]


def _from_slab_f32(ref, rows):
    return jnp.concatenate([ref[pl.ds(c, rows, stride=SUBLANES), :] for c in range(SUBLANES)], axis=1)


def _wo_kernel(o_ref, x_ref, wo_ref, gn_ref, wr_ref, br_ref, h_ref, slab_ref, metat_ref):
    h = x_ref[...] + _dot(o_ref[...], wo_ref[...])
    h_ref[...] = h
    _ffn_prep(h, gn_ref, wr_ref, br_ref, slab_ref, metat_ref)


def _router_specs():
    cmap = lambda i: (0, 0)
    return [pl.BlockSpec((1, D_MODEL), cmap),
            pl.BlockSpec((D_MODEL, LANES), cmap),
            pl.BlockSpec((1, LANES), cmap)]


def _prep_out(n, tm):
    specs = [pl.BlockSpec((tm, D_MODEL), lambda i: (i, 0)),
             pl.BlockSpec((tm * SUBLANES, LANES), lambda i: (i, 0)),
             pl.BlockSpec((SUBLANES, tm), lambda i: (0, i))]
    shapes = [jax.ShapeDtypeStruct((n, D_MODEL), f32),
              jax.ShapeDtypeStruct((n * SUBLANES, LANES), u32),
              jax.ShapeDtypeStruct((SUBLANES, n), f32)]
    return specs, shapes


def _wo_call(o_bf, x2, wo_bf, router, tm):
    n = x2.shape[0]
    ospecs, oshapes = _prep_out(n, tm)
    return pl.pallas_call(
        _wo_kernel,
        grid=(n // tm,),
        in_specs=[pl.BlockSpec((tm, HD), lambda i: (i, 0)),
                  pl.BlockSpec((tm, D_MODEL), lambda i: (i, 0)),
                  pl.BlockSpec((HD, D_MODEL), lambda i: (0, 0))] + _router_specs(),
        out_specs=ospecs,
        out_shape=oshapes,
        compiler_params=_cparams(("arbitrary",)),
        name="wo_residual",
    )(o_bf, x2, wo_bf, *router)


def _rank_kernel(metat_ref, rank_ref, cnt_ref, carry_ref, *, tr):
    @pl.when(pl.program_id(0) == 0)
    def _():
        carry_ref[...] = jnp.zeros_like(carry_ref)

    cls = metat_ref[0:1, :]
    sub = lax.broadcasted_iota(i32, (32, tr), 0).astype(f32)
    oh = jnp.where(sub == cls, 1.0, 0.0)
    rr = lax.broadcasted_iota(i32, (tr, tr), 0)
    cc = lax.broadcasted_iota(i32, (tr, tr), 1)
    upper = jnp.where(rr < cc, 1.0, 0.0).astype(bf16)
    pre = _dot(oh.astype(bf16), upper)
    carry = carry_ref[:, 0:1]
    rank_ref[...] = jnp.sum(oh * (pre + carry), axis=0, keepdims=True).astype(i32)
    newc = carry + jnp.sum(oh, axis=1, keepdims=True)
    carry_ref[...] = jnp.broadcast_to(newc, carry_ref.shape)
    cnt_ref[...] = carry_ref[...]


def _rank_call(metat, tr):
    n = metat.shape[1]
    return pl.pallas_call(
        functools.partial(_rank_kernel, tr=tr),
        grid=(n // tr,),
        in_specs=[pl.BlockSpec((SUBLANES, tr), lambda i: (0, i))],
        out_specs=[pl.BlockSpec((1, tr), lambda i: (0, i)),
                   pl.BlockSpec((32, LANES), lambda i: (0, 0))],
        out_shape=[jax.ShapeDtypeStruct((1, n), i32), jax.ShapeDtypeStruct((32, LANES), f32)],
        scratch_shapes=[pltpu.VMEM((32, LANES), f32)],
        compiler_params=_cparams(("arbitrary",)),
        name="class_rank",
    )(metat)


ROWS_PER_GATHER_STEP = 256


GATHER_UNROLL = 8


def _gather_kernel(idx_ref, src_ref, o_ref, sem, *, rows):
    base = pl.program_id(0) * rows

    def issue(kk, c):
        for u in range(GATHER_UNROLL):
            k = kk * GATHER_UNROLL + u
            t = idx_ref[base + k]
            pltpu.make_async_copy(
                src_ref.at[pl.ds(pl.multiple_of(t * SUBLANES, SUBLANES), SUBLANES)],
                o_ref.at[pl.ds(pl.multiple_of(k * SUBLANES, SUBLANES), SUBLANES)], sem).start()
        return c

    lax.fori_loop(0, rows // GATHER_UNROLL, issue, 0)
    pltpu.make_async_copy(src_ref.at[pl.ds(0, rows * SUBLANES)], o_ref, sem).wait()


def _gather_slabs(idx, src):
    n_out = idx.shape[0]
    rows = math.gcd(ROWS_PER_GATHER_STEP, n_out)
    assert rows % GATHER_UNROLL == 0 and src.shape[0] >= rows * SUBLANES
    return pl.pallas_call(
        functools.partial(_gather_kernel, rows=rows),
        grid_spec=pltpu.PrefetchScalarGridSpec(
            num_scalar_prefetch=1, grid=(n_out // rows,),
            in_specs=[pl.BlockSpec(memory_space=pl.ANY)],
            out_specs=pl.BlockSpec((rows * SUBLANES, LANES), lambda i, idx: (i, 0)),
            scratch_shapes=[pltpu.SemaphoreType.DMA(())]),
        out_shape=jax.ShapeDtypeStruct((n_out * SUBLANES, LANES), src.dtype),
        compiler_params=_cparams(("arbitrary",)),
        name="gather_slabs",
    )(idx, src)


def _start_slab_gather(idx_ref, base, chunks, src_ref, buf, sl, sems):
    def issue(kk, c):
        for u in range(GATHER_UNROLL):
            k = kk * GATHER_UNROLL + u
            t = idx_ref[base + k]
            pltpu.make_async_copy(
                src_ref.at[pl.ds(pl.multiple_of(t * SUBLANES, SUBLANES), SUBLANES)],
                buf.at[sl, pl.ds(pl.multiple_of(k * SUBLANES, SUBLANES), SUBLANES)],
                sems.at[sl]).start()
        return c

    lax.fori_loop(0, chunks, issue, 0)


def _wait_slab_gather(chunks, src_ref, buf, sl, sems):
    rows = GATHER_UNROLL * SUBLANES

    def drain(kk, c):
        pltpu.make_async_copy(src_ref.at[pl.ds(0, rows)], buf.at[sl, pl.ds(0, rows)], sems.at[sl]).wait()
        return c

    lax.fori_loop(0, chunks, drain, 0)


def _prefetched_slabs(idx_ref, src_ref, buf, sems, rows):
    i = pl.program_id(0)
    slot = i % 2

    chunks = rows // GATHER_UNROLL

    @pl.when(i == 0)
    def _():
        _start_slab_gather(idx_ref, 0, chunks, src_ref, buf, slot, sems)

    @pl.when(i + 1 < pl.num_programs(0))
    def _():
        _start_slab_gather(idx_ref, (i + 1) * rows, chunks, src_ref, buf, 1 - slot, sems)

    _wait_slab_gather(chunks, src_ref, buf, slot, sems)
    return buf.at[slot]


def _expert_kernel(e1_ref, e2_ref, chunks_ref, src_ref, slab_ref, wg1, wu1, wd1, wg2, wu2, wd2,
                   y_ref, xbuf, sems, *, te):
    del e1_ref, e2_ref
    i = pl.program_id(0)
    nt = pl.num_programs(0)
    slot = i % 2
    chunks = chunks_ref[i]
    nxt = jnp.minimum(i + 1, nt - 1)

    @pl.when(i == 0)
    def _():
        xbuf[...] = jnp.zeros_like(xbuf)
        _start_slab_gather(src_ref, 0, chunks, slab_ref, xbuf, slot, sems)

    @pl.when(i + 1 < nt)
    def _():
        _start_slab_gather(src_ref, (i + 1) * te, chunks_ref[nxt], slab_ref, xbuf, 1 - slot, sems)

    @pl.when(chunks == 0)
    def _():
        y_ref[...] = jnp.zeros_like(y_ref)

    @pl.when(chunks > 0)
    def _():
        _wait_slab_gather(chunks, slab_ref, xbuf, slot, sems)
        xs = xbuf.at[slot]
        los, his = [], []
        for c in range(4):
            lo, hi = _unpack_words(xs[pl.ds(c, te, stride=SUBLANES), :])
            los.append(lo)
            his.append(hi)
        x = jnp.concatenate(los + his, axis=1).astype(bf16)
        meta = lax.bitcast_convert_type(xs[pl.ds(4, te, stride=SUBLANES), :], f32)
        y = None
        for wg, wu, wd, lane in ((wg1, wu1, wd1, 1), (wg2, wu2, wd2, 2)):
            a = _dot(x, wg[0])
            c = _dot(x, wu[0])
            hh = (a * (1.0 / (1.0 + jnp.exp(-a)))) * c * meta[:, lane:lane + 1]
            part = _dot(hh.astype(bf16), wd[0])
            y = part if y is None else y + part
        for c in range(SUBLANES):
            y_ref[pl.ds(c, te, stride=SUBLANES), :] = y[:, c * LANES:(c + 1) * LANES]


def _expert_call(te1, te2, tchunks, src, slab, wg, wu, wd, te):
    ntiles = te1.shape[0]
    assert te % GATHER_UNROLL == 0 and slab.shape[0] >= te * SUBLANES

    def wspec(shape, which):
        if which == 1:
            return pl.BlockSpec(shape, lambda i, e1, e2, v, s: (e1[i], 0, 0))
        return pl.BlockSpec(shape, lambda i, e1, e2, v, s: (e2[i], 0, 0))

    gs = (1, D_MODEL, D_EXPERT)
    ds_ = (1, D_EXPERT, D_MODEL)
    return pl.pallas_call(
        functools.partial(_expert_kernel, te=te),
        grid_spec=pltpu.PrefetchScalarGridSpec(
            num_scalar_prefetch=4, grid=(ntiles,),
            in_specs=[pl.BlockSpec(memory_space=pl.ANY),
                      wspec(gs, 1), wspec(gs, 1), wspec(ds_, 1),
                      wspec(gs, 2), wspec(gs, 2), wspec(ds_, 2)],
            out_specs=pl.BlockSpec((te * SUBLANES, LANES), lambda i, e1, e2, v, s: (i, 0)),
            scratch_shapes=[pltpu.VMEM((2, te * SUBLANES, LANES), u32),
                            pltpu.SemaphoreType.DMA((2,))]),
        out_shape=jax.ShapeDtypeStruct((ntiles * te * SUBLANES, LANES), f32),
        compiler_params=_cparams(("arbitrary",)),
        name="expert_pair_ffn",
    )(te1, te2, tchunks, src, slab, wg, wu, wd, wg, wu, wd)


def _invert_kernel(pos_ref, src_ref, *, n, slots):
    def clear(kk, c):
        for u in range(GATHER_UNROLL):
            src_ref[kk * GATHER_UNROLL + u] = 0
        return c

    def place(kk, c):
        for u in range(GATHER_UNROLL):
            t = kk * GATHER_UNROLL + u
            src_ref[pos_ref[t]] = t
        return c

    lax.fori_loop(0, slots // GATHER_UNROLL, clear, 0)
    lax.fori_loop(0, n // GATHER_UNROLL, place, 0)


def _invert_slots(pos, slots):
    n = pos.shape[0]
    assert n % GATHER_UNROLL == 0 and slots % GATHER_UNROLL == 0
    return pl.pallas_call(
        functools.partial(_invert_kernel, n=n, slots=slots),
        in_specs=[pl.BlockSpec(memory_space=pltpu.SMEM)],
        out_specs=pl.BlockSpec(memory_space=pltpu.SMEM),
        out_shape=jax.ShapeDtypeStruct((slots,), i32),
        name="invert_slots",
    )(pos)


def _moe(slab, metat, wg, wu, wd, layer, te, tr):
    n = metat.shape[1]
    rank, cnt = _rank_call(metat, tr)
    counts = cnt[:N_CLASSES, 0].astype(i32)
    padded = ((counts + te - 1) // te) * te
    ends = jnp.cumsum(padded)
    off = ends - padded
    cls = metat[0].astype(i32)
    class_ids = jnp.arange(N_CLASSES, dtype=i32)
    pos = rank[0] + jnp.sum(jnp.where(cls[:, None] == class_ids[None, :], off[None, :], 0), axis=1)
    ntiles = -(-n // te) + N_CLASSES
    tstart = jnp.arange(ntiles, dtype=i32) * te
    tcls = jnp.minimum(jnp.sum((tstart[:, None] >= ends[None, :]).astype(i32), axis=1), N_CLASSES - 1)
    real_end = (off + counts)[tcls]
    tchunks = jnp.where(tstart < ends[-1], (jnp.clip(real_end - tstart, 0, te) + GATHER_UNROLL - 1) // GATHER_UNROLL, 0)
    tchunks = tchunks.astype(i32)
    te1 = jnp.asarray(_CLASS_E1 + layer * N_EXPERTS)[tcls]
    te2 = jnp.asarray(_CLASS_E2 + layer * N_EXPERTS)[tcls]
    src = _invert_slots(pos, ntiles * te)
    return pos, _expert_call(te1, te2, tchunks, src, slab, wg, wu, wd, te)


def _pool_mix_tail(hf, up, ext, tpos0, pw_ref, pb_ref, ps_ref):
    rows = up.shape[0]
    tpos = tpos0 + lax.broadcasted_iota(i32, (rows, 1), 0)
    parts = []
    for gi, win in enumerate(POOL_WINDOWS):
        sl = slice(gi * POOL_CH, (gi + 1) * POOL_CH)
        s = ext[:, sl]
        for kk in range(gi + 1):
            s = s + pltpu.roll(s, 1 << kk, 0)
        cnt = jnp.minimum(win, tpos + 1).astype(f32)
        z = s[16:, :] / cnt - up[:, sl]
        parts.append(_dot(z.astype(bf16), pw_ref[gi]))
    y = (jnp.concatenate(parts, axis=1) + pb_ref[...]) * ps_ref[...]
    return hf + y


def _pool_kernel(pos_ref, h_ref, ys_ref, pn_ref, pw_ref, pb_ref, ps_ref, gn_ref, wr_ref, br_ref,
                 h2_ref, slab_ref, metat_ref, st_ref, ybuf, sems, halo_ref, *, tm, tps):
    r = pl.program_id(0) % tps

    @pl.when(pl.program_id(0) == 0)
    def _():
        halo_ref[...] = jnp.zeros_like(halo_ref)

    y = _prefetched_slabs(pos_ref, ys_ref, ybuf, sems, tm)
    hf = h_ref[...] + _from_slab_f32(y, tm)
    up = _rms(hf, pn_ref[...])
    uph = jnp.where(r > 0, halo_ref[...], 0.0)
    ext = jnp.concatenate([uph, up], axis=0)
    h2 = _pool_mix_tail(hf, up, ext, r * tm, pw_ref, pb_ref, ps_ref)
    h2_ref[...] = h2
    st_ref[...] = up[tm - 16:, :]
    halo_ref[...] = up[tm - 16:, :]
    _ffn_prep(h2, gn_ref, wr_ref, br_ref, slab_ref, metat_ref)


def _pool_call(h, pos, ys, pn, pw_bf, pb, psc, router, batch, seq, tm):
    n = h.shape[0]
    tps = seq // tm
    assert tm % GATHER_UNROLL == 0 and ys.shape[0] >= tm * SUBLANES
    c2 = lambda i, p: (0, 0)
    return pl.pallas_call(
        functools.partial(_pool_kernel, tm=tm, tps=tps),
        grid_spec=pltpu.PrefetchScalarGridSpec(
            num_scalar_prefetch=1, grid=(n // tm,),
            in_specs=[pl.BlockSpec((tm, D_MODEL), lambda i, p: (i, 0)),
                      pl.BlockSpec(memory_space=pl.ANY),
                      pl.BlockSpec((1, D_MODEL), c2),
                      pl.BlockSpec((4, POOL_CH, POOL_CH), lambda i, p: (0, 0, 0)),
                      pl.BlockSpec((1, D_MODEL), c2),
                      pl.BlockSpec((1, D_MODEL), c2),
                      pl.BlockSpec((1, D_MODEL), c2),
                      pl.BlockSpec((D_MODEL, LANES), c2),
                      pl.BlockSpec((1, LANES), c2)],
            out_specs=[pl.BlockSpec((tm, D_MODEL), lambda i, p: (i, 0)),
                       pl.BlockSpec((tm * SUBLANES, LANES), lambda i, p: (i, 0)),
                       pl.BlockSpec((SUBLANES, tm), lambda i, p: (0, i)),
                       pl.BlockSpec((16, D_MODEL), lambda i, p: (i // tps, 0))],
            scratch_shapes=[pltpu.VMEM((2, tm * SUBLANES, LANES), f32),
                            pltpu.SemaphoreType.DMA((2,)),
                            pltpu.VMEM((16, D_MODEL), f32)]),
        out_shape=[jax.ShapeDtypeStruct((n, D_MODEL), f32),
                   jax.ShapeDtypeStruct((n * SUBLANES, LANES), u32),
                   jax.ShapeDtypeStruct((SUBLANES, n), f32),
                   jax.ShapeDtypeStruct((batch * 16, D_MODEL), f32)],
        compiler_params=_cparams(("arbitrary",)),
        name="pool_mixer",
    )(pos, h, ys, pn, pw_bf, pb, psc, *router)


def _pool_sample_kernel(h_ref, y_ref, st_ref, pn_ref, pw_ref, pb_ref, ps_ref,
                        gn_ref, wr_ref, br_ref, h2_ref, slab_ref, metat_ref, nst_ref, *, nb):
    rows = h_ref.shape[0]
    hf = h_ref[...] + _from_slab_f32(y_ref, rows)
    up = _rms(hf, pn_ref[...])
    st = st_ref[...]
    ub = up[:nb, :]
    parts = []
    for gi, win in enumerate(POOL_WINDOWS):
        sl = slice(gi * POOL_CH, (gi + 1) * POOL_CH)
        tot = ub[:, sl] + jnp.sum(st[:, POOL_STATE - (win - 1):, sl], axis=1)
        z = tot / float(win) - ub[:, sl]
        parts.append(_dot(z.astype(bf16), pw_ref[gi]))
    y = (jnp.concatenate(parts, axis=1) + pb_ref[...]) * ps_ref[...]
    if rows > nb:
        y = jnp.concatenate([y, jnp.zeros((rows - nb, D_MODEL), f32)], axis=0)
    h2 = hf + y
    h2_ref[...] = h2
    nst_ref[:, :POOL_STATE - 1, :] = st[:, 1:, :]
    nst_ref[:, POOL_STATE - 1, :] = ub
    _ffn_prep(h2, gn_ref, wr_ref, br_ref, slab_ref, metat_ref)


def _pool_sample_call(h, ytt, state, pn, pw_bf, pb, psc, router, nb):
    rows = h.shape[0]
    ospecs, oshapes = _prep_out(rows, rows)
    ospecs.append(pl.BlockSpec((nb, POOL_STATE, D_MODEL), lambda i: (0, 0, 0)))
    oshapes.append(jax.ShapeDtypeStruct((nb, POOL_STATE, D_MODEL), f32))
    cm = lambda i: (0, 0)
    return pl.pallas_call(
        functools.partial(_pool_sample_kernel, nb=nb),
        grid=(1,),
        in_specs=[pl.BlockSpec((rows, D_MODEL), cm),
                  pl.BlockSpec((rows * SUBLANES, LANES), cm),
                  pl.BlockSpec((nb, POOL_STATE, D_MODEL), lambda i: (0, 0, 0)),
                  pl.BlockSpec((1, D_MODEL), cm),
                  pl.BlockSpec((4, POOL_CH, POOL_CH), lambda i: (0, 0, 0)),
                  pl.BlockSpec((1, D_MODEL), cm),
                  pl.BlockSpec((1, D_MODEL), cm)] + _router_specs(),
        out_specs=ospecs,
        out_shape=oshapes,
        compiler_params=_cparams(("arbitrary",)),
        name="pool_mixer_sample",
    )(h, ytt, state, pn, pw_bf, pb, psc, *router)


def _final_kernel(h_ref, y_ref, o_ref, *, tm):
    o_ref[...] = h_ref[...] + _from_slab_f32(y_ref, tm)


def _final_call(h, ytt, tm):
    n = h.shape[0]
    return pl.pallas_call(
        functools.partial(_final_kernel, tm=tm),
        grid=(n // tm,),
        in_specs=[pl.BlockSpec((tm, D_MODEL), lambda i: (i, 0)),
                  pl.BlockSpec((tm * SUBLANES, LANES), lambda i: (i, 0))],
        out_specs=pl.BlockSpec((tm, D_MODEL), lambda i: (i, 0)),
        out_shape=jax.ShapeDtypeStruct((n, D_MODEL), f32),
        compiler_params=_cparams(("arbitrary",)),
        name="final_residual",
    )(h, ytt)


def _final_gather_kernel(pos_ref, h_ref, ys_ref, o_ref, ybuf, sems, *, tm):
    y = _prefetched_slabs(pos_ref, ys_ref, ybuf, sems, tm)
    o_ref[...] = h_ref[...] + _from_slab_f32(y, tm)


def _final_gather_call(h, pos, ys, tm):
    n = h.shape[0]
    assert tm % GATHER_UNROLL == 0 and ys.shape[0] >= tm * SUBLANES
    return pl.pallas_call(
        functools.partial(_final_gather_kernel, tm=tm),
        grid_spec=pltpu.PrefetchScalarGridSpec(
            num_scalar_prefetch=1, grid=(n // tm,),
            in_specs=[pl.BlockSpec((tm, D_MODEL), lambda i, p: (i, 0)),
                      pl.BlockSpec(memory_space=pl.ANY)],
            out_specs=pl.BlockSpec((tm, D_MODEL), lambda i, p: (i, 0)),
            scratch_shapes=[pltpu.VMEM((2, tm * SUBLANES, LANES), f32),
                            pltpu.SemaphoreType.DMA((2,))]),
        out_shape=jax.ShapeDtypeStruct((n, D_MODEL), f32),
        compiler_params=_cparams(("arbitrary",)),
        name="final_residual_gather",
    )(pos, h, ys)


def _t5_bucket(dist):
    max_exact = N_BUCKETS // 2
    df = jnp.maximum(dist, max_exact).astype(f32)
    large = max_exact + (jnp.log(df / max_exact) / math.log(MAX_DISTANCE / max_exact)
                         * (N_BUCKETS - max_exact)).astype(i32)
    large = jnp.minimum(large, N_BUCKETS - 1)
    return jnp.where(dist < max_exact, dist, large)


def _prompt_bias_tables(rel_bias):
    qi = jnp.arange(BAND)[:, None]
    kj = jnp.arange(2 * BAND)[None, :]
    step = BAND + qi - kj
    tabs = []
    for g, (win, dil) in enumerate(zip(WINDOWS, DILATIONS)):
        n_step = win // dil
        tab = rel_bias[:, g * N_HEADS:(g + 1) * N_HEADS].astype(f32)
        svals = jnp.maximum(2 * BAND - 1 - jnp.arange(3 * BAND), 0)
        bvec = tab[_t5_bucket(svals * dil)].T
        bias = jnp.stack([bvec[:, BAND - 1 - q:3 * BAND - 1 - q] for q in range(BAND)], axis=1)
        band = (step >= 0) & (step <= n_step)
        reg = jnp.where(band[None], bias, NEG)
        first = jnp.where((band & (kj >= BAND))[None], bias, NEG)
        tabs.append(jnp.stack([reg, first]))
    return jnp.stack(tabs)


def _sample_bias_tables(rel_bias, lengths):
    bs, b0 = [], []
    for g, (win, dil) in enumerate(zip(WINDOWS, DILATIONS)):
        length = lengths[g]
        n_step = win // dil
        tab = rel_bias[:, g * N_HEADS:(g + 1) * N_HEADS].astype(f32)
        dist = length - jnp.arange(length)
        hit = (dist % dil == 0) & (dist <= n_step * dil)
        b = jnp.where(hit[:, None], tab[_t5_bucket(dist)], NEG)
        bs.append(b.T[:, None, :])
        b0.append(tab[_t5_bucket(jnp.zeros((1,), i32))].T[:, :, None])
    return bs, b0


def _router_weights(ffn_norm_i, wgr, bgr, wer, ber):
    w = jnp.zeros((D_MODEL, LANES), f32)
    w = w.at[:, :N_EGROUPS].set(wgr).at[:, N_EGROUPS:N_EGROUPS + N_EXPERTS].set(wer)
    b = jnp.zeros((1, LANES), f32)
    b = b.at[0, :N_EGROUPS].set(bgr).at[0, N_EGROUPS:N_EGROUPS + N_EXPERTS].set(ber)
    return ffn_norm_i.reshape(1, D_MODEL), w.astype(bf16), b


def _norm_mats():
    head_of_col = np.arange(HD) // HEAD_DIM
    s = (head_of_col[:, None] == np.arange(LANES)[None, :]).astype(np.float32)
    e = s.T
    return jnp.asarray(s, bf16), jnp.asarray(np.concatenate([e, e], axis=0), bf16)


def kernel(x_prompt, x_sample, cache_kv_w128, cache_kv_w512, cache_kv_w2048, state_pool, w_qkv, q_norm, k_norm, w_o, rel_bias, attn_norm, pool_norm, pool_w, pool_b, pool_scale, ffn_norm, router_group_w, router_group_b, router_expert_w, router_expert_b, w_gate, w_up, w_down):
    batch, seq, _ = x_prompt.shape
    nb = x_sample.shape[0]
    assert x_sample.shape[1] == 1
    n = batch * seq
    srows = LANES
    caches6 = (cache_kv_w128, cache_kv_w512, cache_kv_w2048)

    smat, e2mat = _norm_mats()
    wqkv_bf = w_qkv[0].astype(bf16)
    wo_bf = w_o[0].astype(bf16)
    gq = jnp.tile(q_norm[0], (1, N_HEADS)).reshape(N_DGROUPS, 1, HD)
    gk = jnp.tile(k_norm[0], (1, N_HEADS)).reshape(N_DGROUPS, 1, HD)
    an = attn_norm[0].reshape(1, D_MODEL)
    routers = [_router_weights(ffn_norm[i], router_group_w[i], router_group_b[i],
                               router_expert_w[i], router_expert_b[i]) for i in range(2)]
    allx = lambda w: w.astype(bf16).reshape((-1,) + w.shape[2:])
    wg_bf, wu_bf, wd_bf = allx(w_gate), allx(w_up), allx(w_down)
    pw_bf = pool_w[0].astype(bf16)
    pn = pool_norm[0].reshape(1, D_MODEL)
    pb = pool_b[0].reshape(1, D_MODEL)
    psc = pool_scale[0].reshape(1, D_MODEL)

    x2 = x_prompt.reshape(n, D_MODEL)
    planes, kv2, kv1, kv0 = _qkv_prompt(x2, an, wqkv_bf, gq, gk, smat, e2mat, batch, seq)
    o_bf = _attn_prompt(planes, _prompt_bias_tables(rel_bias), batch, seq)
    h0, slab0, metat0 = _wo_call(o_bf, x2, wo_bf, routers[0], tm=512)
    pos0, ys0 = _moe(slab0, metat0, wg_bf, wu_bf, wd_bf, 0, te=256, tr=512)
    h1, slab1, metat1, pst = _pool_call(h0, pos0, ys0, pn, pw_bf, pb, psc, routers[1], batch, seq, tm=512)
    pos1, ys1 = _moe(slab1, metat1, wg_bf, wu_bf, wd_bf, 1, te=256, tr=512)
    y_prompt = _final_gather_call(h1, pos1, ys1, tm=512).reshape(batch, seq, D_MODEL)

    new_kv_p = [jnp.transpose(t, (0, 1, 5, 2, 3, 4)) for t in (kv0, kv1, kv2)]
    new_pool_p = pst.reshape(batch, 16, D_MODEL)[:, 16 - POOL_STATE:][None]

    xs = jnp.zeros((srows, D_MODEL), f32).at[:nb].set(x_sample[:, 0, :])
    qs, ks, vs = _qkv_sample(xs, an, wqkv_bf, gq, gk, smat, e2mat)
    to4 = lambda a: a[:, :nb].reshape(N_DGROUPS, nb, N_HEADS, HEAD_DIM)
    qs4, ks4, vs4 = to4(qs), to4(ks), to4(vs)
    bias_s, bias_s0 = _sample_bias_tables(rel_bias, [c.shape[2] for c in caches6])
    new_kv_s, ms, ls, accs = [], [], [], []
    for g in range(N_DGROUPS):
        cache_t = jnp.transpose(caches6[g], (0, 1, 3, 4, 5, 2))
        new_t, m_g, l_g, acc_g = _sample_kv_call(cache_t, qs4[g], ks4[g], vs4[g], bias_s[g], bias_s0[g], nb)
        new_kv_s.append(jnp.transpose(new_t, (0, 1, 5, 2, 3, 4)))
        ms.append(m_g)
        ls.append(l_g)
        accs.append(acc_g)
    os_ = _sample_merge(jnp.stack(ms), jnp.stack(ls), jnp.stack(accs))
    os_pad = jnp.zeros((srows, HD), bf16).at[:nb].set(os_.reshape(nb, HD).astype(bf16))
    hs0, sslab0, smetat0 = _wo_call(os_pad, xs, wo_bf, routers[0], tm=srows)
    ystt0 = _gather_slabs(*_moe(sslab0, smetat0, wg_bf, wu_bf, wd_bf, 0, te=64, tr=srows))
    hs1, sslab1, smetat1, new_pool_s = _pool_sample_call(
        hs0, ystt0, state_pool[0], pn, pw_bf, pb, psc, routers[1], nb)
    ystt1 = _gather_slabs(*_moe(sslab1, smetat1, wg_bf, wu_bf, wd_bf, 1, te=64, tr=srows))
    y_sample = _final_call(hs1, ystt1, tm=srows)[:nb].reshape(nb, 1, D_MODEL)

    return (y_prompt, y_sample, new_kv_p[0], new_kv_p[1], new_kv_p[2], new_pool_p,
            new_kv_s[0], new_kv_s[1], new_kv_s[2], new_pool_s[None])
```

```python
import functools
import math

import numpy as np
import jax
import jax.numpy as jnp
from jax import lax
from jax.experimental import pallas as pl
from jax.experimental.pallas import tpu as pltpu

f32 = jnp.float32
bf16 = jnp.bfloat16
u32 = jnp.uint32
i32 = jnp.int32

D_MODEL = 1024
N_HEADS = 16
HEAD_DIM = 64
HD = N_HEADS * HEAD_DIM
N_DGROUPS = 3
WINDOWS = (128, 512, 2048)
DILATIONS = (1, 4, 16)
BAND = 128
N_BUCKETS = 32
MAX_DISTANCE = 2048
POOL_WINDOWS = (2, 4, 8, 16)
POOL_CH = D_MODEL // 4
POOL_STATE = max(POOL_WINDOWS) - 1
N_EGROUPS = 4
EXPERTS_PER_GROUP = 4
N_EXPERTS = 16
D_EXPERT = D_MODEL // 2
RMS_EPS = 1e-6
NEG = -1e30

LANES = 128
SUBLANES = 8
HI_MASK = 0xFFFF0000
N_CLASSES = N_EGROUPS * 6
PLANES_PER_KIND = HD // (2 * LANES)
VMEM_LIMIT = 56 * 1024 * 1024

_PAIRS = [(a, b) for a in range(4) for b in range(a + 1, 4)]
_CLASS_E1 = np.array([g * 4 + _PAIRS[p][0] for g in range(4) for p in range(6)], np.int32)
_CLASS_E2 = np.array([g * 4 + _PAIRS[p][1] for g in range(4) for p in range(6)], np.int32)


def _cparams(sem=None, vmem=VMEM_LIMIT):
    kw = dict(vmem_limit_bytes=vmem)
    if sem is not None:
        kw["dimension_semantics"] = sem
    return pltpu.CompilerParams(**kw)


def _dot(a, b):
    return jnp.dot(a, b, preferred_element_type=f32)


def _rms(x, g):
    return x * lax.rsqrt(jnp.mean(x * x, axis=-1, keepdims=True) + RMS_EPS) * g


def _pack_words(lo, hi):
    lb = lax.bitcast_convert_type(lo.astype(bf16).astype(f32), u32) >> 16
    hb = lax.bitcast_convert_type(hi.astype(bf16).astype(f32), u32) & jnp.uint32(HI_MASK)
    return lb | hb


def _unpack_words(w):
    lo = lax.bitcast_convert_type(w << 16, f32)
    hi = lax.bitcast_convert_type(w & jnp.uint32(HI_MASK), f32)
    return lo, hi


def _head_norm(z, gain, s_ref, e2_ref):
    zz = z * z
    zz_hi = zz.astype(bf16)
    zz_lo = (zz - zz_hi.astype(f32)).astype(bf16)
    ss = _dot(zz_hi, s_ref[...]) + _dot(zz_lo, s_ref[...])
    r = lax.rsqrt(ss * (1.0 / HEAD_DIM) + RMS_EPS)
    r_hi = r.astype(bf16)
    r_lo = (r - r_hi.astype(f32)).astype(bf16)
    rb = _dot(jnp.concatenate([r_hi, r_lo], axis=1), e2_ref[...])
    return z * rb * gain


def _qkv_kernel(x_ref, gn_ref, w_ref, gq_ref, gk_ref, s_ref, e2_ref,
                planes_ref, kv2_ref, kv1_ref, kv0_ref, *, tm, tps, t512):
    j = pl.program_id(0)
    r = pl.program_id(1) % tps
    u = _rms(x_ref[...], gn_ref[...]).astype(bf16)
    q = _head_norm(_dot(u, w_ref[:, :HD]), gq_ref[0], s_ref, e2_ref) * (HEAD_DIM ** -0.5)
    k = _head_norm(_dot(u, w_ref[:, HD:2 * HD]), gk_ref[0], s_ref, e2_ref)
    v = _dot(u, w_ref[:, 2 * HD:])
    for kind, arr in enumerate((q, k, v)):
        for p in range(PLANES_PER_KIND):
            c0 = 2 * LANES * p
            planes_ref[kind * PLANES_PER_KIND + p] = _pack_words(
                arr[:, c0:c0 + LANES], arr[:, c0 + LANES:c0 + 2 * LANES])

    def store_t(ref, kk, vv):
        ref[0, 0, 0] = kk.T.reshape(N_HEADS, HEAD_DIM, kk.shape[0])
        ref[0, 0, 1] = vv.T.reshape(N_HEADS, HEAD_DIM, vv.shape[0])

    @pl.when(j == 2)
    def _():
        store_t(kv2_ref, k, v)

    @pl.when((j == 1) & (r >= tps - t512))
    def _():
        store_t(kv1_ref, k, v)

    @pl.when((j == 0) & (r == tps - 1))
    def _():
        store_t(kv0_ref, k[tm - BAND:, :], v[tm - BAND:, :])


def _qkv_prompt(x2, gn, w_bf, gq, gk, smat, e2mat, batch, seq, tm=256):
    n = x2.shape[0]
    nt = n // tm
    tps = seq // tm
    t512 = WINDOWS[1] // tm
    assert seq % tm == 0 and WINDOWS[1] % tm == 0 and tm >= BAND and seq == WINDOWS[2]

    def kv2_map(j, i):
        on = j == 2
        return (0, jnp.where(on, i // tps, 0), 0, 0, 0, jnp.where(on, i % tps, 0))

    def kv1_map(j, i):
        inside = jnp.maximum(i % tps - (tps - t512), 0)
        b = jnp.where(j == 1, i // tps, jnp.where(j < 1, 0, batch - 1))
        t = jnp.where(j == 1, inside, jnp.where(j < 1, 0, t512 - 1))
        return (0, b, 0, 0, 0, t)

    def kv0_map(j, i):
        return (0, jnp.where(j == 0, i // tps, batch - 1), 0, 0, 0, 0)

    nplanes = 3 * PLANES_PER_KIND
    kvt = lambda keep: (1, batch, 2, N_HEADS, HEAD_DIM, keep)
    kvblock = lambda cols: (1, 1, 2, N_HEADS, HEAD_DIM, cols)
    return pl.pallas_call(
        functools.partial(_qkv_kernel, tm=tm, tps=tps, t512=t512),
        grid=(N_DGROUPS, nt),
        in_specs=[
            pl.BlockSpec((tm, D_MODEL), lambda j, i: (i, 0)),
            pl.BlockSpec((1, D_MODEL), lambda j, i: (0, 0)),
            pl.BlockSpec((D_MODEL, 3 * HD), lambda j, i: (0, j)),
            pl.BlockSpec((1, 1, HD), lambda j, i: (j, 0, 0)),
            pl.BlockSpec((1, 1, HD), lambda j, i: (j, 0, 0)),
            pl.BlockSpec((HD, LANES), lambda j, i: (0, 0)),
            pl.BlockSpec((2 * LANES, HD), lambda j, i: (0, 0)),
        ],
        out_specs=[
            pl.BlockSpec((nplanes, tm, LANES), lambda j, i: (j, i, 0)),
            pl.BlockSpec(kvblock(tm), kv2_map),
            pl.BlockSpec(kvblock(tm), kv1_map),
            pl.BlockSpec(kvblock(BAND), kv0_map),
        ],
        out_shape=[
            jax.ShapeDtypeStruct((N_DGROUPS * nplanes, n, LANES), u32),
            jax.ShapeDtypeStruct(kvt(WINDOWS[2]), f32),
            jax.ShapeDtypeStruct(kvt(WINDOWS[1]), f32),
            jax.ShapeDtypeStruct(kvt(WINDOWS[0]), f32),
        ],
        compiler_params=_cparams(("arbitrary", "arbitrary")),
        name="qkv_prompt",
    )(x2, gn, w_bf, gq, gk, smat, e2mat)


def _qkv_sample_kernel(x_ref, gn_ref, w_ref, gq_ref, gk_ref, s_ref, e2_ref, q_ref, k_ref, v_ref):
    u = _rms(x_ref[...], gn_ref[...]).astype(bf16)
    z = _dot(u, w_ref[...])
    q_ref[0] = _head_norm(z[:, :HD], gq_ref[0], s_ref, e2_ref) * (HEAD_DIM ** -0.5)
    k_ref[0] = _head_norm(z[:, HD:2 * HD], gk_ref[0], s_ref, e2_ref)
    v_ref[0] = z[:, 2 * HD:]


def _qkv_sample(xs, gn, w_bf, gq, gk, smat, e2mat):
    rows = xs.shape[0]
    out = jax.ShapeDtypeStruct((N_DGROUPS, rows, HD), f32)
    ospec = pl.BlockSpec((1, rows, HD), lambda j: (j, 0, 0))
    return pl.pallas_call(
        _qkv_sample_kernel,
        grid=(N_DGROUPS,),
        in_specs=[
            pl.BlockSpec((rows, D_MODEL), lambda j: (0, 0)),
            pl.BlockSpec((1, D_MODEL), lambda j: (0, 0)),
            pl.BlockSpec((D_MODEL, 3 * HD), lambda j: (0, j)),
            pl.BlockSpec((1, 1, HD), lambda j: (j, 0, 0)),
            pl.BlockSpec((1, 1, HD), lambda j: (j, 0, 0)),
            pl.BlockSpec((HD, LANES), lambda j: (0, 0)),
            pl.BlockSpec((2 * LANES, HD), lambda j: (0, 0)),
        ],
        out_specs=[ospec, ospec, ospec],
        out_shape=[out, out, out],
        compiler_params=_cparams(("arbitrary",)),
        name="qkv_sample",
    )(xs, gn, w_bf, gq, gk, smat, e2mat)


ATTN_BLOCKS_PER_STEP = 8


def _attn_plane(qw, kcw, kpw, vcw, vpw, bias_fn, prev):
    q2 = _unpack_words(qw)
    kc2 = _unpack_words(kcw)
    vc2 = _unpack_words(vcw)
    if kpw is not None:
        kp2 = _unpack_words(kpw)
        vp2 = _unpack_words(vpw)
    lane = lax.broadcasted_iota(i32, (BAND, LANES), 1)
    left = lane < HEAD_DIM
    res = []
    for pr in range(2):
        q = q2[pr].astype(bf16)
        if kpw is not None:
            kk = jnp.concatenate([kp2[pr], kc2[pr]], axis=0).astype(bf16)
            vv = jnp.concatenate([vp2[pr], vc2[pr]], axis=0).astype(bf16)
        else:
            kk = kc2[pr].astype(bf16)
            vv = vc2[pr].astype(bf16)
        zq = jnp.zeros_like(q)
        qs = jnp.concatenate([jnp.where(left, q, zq), jnp.where(left, zq, q)], axis=0)
        s = lax.dot_general(qs, kk, (((1,), (1,)), ((), ())), preferred_element_type=f32)
        s = s + bias_fn(pr)
        m = jnp.max(s, axis=1, keepdims=True)
        p = jnp.exp(s - m)
        l = jnp.sum(p, axis=1, keepdims=True)
        o2 = _dot(p.astype(bf16), vv)
        l_pair = jnp.where(left, l[:BAND], l[BAND:])
        m_pair = jnp.where(left, m[:BAND], m[BAND:])
        o = jnp.where(left, o2[:BAND], o2[BAND:]) / l_pair
        lse = m_pair + jnp.log(l_pair)
        if prev is not None:
            po, pl_ = prev[pr]
            mx = jnp.maximum(pl_, lse)
            ea = jnp.exp(pl_ - mx)
            eb = jnp.exp(lse - mx)
            tot = ea + eb
            o = po * (ea / tot) + o * (eb / tot)
            lse = mx + jnp.log(tot)
        res.append((o, lse))
    return res


def _attn_kernel(q0, k0, v0, q1, k1, v1, q2, k2, v2, bias_ref, o_ref,
                 acc_lo, acc_hi, lse_lo, lse_hi, *, seq):
    state = ((acc_lo, lse_lo), (acc_hi, lse_hi))

    def load_state(rows):
        return tuple((a[rows, :], l[rows, :]) for a, l in state)

    def store_state(rows, res):
        for (a, l), (o, lse) in zip(state, res):
            a[rows, :] = o
            l[rows, :] = lse

    def bias2(g, first, pr, cols=None):
        if cols is None:
            b = bias_ref[g, first, pl.ds(2 * pr, 2)]
        else:
            b = bias_ref[g, first, pl.ds(2 * pr, 2), :, cols]
        return b.reshape(2 * BAND, b.shape[-1])

    d2 = DILATIONS[2]
    assert seq == d2 * BAND

    def body2(i, c):
        work = []
        for u in range(ATTN_BLOCKS_PER_STEP):
            cur = pl.ds(i * ATTN_BLOCKS_PER_STEP + u, BAND, stride=d2)
            work.append((cur, (q2[0, cur, :], k2[0, cur, :], None, v2[0, cur, :], None)))
        outs = [_attn_plane(*args, lambda pr: bias2(2, 0, pr, pl.ds(BAND, BAND)), None)
                for _, args in work]
        for (cur, _), res in zip(work, outs):
            store_state(cur, res)
        return c

    lax.fori_loop(0, d2 // ATTN_BLOCKS_PER_STEP, body2, 0)

    d1 = DILATIONS[1]
    nb1 = seq // (d1 * BAND)
    assert (d1 * nb1) % ATTN_BLOCKS_PER_STEP == 0 and d2 % ATTN_BLOCKS_PER_STEP == 0

    def body1(i, c):
        work = []
        for u in range(ATTN_BLOCKS_PER_STEP):
            idx = i * ATTN_BLOCKS_PER_STEP + u
            r = idx // nb1
            jb = idx % nb1
            st = r + jb * (d1 * BAND)
            sp = r + jnp.maximum(jb - 1, 0) * (d1 * BAND)
            first = jnp.where(jb == 0, 1, 0)
            cur = pl.ds(st, BAND, stride=d1)
            prv = pl.ds(sp, BAND, stride=d1)
            work.append((cur, first, (q1[0, cur, :], k1[0, cur, :], k1[0, prv, :],
                                      v1[0, cur, :], v1[0, prv, :]), load_state(cur)))
        outs = [_attn_plane(*args, lambda pr, first=first: bias2(1, first, pr), prev)
                for _, first, args, prev in work]
        for (cur, _, _, _), res in zip(work, outs):
            store_state(cur, res)
        return c

    lax.fori_loop(0, d1 * nb1 // ATTN_BLOCKS_PER_STEP, body1, 0)

    def body0(i, c):
        work = []
        for u in range(ATTN_BLOCKS_PER_STEP):
            jb = i * ATTN_BLOCKS_PER_STEP + u
            st = pl.multiple_of(jb * BAND, BAND)
            sp = pl.multiple_of(jnp.maximum(jb - 1, 0) * BAND, BAND)
            first = jnp.where(jb == 0, 1, 0)
            cur = pl.ds(st, BAND)
            work.append((cur, first, (q0[0, cur, :], k0[0, cur, :], k0[0, pl.ds(sp, BAND), :],
                                      v0[0, cur, :], v0[0, pl.ds(sp, BAND), :]), load_state(cur)))
        outs = [_attn_plane(*args, lambda pr, first=first: bias2(0, first, pr), prev)
                for _, first, args, prev in work]
        for (cur, _, _, _), res in zip(work, outs):
            o_ref[cur, :LANES] = res[0][0].astype(bf16)
            o_ref[cur, LANES:] = res[1][0].astype(bf16)
        return c

    lax.fori_loop(0, seq // BAND // ATTN_BLOCKS_PER_STEP, body0, 0)


def _attn_prompt(planes, bias_tab, batch, seq):
    n = batch * seq
    ppk = PLANES_PER_KIND

    def spec(g, kind):
        return pl.BlockSpec((1, seq, LANES), lambda b, p: ((g * 3 + kind) * ppk + p, b, 0))

    in_specs = [spec(g, kind) for g in range(N_DGROUPS) for kind in range(3)]
    in_specs.append(pl.BlockSpec((N_DGROUPS, 2, 4, BAND, 2 * BAND), lambda b, p: (0, 0, p, 0, 0)))
    return pl.pallas_call(
        functools.partial(_attn_kernel, seq=seq),
        grid=(batch, ppk),
        in_specs=in_specs,
        out_specs=pl.BlockSpec((seq, 2 * LANES), lambda b, p: (b, p)),
        out_shape=jax.ShapeDtypeStruct((n, HD), bf16),
        scratch_shapes=[pltpu.VMEM((seq, LANES), f32)] * 4,
        compiler_params=_cparams(("arbitrary", "arbitrary")),
        name="attn_prompt",
    )(*([planes] * 9), bias_tab)


SAMPLE_BLOCK_BYTES = 4 * 1024 * 1024


def _sample_kv_kernel(c_ref, q_ref, kn_ref, vn_ref, knc_ref, vnc_ref, bias_ref, b0_ref,
                      o_ref, m_ref, l_ref, acc_ref, *, length):
    k = c_ref[0, 0, 0]
    v = c_ref[0, 0, 1]
    hb = k.shape[0]
    q = q_ref[0, 0]
    kn = kn_ref[0, 0]
    vn = vn_ref[0, 0]
    rb = lambda a: a.astype(bf16).astype(f32)
    q8 = jnp.broadcast_to(q[:, None, :], (hb, SUBLANES, HEAD_DIM)).astype(bf16)
    s = jnp.einsum("hqe,hel->hql", q8, k.astype(bf16), preferred_element_type=f32)[:, 0:1, :]
    s = s + bias_ref[...]
    s0 = jnp.sum(rb(kn) * rb(q), axis=-1, keepdims=True)[:, :, None] + b0_ref[...]
    m = jnp.maximum(jnp.max(s, axis=2, keepdims=True), s0)
    p = jnp.exp(s - m)
    p0 = jnp.exp(s0 - m)
    l = jnp.sum(p, axis=2, keepdims=True) + p0
    p8 = jnp.broadcast_to(p, (hb, SUBLANES, length)).astype(bf16)
    acc = jnp.einsum("hql,hel->hqe", p8, v.astype(bf16), preferred_element_type=f32)[:, 0, :]
    acc = acc + rb(p0[:, 0, :]) * rb(vn)
    m_ref[0, 0] = m[:, 0, :]
    l_ref[0, 0] = l[:, 0, :]
    acc_ref[0, 0] = acc
    last = lax.broadcasted_iota(i32, k.shape, 2) == length - 1
    o_ref[0, 0, 0] = jnp.where(last, knc_ref[0], pltpu.roll(k, length - 1, 2))
    o_ref[0, 0, 1] = jnp.where(last, vnc_ref[0], pltpu.roll(v, length - 1, 2))


def _sample_kv_call(cache_t, q, kn, vn, bias, b0, nb):
    length = cache_t.shape[-1]
    hb = max(1, min(N_HEADS, SAMPLE_BLOCK_BYTES // (2 * HEAD_DIM * length * 4)))
    assert N_HEADS % hb == 0
    nh = N_HEADS // hb
    row = pl.BlockSpec((1, 1, hb, HEAD_DIM), lambda b, h: (b, h, 0, 0))
    col = pl.BlockSpec((1, hb, HEAD_DIM, 1), lambda b, h: (b, h, 0, 0))
    stat = pl.BlockSpec((1, 1, hb, 1), lambda b, h: (b, h, 0, 0))
    cspec = pl.BlockSpec((1, 1, 2, hb, HEAD_DIM, length), lambda b, h: (0, b, 0, h, 0, 0))
    grp = lambda a: a.reshape(nb, nh, hb, HEAD_DIM)
    new_t, m, l, acc = pl.pallas_call(
        functools.partial(_sample_kv_kernel, length=length),
        grid=(nb, nh),
        in_specs=[cspec, row, row, row, col, col,
                  pl.BlockSpec((hb, 1, length), lambda b, h: (h, 0, 0)),
                  pl.BlockSpec((hb, 1, 1), lambda b, h: (h, 0, 0))],
        out_specs=[cspec, stat, stat, row],
        out_shape=[jax.ShapeDtypeStruct(cache_t.shape, f32),
                   jax.ShapeDtypeStruct((nb, nh, hb, 1), f32),
                   jax.ShapeDtypeStruct((nb, nh, hb, 1), f32),
                   jax.ShapeDtypeStruct((nb, nh, hb, HEAD_DIM), f32)],
        compiler_params=_cparams(("arbitrary", "arbitrary")),
        name="sample_kv_group",
    )(cache_t, grp(q), grp(kn), grp(vn), kn[..., None], vn[..., None], bias, b0)
    return new_t, m.reshape(nb, N_HEADS, 1), l.reshape(nb, N_HEADS, 1), acc.reshape(nb, N_HEADS, HEAD_DIM)


def _sample_merge_kernel(m_ref, l_ref, acc_ref, o_ref):
    m = m_ref[...]
    mm = jnp.max(m, axis=0)
    w = jnp.exp(m - mm[None])
    num = jnp.sum(w * acc_ref[...], axis=0)
    den = jnp.sum(w * l_ref[...], axis=0)
    o_ref[...] = num / den


def _sample_merge(ms, ls, accs):
    return pl.pallas_call(
        _sample_merge_kernel,
        out_shape=jax.ShapeDtypeStruct(accs.shape[1:], f32),
        name="sample_merge",
    )(ms, ls, accs)


def _route(u, wr_ref, br_ref):
    lg = _dot(u.astype(bf16), wr_ref[...]) + br_ref[...]
    rows = u.shape[0]
    lane = lax.broadcasted_iota(i32, (rows, LANES), 1)
    lanef = lane.astype(f32)
    big = jnp.float32(1e9)
    isg = lane < N_EGROUPS
    lgg = jnp.where(isg, lg, NEG)
    mg = jnp.max(lgg, axis=1, keepdims=True)
    eg = jnp.where(isg, jnp.exp(lgg - mg), 0.0)
    pg = eg / jnp.sum(eg, axis=1, keepdims=True)
    gw = jnp.max(pg, axis=1, keepdims=True)
    gi = jnp.min(jnp.where(isg & (pg == gw), lanef, big), axis=1, keepdims=True)
    grp_of_lane = ((lane - N_EGROUPS) >> 2).astype(f32)
    sel = (lane >= N_EGROUPS) & (lane < N_EGROUPS + N_EXPERTS) & (grp_of_lane == gi)
    les = jnp.where(sel, lg, NEG)
    v1 = jnp.max(les, axis=1, keepdims=True)
    i1 = jnp.min(jnp.where(sel & (les == v1), lanef, big), axis=1, keepdims=True)
    sel2 = sel & (lanef != i1)
    les2 = jnp.where(sel2, lg, NEG)
    v2 = jnp.max(les2, axis=1, keepdims=True)
    i2 = jnp.min(jnp.where(sel2 & (les2 == v2), lanef, big), axis=1, keepdims=True)
    ex = jnp.exp(v2 - v1)
    den = 1.0 + ex
    pw1 = (1.0 / den) * gw
    pw2 = (ex / den) * gw
    e1 = i1 - N_EGROUPS
    e2 = i2 - N_EGROUPS
    first_low = e1 < e2
    lo = jnp.minimum(e1, e2) - 4.0 * gi
    hi = jnp.maximum(e1, e2) - 4.0 * gi
    pair = jnp.where(lo == 0.0, hi - 1.0, jnp.where(lo == 1.0, hi + 1.0, 5.0))
    cls = gi * 6.0 + pair
    g_lo = jnp.where(first_low, pw1, pw2)
    g_hi = jnp.where(first_low, pw2, pw1)
    zero = jnp.zeros((rows, LANES), f32)
    return jnp.where(lane == 0, cls, jnp.where(lane == 1, g_lo, jnp.where(lane == 2, g_hi, zero)))


def _write_slab(slab_ref, u, meta):
    rows = u.shape[0]
    half = D_MODEL // 2
    for c in range(4):
        w = _pack_words(u[:, c * LANES:(c + 1) * LANES], u[:, half + c * LANES:half + (c + 1) * LANES])
        slab_ref[pl.ds(c, rows, stride=SUBLANES), :] = w
    slab_ref[pl.ds(4, rows, stride=SUBLANES), :] = lax.bitcast_convert_type(meta, u32)
    z = jnp.zeros((rows, LANES), u32)
    for c in range(5, SUBLANES):
        slab_ref[pl.ds(c, rows, stride=SUBLANES), :] = z


def _ffn_prep(h, gn_ref, wr_ref, br_ref, slab_ref, metat_ref):
    u = _rms(h, gn_ref[...])
    meta = _route(u, wr_ref, br_ref)
    _write_slab(slab_ref, u, meta)
    metat_ref[...] = meta.T[:SUBLANES, :]


def _from_slab_f32(ref, rows):
    return jnp.concatenate([ref[pl.ds(c, rows, stride=SUBLANES), :] for c in range(SUBLANES)], axis=1)


def _wo_kernel(o_ref, x_ref, wo_ref, gn_ref, wr_ref, br_ref, h_ref, slab_ref, metat_ref):
    h = x_ref[...] + _dot(o_ref[...], wo_ref[...])
    h_ref[...] = h
    _ffn_prep(h, gn_ref, wr_ref, br_ref, slab_ref, metat_ref)


def _router_specs():
    cmap = lambda i: (0, 0)
    return [pl.BlockSpec((1, D_MODEL), cmap),
            pl.BlockSpec((D_MODEL, LANES), cmap),
            pl.BlockSpec((1, LANES), cmap)]


def _prep_out(n, tm):
    specs = [pl.BlockSpec((tm, D_MODEL), lambda i: (i, 0)),
             pl.BlockSpec((tm * SUBLANES, LANES), lambda i: (i, 0)),
             pl.BlockSpec((SUBLANES, tm), lambda i: (0, i))]
    shapes = [jax.ShapeDtypeStruct((n, D_MODEL), f32),
              jax.ShapeDtypeStruct((n * SUBLANES, LANES), u32),
              jax.ShapeDtypeStruct((SUBLANES, n), f32)]
    return specs, shapes


def _wo_call(o_bf, x2, wo_bf, router, tm):
    n = x2.shape[0]
    ospecs, oshapes = _prep_out(n, tm)
    return pl.pallas_call(
        _wo_kernel,
        grid=(n // tm,),
        in_specs=[pl.BlockSpec((tm, HD), lambda i: (i, 0)),
                  pl.BlockSpec((tm, D_MODEL), lambda i: (i, 0)),
                  pl.BlockSpec((HD, D_MODEL), lambda i: (0, 0))] + _router_specs(),
        out_specs=ospecs,
        out_shape=oshapes,
        compiler_params=_cparams(("arbitrary",)),
        name="wo_residual",
    )(o_bf, x2, wo_bf, *router)


def _rank_kernel(metat_ref, rank_ref, cnt_ref, carry_ref, *, tr):
    @pl.when(pl.program_id(0) == 0)
    def _():
        carry_ref[...] = jnp.zeros_like(carry_ref)

    cls = metat_ref[0:1, :]
    sub = lax.broadcasted_iota(i32, (32, tr), 0).astype(f32)
    oh = jnp.where(sub == cls, 1.0, 0.0)
    rr = lax.broadcasted_iota(i32, (tr, tr), 0)
    cc = lax.broadcasted_iota(i32, (tr, tr), 1)
    upper = jnp.where(rr < cc, 1.0, 0.0).astype(bf16)
    pre = _dot(oh.astype(bf16), upper)
    carry = carry_ref[:, 0:1]
    rank_ref[...] = jnp.sum(oh * (pre + carry), axis=0, keepdims=True).astype(i32)
    newc = carry + jnp.sum(oh, axis=1, keepdims=True)
    carry_ref[...] = jnp.broadcast_to(newc, carry_ref.shape)
    cnt_ref[...] = carry_ref[...]


def _rank_call(metat, tr):
    n = metat.shape[1]
    return pl.pallas_call(
        functools.partial(_rank_kernel, tr=tr),
        grid=(n // tr,),
        in_specs=[pl.BlockSpec((SUBLANES, tr), lambda i: (0, i))],
        out_specs=[pl.BlockSpec((1, tr), lambda i: (0, i)),
                   pl.BlockSpec((32, LANES), lambda i: (0, 0))],
        out_shape=[jax.ShapeDtypeStruct((1, n), i32), jax.ShapeDtypeStruct((32, LANES), f32)],
        scratch_shapes=[pltpu.VMEM((32, LANES), f32)],
        compiler_params=_cparams(("arbitrary",)),
        name="class_rank",
    )(metat)


ROWS_PER_GATHER_STEP = 256


GATHER_UNROLL = 8


def _gather_kernel(idx_ref, src_ref, o_ref, sem, *, rows):
    base = pl.program_id(0) * rows

    def issue(kk, c):
        for u in range(GATHER_UNROLL):
            k = kk * GATHER_UNROLL + u
            t = idx_ref[base + k]
            pltpu.make_async_copy(
                src_ref.at[pl.ds(pl.multiple_of(t * SUBLANES, SUBLANES), SUBLANES)],
                o_ref.at[pl.ds(pl.multiple_of(k * SUBLANES, SUBLANES), SUBLANES)], sem).start()
        return c

    lax.fori_loop(0, rows // GATHER_UNROLL, issue, 0)
    pltpu.make_async_copy(src_ref.at[pl.ds(0, rows * SUBLANES)], o_ref, sem).wait()


def _gather_slabs(idx, src):
    n_out = idx.shape[0]
    rows = math.gcd(ROWS_PER_GATHER_STEP, n_out)
    assert rows % GATHER_UNROLL == 0 and src.shape[0] >= rows * SUBLANES
    return pl.pallas_call(
        functools.partial(_gather_kernel, rows=rows),
        grid_spec=pltpu.PrefetchScalarGridSpec(
            num_scalar_prefetch=1, grid=(n_out // rows,),
            in_specs=[pl.BlockSpec(memory_space=pl.ANY)],
            out_specs=pl.BlockSpec((rows * SUBLANES, LANES), lambda i, idx: (i, 0)),
            scratch_shapes=[pltpu.SemaphoreType.DMA(())]),
        out_shape=jax.ShapeDtypeStruct((n_out * SUBLANES, LANES), src.dtype),
        compiler_params=_cparams(("arbitrary",)),
        name="gather_slabs",
    )(idx, src)


def _start_slab_gather(idx_ref, base, chunks, src_ref, buf, sl, sems):
    def issue(kk, c):
        for u in range(GATHER_UNROLL):
            k = kk * GATHER_UNROLL + u
            t = idx_ref[base + k]
            pltpu.make_async_copy(
                src_ref.at[pl.ds(pl.multiple_of(t * SUBLANES, SUBLANES), SUBLANES)],
                buf.at[sl, pl.ds(pl.multiple_of(k * SUBLANES, SUBLANES), SUBLANES)],
                sems.at[sl]).start(priority=u % 2)
        return c

    lax.fori_loop(0, chunks, issue, 0)


def _wait_slab_gather(chunks, src_ref, buf, sl, sems):
    rows = GATHER_UNROLL * SUBLANES

    def drain(kk, c):
        pltpu.make_async_copy(src_ref.at[pl.ds(0, rows)], buf.at[sl, pl.ds(0, rows)], sems.at[sl]).wait()
        return c

    lax.fori_loop(0, chunks, drain, 0)


def _prefetched_slabs(idx_ref, src_ref, buf, sems, rows):
    i = pl.program_id(0)
    slot = i % 2

    chunks = rows // GATHER_UNROLL

    @pl.when(i == 0)
    def _():
        _start_slab_gather(idx_ref, 0, chunks, src_ref, buf, slot, sems)

    @pl.when(i + 1 < pl.num_programs(0))
    def _():
        _start_slab_gather(idx_ref, (i + 1) * rows, chunks, src_ref, buf, 1 - slot, sems)

    _wait_slab_gather(chunks, src_ref, buf, slot, sems)
    return buf.at[slot]


def _expert_kernel(e1_ref, e2_ref, chunks_ref, src_ref, slab_ref, wg1, wu1, wd1, wg2, wu2, wd2,
                   y_ref, xbuf, sems, *, te):
    del e1_ref, e2_ref
    i = pl.program_id(0)
    nt = pl.num_programs(0)
    slot = i % 2
    chunks = chunks_ref[i]
    nxt = jnp.minimum(i + 1, nt - 1)

    @pl.when(i == 0)
    def _():
        xbuf[...] = jnp.zeros_like(xbuf)
        _start_slab_gather(src_ref, 0, chunks, slab_ref, xbuf, slot, sems)

    @pl.when(i + 1 < nt)
    def _():
        _start_slab_gather(src_ref, (i + 1) * te, chunks_ref[nxt], slab_ref, xbuf, 1 - slot, sems)

    @pl.when(chunks == 0)
    def _():
        y_ref[...] = jnp.zeros_like(y_ref)

    @pl.when(chunks > 0)
    def _():
        _wait_slab_gather(chunks, slab_ref, xbuf, slot, sems)
        xs = xbuf.at[slot]
        los, his = [], []
        for c in range(4):
            lo, hi = _unpack_words(xs[pl.ds(c, te, stride=SUBLANES), :])
            los.append(lo)
            his.append(hi)
        x = jnp.concatenate(los + his, axis=1).astype(bf16)
        meta = lax.bitcast_convert_type(xs[pl.ds(4, te, stride=SUBLANES), :], f32)
        y = None
        for wg, wu, wd, lane in ((wg1, wu1, wd1, 1), (wg2, wu2, wd2, 2)):
            a = _dot(x, wg[0])
            c = _dot(x, wu[0])
            hh = (a * (1.0 / (1.0 + jnp.exp(-a)))) * c * meta[:, lane:lane + 1]
            part = _dot(hh.astype(bf16), wd[0])
            y = part if y is None else y + part
        for c in range(SUBLANES):
            y_ref[pl.ds(c, te, stride=SUBLANES), :] = y[:, c * LANES:(c + 1) * LANES]


def _expert_call(te1, te2, tchunks, src, slab, wg, wu, wd, te):
    ntiles = te1.shape[0]
    assert te % GATHER_UNROLL == 0 and slab.shape[0] >= te * SUBLANES

    def wspec(shape, which):
        if which == 1:
            return pl.BlockSpec(shape, lambda i, e1, e2, v, s: (e1[i], 0, 0))
        return pl.BlockSpec(shape, lambda i, e1, e2, v, s: (e2[i], 0, 0))

    gs = (1, D_MODEL, D_EXPERT)
    ds_ = (1, D_EXPERT, D_MODEL)
    return pl.pallas_call(
        functools.partial(_expert_kernel, te=te),
        grid_spec=pltpu.PrefetchScalarGridSpec(
            num_scalar_prefetch=4, grid=(ntiles,),
            in_specs=[pl.BlockSpec(memory_space=pl.ANY),
                      wspec(gs, 1), wspec(gs, 1), wspec(ds_, 1),
                      wspec(gs, 2), wspec(gs, 2), wspec(ds_, 2)],
            out_specs=pl.BlockSpec((te * SUBLANES, LANES), lambda i, e1, e2, v, s: (i, 0)),
            scratch_shapes=[pltpu.VMEM((2, te * SUBLANES, LANES), u32),
                            pltpu.SemaphoreType.DMA((2,))]),
        out_shape=jax.ShapeDtypeStruct((ntiles * te * SUBLANES, LANES), f32),
        compiler_params=_cparams(("arbitrary",)),
        name="expert_pair_ffn",
    )(te1, te2, tchunks, src, slab, wg, wu, wd, wg, wu, wd)


def _invert_kernel(pos_ref, src_ref, *, n, slots):
    def clear(kk, c):
        for u in range(GATHER_UNROLL):
            src_ref[kk * GATHER_UNROLL + u] = 0
        return c

    def place(kk, c):
        for u in range(GATHER_UNROLL):
            t = kk * GATHER_UNROLL + u
            src_ref[pos_ref[t]] = t
        return c

    lax.fori_loop(0, slots // GATHER_UNROLL, clear, 0)
    lax.fori_loop(0, n // GATHER_UNROLL, place, 0)


def _invert_slots(pos, slots):
    n = pos.shape[0]
    assert n % GATHER_UNROLL == 0 and slots % GATHER_UNROLL == 0
    return pl.pallas_call(
        functools.partial(_invert_kernel, n=n, slots=slots),
        in_specs=[pl.BlockSpec(memory_space=pltpu.SMEM)],
        out_specs=pl.BlockSpec(memory_space=pltpu.SMEM),
        out_shape=jax.ShapeDtypeStruct((slots,), i32),
        name="invert_slots",
    )(pos)


def _moe(slab, metat, wg, wu, wd, layer, te, tr):
    n = metat.shape[1]
    rank, cnt = _rank_call(metat, tr)
    counts = cnt[:N_CLASSES, 0].astype(i32)
    padded = ((counts + te - 1) // te) * te
    ends = jnp.cumsum(padded)
    off = ends - padded
    cls = metat[0].astype(i32)
    class_ids = jnp.arange(N_CLASSES, dtype=i32)
    pos = rank[0] + jnp.sum(jnp.where(cls[:, None] == class_ids[None, :], off[None, :], 0), axis=1)
    ntiles = -(-n // te) + N_CLASSES
    tstart = jnp.arange(ntiles, dtype=i32) * te
    tcls = jnp.minimum(jnp.sum((tstart[:, None] >= ends[None, :]).astype(i32), axis=1), N_CLASSES - 1)
    real_end = (off + counts)[tcls]
    tchunks = jnp.where(tstart < ends[-1], (jnp.clip(real_end - tstart, 0, te) + GATHER_UNROLL - 1) // GATHER_UNROLL, 0)
    tchunks = tchunks.astype(i32)
    te1 = jnp.asarray(_CLASS_E1 + layer * N_EXPERTS)[tcls]
    te2 = jnp.asarray(_CLASS_E2 + layer * N_EXPERTS)[tcls]
    src = _invert_slots(pos, ntiles * te)
    return pos, _expert_call(te1, te2, tchunks, src, slab, wg, wu, wd, te)


def _pool_mix_tail(hf, up, ext, tpos0, pw_ref, pb_ref, ps_ref):
    rows = up.shape[0]
    tpos = tpos0 + lax.broadcasted_iota(i32, (rows, 1), 0)
    parts = []
    for gi, win in enumerate(POOL_WINDOWS):
        sl = slice(gi * POOL_CH, (gi + 1) * POOL_CH)
        s = ext[:, sl]
        for kk in range(gi + 1):
            s = s + pltpu.roll(s, 1 << kk, 0)
        cnt = jnp.minimum(win, tpos + 1).astype(f32)
        z = s[16:, :] / cnt - up[:, sl]
        parts.append(_dot(z.astype(bf16), pw_ref[gi]))
    y = (jnp.concatenate(parts, axis=1) + pb_ref[...]) * ps_ref[...]
    return hf + y


def _pool_kernel(pos_ref, h_ref, ys_ref, pn_ref, pw_ref, pb_ref, ps_ref, gn_ref, wr_ref, br_ref,
                 h2_ref, slab_ref, metat_ref, st_ref, ybuf, sems, halo_ref, *, tm, tps):
    r = pl.program_id(0) % tps

    @pl.when(pl.program_id(0) == 0)
    def _():
        halo_ref[...] = jnp.zeros_like(halo_ref)

    y = _prefetched_slabs(pos_ref, ys_ref, ybuf, sems, tm)
    hf = h_ref[...] + _from_slab_f32(y, tm)
    up = _rms(hf, pn_ref[...])
    uph = jnp.where(r > 0, halo_ref[...], 0.0)
    ext = jnp.concatenate([uph, up], axis=0)
    h2 = _pool_mix_tail(hf, up, ext, r * tm, pw_ref, pb_ref, ps_ref)
    h2_ref[...] = h2
    st_ref[...] = up[tm - 16:, :]
    halo_ref[...] = up[tm - 16:, :]
    _ffn_prep(h2, gn_ref, wr_ref, br_ref, slab_ref, metat_ref)


def _pool_call(h, pos, ys, pn, pw_bf, pb, psc, router, batch, seq, tm):
    n = h.shape[0]
    tps = seq // tm
    assert tm % GATHER_UNROLL == 0 and ys.shape[0] >= tm * SUBLANES
    c2 = lambda i, p: (0, 0)
    return pl.pallas_call(
        functools.partial(_pool_kernel, tm=tm, tps=tps),
        grid_spec=pltpu.PrefetchScalarGridSpec(
            num_scalar_prefetch=1, grid=(n // tm,),
            in_specs=[pl.BlockSpec((tm, D_MODEL), lambda i, p: (i, 0)),
                      pl.BlockSpec(memory_space=pl.ANY),
                      pl.BlockSpec((1, D_MODEL), c2),
                      pl.BlockSpec((4, POOL_CH, POOL_CH), lambda i, p: (0, 0, 0)),
                      pl.BlockSpec((1, D_MODEL), c2),
                      pl.BlockSpec((1, D_MODEL), c2),
                      pl.BlockSpec((1, D_MODEL), c2),
                      pl.BlockSpec((D_MODEL, LANES), c2),
                      pl.BlockSpec((1, LANES), c2)],
            out_specs=[pl.BlockSpec((tm, D_MODEL), lambda i, p: (i, 0)),
                       pl.BlockSpec((tm * SUBLANES, LANES), lambda i, p: (i, 0)),
                       pl.BlockSpec((SUBLANES, tm), lambda i, p: (0, i)),
                       pl.BlockSpec((16, D_MODEL), lambda i, p: (i // tps, 0))],
            scratch_shapes=[pltpu.VMEM((2, tm * SUBLANES, LANES), f32),
                            pltpu.SemaphoreType.DMA((2,)),
                            pltpu.VMEM((16, D_MODEL), f32)]),
        out_shape=[jax.ShapeDtypeStruct((n, D_MODEL), f32),
                   jax.ShapeDtypeStruct((n * SUBLANES, LANES), u32),
                   jax.ShapeDtypeStruct((SUBLANES, n), f32),
                   jax.ShapeDtypeStruct((batch * 16, D_MODEL), f32)],
        compiler_params=_cparams(("arbitrary",)),
        name="pool_mixer",
    )(pos, h, ys, pn, pw_bf, pb, psc, *router)


def _pool_sample_kernel(h_ref, y_ref, st_ref, pn_ref, pw_ref, pb_ref, ps_ref,
                        gn_ref, wr_ref, br_ref, h2_ref, slab_ref, metat_ref, nst_ref, *, nb):
    rows = h_ref.shape[0]
    hf = h_ref[...] + _from_slab_f32(y_ref, rows)
    up = _rms(hf, pn_ref[...])
    st = st_ref[...]
    ub = up[:nb, :]
    parts = []
    for gi, win in enumerate(POOL_WINDOWS):
        sl = slice(gi * POOL_CH, (gi + 1) * POOL_CH)
        tot = ub[:, sl] + jnp.sum(st[:, POOL_STATE - (win - 1):, sl], axis=1)
        z = tot / float(win) - ub[:, sl]
        parts.append(_dot(z.astype(bf16), pw_ref[gi]))
    y = (jnp.concatenate(parts, axis=1) + pb_ref[...]) * ps_ref[...]
    if rows > nb:
        y = jnp.concatenate([y, jnp.zeros((rows - nb, D_MODEL), f32)], axis=0)
    h2 = hf + y
    h2_ref[...] = h2
    nst_ref[:, :POOL_STATE - 1, :] = st[:, 1:, :]
    nst_ref[:, POOL_STATE - 1, :] = ub
    _ffn_prep(h2, gn_ref, wr_ref, br_ref, slab_ref, metat_ref)


def _pool_sample_call(h, ytt, state, pn, pw_bf, pb, psc, router, nb):
    rows = h.shape[0]
    ospecs, oshapes = _prep_out(rows, rows)
    ospecs.append(pl.BlockSpec((nb, POOL_STATE, D_MODEL), lambda i: (0, 0, 0)))
    oshapes.append(jax.ShapeDtypeStruct((nb, POOL_STATE, D_MODEL), f32))
    cm = lambda i: (0, 0)
    return pl.pallas_call(
        functools.partial(_pool_sample_kernel, nb=nb),
        grid=(1,),
        in_specs=[pl.BlockSpec((rows, D_MODEL), cm),
                  pl.BlockSpec((rows * SUBLANES, LANES), cm),
                  pl.BlockSpec((nb, POOL_STATE, D_MODEL), lambda i: (0, 0, 0)),
                  pl.BlockSpec((1, D_MODEL), cm),
                  pl.BlockSpec((4, POOL_CH, POOL_CH), lambda i: (0, 0, 0)),
                  pl.BlockSpec((1, D_MODEL), cm),
                  pl.BlockSpec((1, D_MODEL), cm)] + _router_specs(),
        out_specs=ospecs,
        out_shape=oshapes,
        compiler_params=_cparams(("arbitrary",)),
        name="pool_mixer_sample",
    )(h, ytt, state, pn, pw_bf, pb, psc, *router)


def _final_kernel(h_ref, y_ref, o_ref, *, tm):
    o_ref[...] = h_ref[...] + _from_slab_f32(y_ref, tm)


def _final_call(h, ytt, tm):
    n = h.shape[0]
    return pl.pallas_call(
        functools.partial(_final_kernel, tm=tm),
        grid=(n // tm,),
        in_specs=[pl.BlockSpec((tm, D_MODEL), lambda i: (i, 0)),
                  pl.BlockSpec((tm * SUBLANES, LANES), lambda i: (i, 0))],
        out_specs=pl.BlockSpec((tm, D_MODEL), lambda i: (i, 0)),
        out_shape=jax.ShapeDtypeStruct((n, D_MODEL), f32),
        compiler_params=_cparams(("arbitrary",)),
        name="final_residual",
    )(h, ytt)


def _final_gather_kernel(pos_ref, h_ref, ys_ref, o_ref, ybuf, sems, *, tm):
    y = _prefetched_slabs(pos_ref, ys_ref, ybuf, sems, tm)
    o_ref[...] = h_ref[...] + _from_slab_f32(y, tm)


def _final_gather_call(h, pos, ys, tm):
    n = h.shape[0]
    assert tm % GATHER_UNROLL == 0 and ys.shape[0] >= tm * SUBLANES
    return pl.pallas_call(
        functools.partial(_final_gather_kernel, tm=tm),
        grid_spec=pltpu.PrefetchScalarGridSpec(
            num_scalar_prefetch=1, grid=(n // tm,),
            in_specs=[pl.BlockSpec((tm, D_MODEL), lambda i, p: (i, 0)),
                      pl.BlockSpec(memory_space=pl.ANY)],
            out_specs=pl.BlockSpec((tm, D_MODEL), lambda i, p: (i, 0)),
            scratch_shapes=[pltpu.VMEM((2, tm * SUBLANES, LANES), f32),
                            pltpu.SemaphoreType.DMA((2,))]),
        out_shape=jax.ShapeDtypeStruct((n, D_MODEL), f32),
        compiler_params=_cparams(("arbitrary",)),
        name="final_residual_gather",
    )(pos, h, ys)


def _t5_bucket(dist):
    max_exact = N_BUCKETS // 2
    df = jnp.maximum(dist, max_exact).astype(f32)
    large = max_exact + (jnp.log(df / max_exact) / math.log(MAX_DISTANCE / max_exact)
                         * (N_BUCKETS - max_exact)).astype(i32)
    large = jnp.minimum(large, N_BUCKETS - 1)
    return jnp.where(dist < max_exact, dist, large)


def _prompt_bias_tables(rel_bias):
    qi = jnp.arange(BAND)[:, None]
    kj = jnp.arange(2 * BAND)[None, :]
    step = BAND + qi - kj
    tabs = []
    for g, (win, dil) in enumerate(zip(WINDOWS, DILATIONS)):
        n_step = win // dil
        tab = rel_bias[:, g * N_HEADS:(g + 1) * N_HEADS].astype(f32)
        svals = jnp.maximum(2 * BAND - 1 - jnp.arange(3 * BAND), 0)
        bvec = tab[_t5_bucket(svals * dil)].T
        bias = jnp.stack([bvec[:, BAND - 1 - q:3 * BAND - 1 - q] for q in range(BAND)], axis=1)
        band = (step >= 0) & (step <= n_step)
        reg = jnp.where(band[None], bias, NEG)
        first = jnp.where((band & (kj >= BAND))[None], bias, NEG)
        tabs.append(jnp.stack([reg, first]))
    return jnp.stack(tabs)


def _sample_bias_tables(rel_bias, lengths):
    bs, b0 = [], []
    for g, (win, dil) in enumerate(zip(WINDOWS, DILATIONS)):
        length = lengths[g]
        n_step = win // dil
        tab = rel_bias[:, g * N_HEADS:(g + 1) * N_HEADS].astype(f32)
        dist = length - jnp.arange(length)
        hit = (dist % dil == 0) & (dist <= n_step * dil)
        b = jnp.where(hit[:, None], tab[_t5_bucket(dist)], NEG)
        bs.append(b.T[:, None, :])
        b0.append(tab[_t5_bucket(jnp.zeros((1,), i32))].T[:, :, None])
    return bs, b0


def _router_weights(ffn_norm_i, wgr, bgr, wer, ber):
    w = jnp.zeros((D_MODEL, LANES), f32)
    w = w.at[:, :N_EGROUPS].set(wgr).at[:, N_EGROUPS:N_EGROUPS + N_EXPERTS].set(wer)
    b = jnp.zeros((1, LANES), f32)
    b = b.at[0, :N_EGROUPS].set(bgr).at[0, N_EGROUPS:N_EGROUPS + N_EXPERTS].set(ber)
    return ffn_norm_i.reshape(1, D_MODEL), w.astype(bf16), b


def _norm_mats():
    head_of_col = np.arange(HD) // HEAD_DIM
    s = (head_of_col[:, None] == np.arange(LANES)[None, :]).astype(np.float32)
    e = s.T
    return jnp.asarray(s, bf16), jnp.asarray(np.concatenate([e, e], axis=0), bf16)


def kernel(x_prompt, x_sample, cache_kv_w128, cache_kv_w512, cache_kv_w2048, state_pool, w_qkv, q_norm, k_norm, w_o, rel_bias, attn_norm, pool_norm, pool_w, pool_b, pool_scale, ffn_norm, router_group_w, router_group_b, router_expert_w, router_expert_b, w_gate, w_up, w_down):
    batch, seq, _ = x_prompt.shape
    nb = x_sample.shape[0]
    assert x_sample.shape[1] == 1
    n = batch * seq
    srows = LANES
    caches6 = (cache_kv_w128, cache_kv_w512, cache_kv_w2048)

    smat, e2mat = _norm_mats()
    wqkv_bf = w_qkv[0].astype(bf16)
    wo_bf = w_o[0].astype(bf16)
    gq = jnp.tile(q_norm[0], (1, N_HEADS)).reshape(N_DGROUPS, 1, HD)
    gk = jnp.tile(k_norm[0], (1, N_HEADS)).reshape(N_DGROUPS, 1, HD)
    an = attn_norm[0].reshape(1, D_MODEL)
    routers = [_router_weights(ffn_norm[i], router_group_w[i], router_group_b[i],
                               router_expert_w[i], router_expert_b[i]) for i in range(2)]
    allx = lambda w: w.astype(bf16).reshape((-1,) + w.shape[2:])
    wg_bf, wu_bf, wd_bf = allx(w_gate), allx(w_up), allx(w_down)
    pw_bf = pool_w[0].astype(bf16)
    pn = pool_norm[0].reshape(1, D_MODEL)
    pb = pool_b[0].reshape(1, D_MODEL)
    psc = pool_scale[0].reshape(1, D_MODEL)

    x2 = x_prompt.reshape(n, D_MODEL)
    planes, kv2, kv1, kv0 = _qkv_prompt(x2, an, wqkv_bf, gq, gk, smat, e2mat, batch, seq)
    o_bf = _attn_prompt(planes, _prompt_bias_tables(rel_bias), batch, seq)
    h0, slab0, metat0 = _wo_call(o_bf, x2, wo_bf, routers[0], tm=512)
    pos0, ys0 = _moe(slab0, metat0, wg_bf, wu_bf, wd_bf, 0, te=256, tr=512)
    h1, slab1, metat1, pst = _pool_call(h0, pos0, ys0, pn, pw_bf, pb, psc, routers[1], batch, seq, tm=512)
    pos1, ys1 = _moe(slab1, metat1, wg_bf, wu_bf, wd_bf, 1, te=256, tr=512)
    y_prompt = _final_gather_call(h1, pos1, ys1, tm=512).reshape(batch, seq, D_MODEL)

    new_kv_p = [jnp.transpose(t, (0, 1, 5, 2, 3, 4)) for t in (kv0, kv1, kv2)]
    new_pool_p = pst.reshape(batch, 16, D_MODEL)[:, 16 - POOL_STATE:][None]

    xs = jnp.zeros((srows, D_MODEL), f32).at[:nb].set(x_sample[:, 0, :])
    qs, ks, vs = _qkv_sample(xs, an, wqkv_bf, gq, gk, smat, e2mat)
    to4 = lambda a: a[:, :nb].reshape(N_DGROUPS, nb, N_HEADS, HEAD_DIM)
    qs4, ks4, vs4 = to4(qs), to4(ks), to4(vs)
    bias_s, bias_s0 = _sample_bias_tables(rel_bias, [c.shape[2] for c in caches6])
    new_kv_s, ms, ls, accs = [], [], [], []
    for g in range(N_DGROUPS):
        cache_t = jnp.transpose(caches6[g], (0, 1, 3, 4, 5, 2))
        new_t, m_g, l_g, acc_g = _sample_kv_call(cache_t, qs4[g], ks4[g], vs4[g], bias_s[g], bias_s0[g], nb)
        new_kv_s.append(jnp.transpose(new_t, (0, 1, 5, 2, 3, 4)))
        ms.append(m_g)
        ls.append(l_g)
        accs.append(acc_g)
    os_ = _sample_merge(jnp.stack(ms), jnp.stack(ls), jnp.stack(accs))
    os_pad = jnp.zeros((srows, HD), bf16).at[:nb].set(os_.reshape(nb, HD).astype(bf16))
    hs0, sslab0, smetat0 = _wo_call(os_pad, xs, wo_bf, routers[0], tm=srows)
    ystt0 = _gather_slabs(*_moe(sslab0, smetat0, wg_bf, wu_bf, wd_bf, 0, te=64, tr=srows))
    hs1, sslab1, smetat1, new_pool_s = _pool_sample_call(
        hs0, ystt0, state_pool[0], pn, pw_bf, pb, psc, routers[1], nb)
    ystt1 = _gather_slabs(*_moe(sslab1, smetat1, wg_bf, wu_bf, wd_bf, 1, te=64, tr=srows))
    y_sample = _final_call(hs1, ystt1, tm=srows)[:nb].reshape(nb, 1, D_MODEL)

    return (y_prompt, y_sample, new_kv_p[0], new_kv_p[1], new_kv_p[2], new_pool_p,
            new_kv_s[0], new_kv_s[1], new_kv_s[2], new_pool_s[None])
```
